```python
import math
import jax, jax.numpy as jnp
from jax import lax
import numpy as np

D_MODEL = 2048
BATCH = 2
SEQ = 16384
DEPTH = 2

HEAD_DIM = 128
DIL_GROUPS = ((128, 1), (512, 4), (2048, 16))
HEADS_PER_DIL_GROUP = 4
N_HEADS_A = HEADS_PER_DIL_GROUP * len(DIL_GROUPS)
N_HEADS_B = 4
WIDTH_A = N_HEADS_A * HEAD_DIM
WIDTH_B = N_HEADS_B * HEAD_DIM
A_OUT = HEADS_PER_DIL_GROUP * HEAD_DIM
D_IN = 3 * WIDTH_A + 3 * WIDTH_B + 2 * D_MODEL
D_FF = 5632
CONV_WIDTH = 3
PLE_DIM = 256
BLOCK = 128
RMS_EPS = 1e-6
ALIBI_MAX = 8.0

kernel_name = 'hybrid_dilated_stickbreak_convffn'


def rmsnorm(x, g):
    xf = x.astype(jnp.float32)
    y = xf * lax.rsqrt(jnp.mean(xf * xf, axis=-1, keepdims=True) + RMS_EPS)
    return (y * g.astype(jnp.float32)).astype(x.dtype)


def alibi_slopes(n):
    return jnp.exp2(-ALIBI_MAX * jnp.arange(1, n + 1, dtype=jnp.float32) / n)


def dilated_window_group(q, k, v, window, dilation, slopes):
    b, s, hg, dh = q.shape
    n_back = window // dilation
    span = dilation * BLOCK
    s_pad = -(-s // span) * span
    L = s_pad // dilation
    nb = L // BLOCK

    def to_blocks(t):
        t = jnp.pad(t, ((0, 0), (0, s_pad - s), (0, 0), (0, 0)))
        t = t.reshape(b, L, dilation, hg, dh).transpose(0, 2, 3, 1, 4)
        return t.reshape(b, dilation, hg, nb, BLOCK, dh)

    def with_prev(t):
        prev = jnp.pad(t, ((0, 0), (0, 0), (0, 0), (1, 0), (0, 0), (0, 0)))[:, :, :, :-1]
        return jnp.concatenate([prev, t], axis=4)

    qb = to_blocks(q)
    kk = with_prev(to_blocks(k))
    vv = with_prev(to_blocks(v))
    scores = jnp.einsum('brhnqe,brhnke->brhnqk', qb, kk).astype(jnp.float32) * (1.0 / math.sqrt(dh))
    qi = jnp.arange(BLOCK)[:, None]
    ki = jnp.arange(2 * BLOCK)[None, :]
    steps = BLOCK + qi - ki
    valid = (steps >= 0) & (steps <= n_back)
    first = (jnp.arange(nb) == 0)[:, None, None]
    valid = valid[None] & ~(first & (ki < BLOCK)[None])
    bias = -slopes[:, None, None] * (steps * dilation).astype(jnp.float32)[None]
    scores = scores + bias[None, None, :, None]
    scores = jnp.where(valid[None, None, None], scores, -jnp.inf)
    lse = jax.nn.logsumexp(scores, axis=-1)
    probs = jnp.exp(scores - lse[..., None])
    out = jnp.einsum('brhnqk,brhnke->brhnqe', probs.astype(v.dtype), vv)

    def from_blocks(t):
        t = t.reshape(b, dilation, hg, L, *t.shape[5:])
        t = jnp.moveaxis(t, 3, 1)
        return t.reshape(b, s_pad, hg, *t.shape[4:])[:, :s]

    return from_blocks(out), from_blocks(lse)


def dilated_mixture_attention(q, k, v):
    b, s = q.shape[:2]
    slopes = alibi_slopes(N_HEADS_A)
    outs, lses = [], []
    for g, (window, dilation) in enumerate(DIL_GROUPS):
        sl = slice(g * HEADS_PER_DIL_GROUP, (g + 1) * HEADS_PER_DIL_GROUP)
        o, l = dilated_window_group(q[:, :, sl], k[:, :, sl], v[:, :, sl], window, dilation, slopes[sl])
        outs.append(o)
        lses.append(l)
    alpha = jax.nn.softmax(jnp.stack(lses), axis=0)
    y = jnp.sum(alpha[..., None] * jnp.stack(outs).astype(jnp.float32), axis=0)
    return y.reshape(b, s, A_OUT).astype(q.dtype)


def stick_breaking_attention(q, k, v):
    b, s, h, dh = q.shape
    nb = s // BLOCK
    qt = q.transpose(0, 2, 1, 3)
    kt = k.transpose(0, 2, 1, 3)
    vt = v.transpose(0, 2, 1, 3)
    scale = 1.0 / math.sqrt(dh)
    idx = jnp.arange(BLOCK)
    tri_within = (idx[:, None] > idx[None, :]).astype(jnp.float32)
    diag_before = idx[None, :] < idx[:, None]
    outs = []
    for n in range(nb):
        c = n + 1
        kl = c * BLOCK
        qblk = qt[:, :, n * BLOCK:(n + 1) * BLOCK]
        z = jnp.einsum('bhqe,bhke->bhqk', qblk, kt[:, :, :kl]).astype(jnp.float32) * scale
        before = jnp.concatenate([jnp.ones((BLOCK, n * BLOCK), dtype=bool), diag_before], axis=1)
        ln = jnp.where(before, jax.nn.log_sigmoid(-z), 0.0)
        lnc = ln.reshape(b, h, BLOCK, c, BLOCK)
        within = jnp.einsum('bhqcj,ji->bhqci', lnc, tri_within)
        tot = jnp.sum(lnc, axis=-1)
        cidx = jnp.arange(c)
        after = jnp.einsum('bhqc,cd->bhqd', tot, (cidx[:, None] > cidx[None, :]).astype(jnp.float32))
        logw = z.reshape(b, h, BLOCK, c, BLOCK) + lnc + within + after[..., None]
        w = jnp.where(before, jnp.exp(logw.reshape(b, h, BLOCK, kl)), 0.0)
        outs.append(jnp.einsum('bhqk,bhke->bhqe', w.astype(v.dtype), vt[:, :, :kl]))
    out = jnp.concatenate(outs, axis=2)
    return out.transpose(0, 2, 1, 3).reshape(b, s, h * dh)


def causal_depthwise_conv(x, w, bias):
    s = x.shape[1]
    xp = jnp.pad(x, ((0, 0), (CONV_WIDTH - 1, 0), (0, 0)))
    y = bias + w[0] * xp[:, 0:s]
    for j in range(1, CONV_WIDTH):
        y = y + w[j] * xp[:, j:j + s]
    return y


def setup_inputs(seed: int = 0) -> dict:
    key = jax.random.key(seed)
    ks = jax.random.split(key, 20)
    f32 = jnp.float32

    def nrm(k, shape, scale):
        return jax.random.normal(k, shape, f32) * scale

    def gain(k):
        return 1.0 + 0.05 * jax.random.normal(k, (DEPTH, D_MODEL), f32)

    return {
        'x': nrm(ks[0], (BATCH, SEQ, D_MODEL), 1.0),
        'p': nrm(ks[1], (DEPTH, BATCH, SEQ, PLE_DIM), 1.0),
        'g_mix_pre': gain(ks[2]),
        'w_in': nrm(ks[3], (DEPTH, D_MODEL, D_IN), D_MODEL ** -0.5),
        'w_branch_a': nrm(ks[4], (DEPTH, A_OUT, D_MODEL), A_OUT ** -0.5),
        'w_branch_b': nrm(ks[5], (DEPTH, WIDTH_B, D_MODEL), WIDTH_B ** -0.5),
        'w_out': nrm(ks[6], (DEPTH, D_MODEL, D_MODEL), D_MODEL ** -0.5),
        'g_mix_post': gain(ks[7]),
        'g_ffn_pre': gain(ks[8]),
        'w_up': nrm(ks[9], (DEPTH, D_MODEL, 2 * D_FF), D_MODEL ** -0.5),
        'conv_w': nrm(ks[10], (DEPTH, CONV_WIDTH, 2 * D_FF), CONV_WIDTH ** -0.5),
        'conv_b': nrm(ks[11], (DEPTH, 2 * D_FF), 0.01),
        'w_down': nrm(ks[12], (DEPTH, D_FF, D_MODEL), D_FF ** -0.5),
        'g_ffn_post': gain(ks[13]),
        'w_ple_in': nrm(ks[14], (DEPTH, PLE_DIM, D_MODEL), PLE_DIM ** -0.5),
        'w_ple_gate': nrm(ks[15], (DEPTH, D_MODEL, D_MODEL), D_MODEL ** -0.5),
    }


def reference(x, p, g_mix_pre, w_in, w_branch_a, w_branch_b, w_out, g_mix_post, g_ffn_pre, w_up, conv_w, conv_b, w_down, g_ffn_post, w_ple_in, w_ple_gate):
    b, s, _ = x.shape
    splits = [WIDTH_A, 2 * WIDTH_A, 3 * WIDTH_A,
              3 * WIDTH_A + WIDTH_B, 3 * WIDTH_A + 2 * WIDTH_B, 3 * WIDTH_A + 3 * WIDTH_B,
              3 * WIDTH_A + 3 * WIDTH_B + D_MODEL]
    h = x
    for i in range(DEPTH):
        u = rmsnorm(h, g_mix_pre[i])
        proj = u @ w_in[i]
        qa, ka, va, qb, kb, vb, gla, glb = jnp.split(proj, splits, axis=-1)
        heads_a = lambda t: t.reshape(b, s, N_HEADS_A, HEAD_DIM)
        heads_b = lambda t: t.reshape(b, s, N_HEADS_B, HEAD_DIM)
        ya = dilated_mixture_attention(heads_a(qa), heads_a(ka), heads_a(va)) @ w_branch_a[i]
        yb = stick_breaking_attention(heads_b(qb), heads_b(kb), heads_b(vb)) @ w_branch_b[i]
        gate_a = jax.nn.sigmoid(gla.astype(jnp.float32)).astype(h.dtype)
        gate_b = jax.nn.sigmoid(glb.astype(jnp.float32)).astype(h.dtype)
        mix = (gate_a * ya + gate_b * yb) @ w_out[i]
        h = h + rmsnorm(mix, g_mix_post[i])
        u = rmsnorm(h, g_ffn_pre[i])
        up = causal_depthwise_conv(u @ w_up[i], conv_w[i], conv_b[i])
        gate_ff, value = jnp.split(up, [D_FF], axis=-1)
        f = (jax.nn.gelu(gate_ff, approximate=True) * value) @ w_down[i]
        h = h + rmsnorm(f, g_ffn_post[i])
        e = p[i] @ w_ple_in[i]
        h = h + jax.nn.sigmoid((h @ w_ple_gate[i]).astype(jnp.float32)).astype(h.dtype) * e
    return h
```

```python
import functools
import math

import jax
import jax.numpy as jnp
from jax import lax
from jax.experimental import pallas as pl
from jax.experimental.pallas import tpu as pltpu

HEAD_DIM = 128
DIL_GROUPS = ((128, 1), (512, 4), (2048, 16))
HEADS_PER_GROUP = 4
N_HEADS_A = HEADS_PER_GROUP * len(DIL_GROUPS)
N_HEADS_B = 4
WIDTH_A = N_HEADS_A * HEAD_DIM
WIDTH_B = N_HEADS_B * HEAD_DIM
A_OUT = HEADS_PER_GROUP * HEAD_DIM
QKV_WIDTH = 3 * WIDTH_A + 3 * WIDTH_B
CONV_WIDTH = 3
RMS_EPS = 1e-6
ALIBI_MAX = 8.0
MASKED = -1e30
QBLK = 128
SPAN = 2048
CONV_HALO = 16
VMEM_LIMIT = 56 * 1024 * 1024

_F32 = jnp.float32
_BF16 = jnp.bfloat16


def _rmsnorm_rows(x, g):
    ms = jnp.mean(x * x, axis=-1, keepdims=True)
    return x * lax.rsqrt(ms + RMS_EPS) * g


def _params(sem):
    return pltpu.CompilerParams(dimension_semantics=sem, vmem_limit_bytes=VMEM_LIMIT)


def _in_proj_kernel(x_ref, g_ref, cs_ref, w_ref, o_ref, u_ref, *, gate_tile0, chunk):
    j = pl.program_id(1)
    tm = x_ref.shape[0]

    @pl.when(j == 0)
    def _():
        def body(c, carry):
            r0 = pl.multiple_of(c * chunk, chunk)
            x = x_ref[pl.ds(r0, chunk), :]
            u_ref[pl.ds(r0, chunk), :] = _rmsnorm_rows(x, g_ref[...]).astype(u_ref.dtype)
            return carry
        lax.fori_loop(0, tm // chunk, body, 0)

    acc = jnp.dot(u_ref[...], w_ref[...], preferred_element_type=_F32)

    @pl.when(j < gate_tile0)
    def _():
        o_ref[...] = (acc * cs_ref[...]).astype(o_ref.dtype)

    @pl.when(j >= gate_tile0)
    def _():
        o_ref[...] = jax.nn.sigmoid(acc).astype(o_ref.dtype)


def _in_proj(h2d, g, colscale, w, *, tm, tn):
    t, d = h2d.shape
    d_in = w.shape[1]
    assert t % tm == 0 and d_in % tn == 0 and QKV_WIDTH % tn == 0
    return pl.pallas_call(
        functools.partial(_in_proj_kernel, gate_tile0=QKV_WIDTH // tn, chunk=min(tm, 128)),
        grid=(t // tm, d_in // tn),
        in_specs=[
            pl.BlockSpec((tm, d), lambda i, j: (i, 0)),
            pl.BlockSpec((1, d), lambda i, j: (0, 0)),
            pl.BlockSpec((1, tn), lambda i, j: (0, j)),
            pl.BlockSpec((d, tn), lambda i, j: (0, j)),
        ],
        out_specs=pl.BlockSpec((tm, tn), lambda i, j: (i, j)),
        out_shape=jax.ShapeDtypeStruct((t, d_in), _BF16),
        scratch_shapes=[pltpu.VMEM((tm, d), _BF16)],
        compiler_params=_params(("parallel", "arbitrary")),
        name="in_proj",
    )(h2d, g, colscale, w)


def _dilated_kernel(q0, q1, q2, k0, k1, k2, v0, v1, v2, b0, b1, b2, o_ref,
                    kb0, kb1, kb2, vb0, vb1, vb2):
    i = pl.program_id(2)
    q_refs, bias_refs = (q0, q1, q2), (b0, b1, b2)
    k_bufs, v_bufs = (kb0, kb1, kb2), (vb0, vb1, vb2)

    for buf, new in zip(k_bufs + v_bufs, (k0, k1, k2, v0, v1, v2)):
        @pl.when(i == 0)
        def _(buf=buf):
            buf[0:SPAN, :] = jnp.zeros((SPAN, HEAD_DIM), buf.dtype)

        @pl.when(i > 0)
        def _(buf=buf):
            buf[0:SPAN, :] = buf[SPAN:2 * SPAN, :]

        buf[SPAN:2 * SPAN, :] = new[0]

    first_valid_row = jnp.where(i == 0, SPAN, 0)

    def body(s, carry):
        o = pl.multiple_of(s * QBLK, QBLK)
        scores, starts = [], []
        for g, (window, _) in enumerate(DIL_GROUPS):
            n = window + QBLK
            start = pl.multiple_of(o + SPAN - window, QBLK)
            q = q_refs[g][0, pl.ds(o, QBLK), :]
            kk = k_bufs[g][pl.ds(start, n), :]
            sc = lax.dot_general(q, kk, (((1,), (1,)), ((), ())), preferred_element_type=_F32)
            sc = sc + bias_refs[g][0]
            kj = lax.broadcasted_iota(jnp.int32, (QBLK, n), 1)
            sc = jnp.where(kj >= first_valid_row - start, sc, MASKED)
            scores.append(sc)
            starts.append(start)
        m = functools.reduce(jnp.maximum, [jnp.max(sc, axis=-1, keepdims=True) for sc in scores])
        denom = jnp.zeros((QBLK, 1), _F32)
        acc = jnp.zeros((QBLK, HEAD_DIM), _F32)
        for g, (window, _) in enumerate(DIL_GROUPS):
            p = jnp.exp(scores[g] - m)
            denom = denom + jnp.sum(p, axis=-1, keepdims=True)
            vv = v_bufs[g][pl.ds(starts[g], window + QBLK), :]
            acc = acc + jnp.dot(p.astype(_BF16), vv, preferred_element_type=_F32)
        o_ref[0, pl.ds(o, QBLK), :] = (acc / denom).astype(o_ref.dtype)
        return carry

    lax.fori_loop(0, SPAN // QBLK, body, 0)


def _dilated_bias_tables():
    slopes = jnp.exp2(-ALIBI_MAX * jnp.arange(1, N_HEADS_A + 1, dtype=_F32) / N_HEADS_A)
    tables = []
    for g, (window, dilation) in enumerate(DIL_GROUPS):
        qi = jnp.arange(QBLK)[:, None]
        kj = jnp.arange(window + QBLK)[None, :]
        dist = qi + window - kj
        valid = (dist >= 0) & (dist <= window) & (dist % dilation == 0)
        sl = slopes[g * HEADS_PER_GROUP:(g + 1) * HEADS_PER_GROUP]
        bias = -sl[:, None, None] * dist.astype(_F32)[None]
        tables.append(jnp.where(valid[None], bias, MASKED))
    return tables


def _dilated_attention(proj3):
    b, s, _ = proj3.shape
    assert s % SPAN == 0
    n_heads = N_HEADS_A

    def col_spec(base):
        return [pl.BlockSpec((1, SPAN, HEAD_DIM), (lambda bi, hh, i, c=base + g * HEADS_PER_GROUP: (bi, i, c + hh)))
                for g in range(len(DIL_GROUPS))]

    tables = _dilated_bias_tables()
    bias_specs = [pl.BlockSpec((1, QBLK, w + QBLK), lambda bi, hh, i: (hh, 0, 0)) for w, _ in DIL_GROUPS]
    return pl.pallas_call(
        _dilated_kernel,
        grid=(b, HEADS_PER_GROUP, s // SPAN),
        in_specs=col_spec(0) + col_spec(n_heads) + col_spec(2 * n_heads) + bias_specs,
        out_specs=pl.BlockSpec((1, SPAN, HEAD_DIM), lambda bi, hh, i: (bi, i, hh)),
        out_shape=jax.ShapeDtypeStruct((b, s, A_OUT), _BF16),
        scratch_shapes=[pltpu.VMEM((2 * SPAN, HEAD_DIM), _BF16) for _ in range(6)],
        compiler_params=_params(("arbitrary", "arbitrary", "arbitrary")),
        name="dilated_attention",
    )(*([proj3] * 9), *tables)


def _stick_kernel(q_ref, k_ref, v_ref, tri_ref, o_ref, *, tq):
    i = pl.program_id(2)
    q = q_ref[0]
    row = lax.broadcasted_iota(jnp.int32, (tq, tq), 0)
    col = lax.broadcasted_iota(jnp.int32, (tq, tq), 1)
    before = col < row

    def chunk(c, after, acc, diagonal):
        k0 = pl.multiple_of(c * tq, tq)
        kk = k_ref[0, pl.ds(k0, tq), :]
        z = lax.dot_general(q, kk, (((1,), (1,)), ((), ())), preferred_element_type=_F32)
        ln = -(jnp.maximum(z, 0.0) + jnp.log1p(jnp.exp(-jnp.abs(z))))
        if diagonal:
            ln = jnp.where(before, ln, 0.0)
        hi = ln.astype(_BF16)
        lo = (ln - hi.astype(_F32)).astype(_BF16)
        tri = tri_ref[...]
        csum = (jnp.dot(hi, tri, preferred_element_type=_F32)
                + jnp.dot(lo, tri, preferred_element_type=_F32))
        w = jnp.exp(z + csum + after)
        if diagonal:
            w = jnp.where(before, w, 0.0)
        acc = acc + jnp.dot(w.astype(_BF16), v_ref[0, pl.ds(k0, tq), :], preferred_element_type=_F32)
        return after + csum[:, 0:1], acc

    after, acc = chunk(i, jnp.zeros((tq, 1), _F32), jnp.zeros((tq, HEAD_DIM), _F32), True)

    def body(t, carry):
        return chunk(i - 1 - t, carry[0], carry[1], False)

    after, acc = lax.fori_loop(0, i, body, (after, acc))
    o_ref[0] = acc.astype(o_ref.dtype)


def _stick_attention(proj3, *, tq):
    b, s, _ = proj3.shape
    assert s % tq == 0
    base = 3 * N_HEADS_A
    idx = jnp.arange(tq)
    tri = (idx[:, None] >= idx[None, :]).astype(_BF16)
    return pl.pallas_call(
        functools.partial(_stick_kernel, tq=tq),
        grid=(b, N_HEADS_B, s // tq),
        in_specs=[
            pl.BlockSpec((1, tq, HEAD_DIM), lambda bi, hh, i: (bi, i, base + hh)),
            pl.BlockSpec((1, s, HEAD_DIM), lambda bi, hh, i: (bi, 0, base + N_HEADS_B + hh)),
            pl.BlockSpec((1, s, HEAD_DIM), lambda bi, hh, i: (bi, 0, base + 2 * N_HEADS_B + hh)),
            pl.BlockSpec((tq, tq), lambda bi, hh, i: (0, 0)),
        ],
        out_specs=pl.BlockSpec((1, tq, HEAD_DIM), lambda bi, hh, i: (bi, i, hh)),
        out_shape=jax.ShapeDtypeStruct((b, s, WIDTH_B), _BF16),
        compiler_params=_params(("parallel", "parallel", "arbitrary")),
        name="stick_breaking_attention",
    )(proj3, proj3, proj3, tri)


def _merge_kernel(ya_ref, yb_ref, ga_ref, gb_ref, h_ref, wa_ref, wb_ref, wo_ref, g_ref, o_ref):
    a = jnp.dot(ya_ref[...], wa_ref[...], preferred_element_type=_F32)
    bb = jnp.dot(yb_ref[...], wb_ref[...], preferred_element_type=_F32)
    m = (ga_ref[...].astype(_F32) * a + gb_ref[...].astype(_F32) * bb).astype(_BF16)
    mix = jnp.dot(m, wo_ref[...], preferred_element_type=_F32)
    o_ref[...] = h_ref[...] + _rmsnorm_rows(mix, g_ref[...])


def _merge(ya, yb, proj, h2d, wa, wb, wo, g, *, tm):
    t, d = h2d.shape
    assert t % tm == 0 and QKV_WIDTH % d == 0
    gate_blk = QKV_WIDTH // d
    full = lambda shape: pl.BlockSpec(shape, lambda i: (0, 0))
    return pl.pallas_call(
        _merge_kernel,
        grid=(t // tm,),
        in_specs=[
            pl.BlockSpec((tm, A_OUT), lambda i: (i, 0)),
            pl.BlockSpec((tm, WIDTH_B), lambda i: (i, 0)),
            pl.BlockSpec((tm, d), lambda i: (i, gate_blk)),
            pl.BlockSpec((tm, d), lambda i: (i, gate_blk + 1)),
            pl.BlockSpec((tm, d), lambda i: (i, 0)),
            full(wa.shape), full(wb.shape), full(wo.shape), full(g.shape),
        ],
        out_specs=pl.BlockSpec((tm, d), lambda i: (i, 0)),
        out_shape=jax.ShapeDtypeStruct((t, d), _F32),
        compiler_params=_params(("parallel",)),
        name="gated_merge",
    )(ya, yb, proj, proj, h2d, wa, wb, wo, g)


def _ffn_kernel(h_ref, hp_ref, gpre_ref, wg_ref, wv_ref, cwg_ref, cwv_ref, cbg_ref, cbv_ref, wd_ref,
                gpost_ref, o_ref, u_ref, *, tiles_per_seq, chunk):
    i = pl.program_id(0)
    j = pl.program_id(1)
    tm = h_ref.shape[0]
    n_chunks = tm // chunk

    @pl.when(j == 0)
    def _():
        halo = _rmsnorm_rows(hp_ref[...], gpre_ref[...])
        at_sequence_start = (i % tiles_per_seq) == 0
        u_ref[0:CONV_HALO, :] = jnp.where(at_sequence_start, 0.0, halo).astype(u_ref.dtype)

        def body(c, carry):
            r0 = pl.multiple_of(c * chunk, chunk)
            x = h_ref[pl.ds(r0, chunk), :]
            u_ref[pl.ds(CONV_HALO + r0, chunk), :] = _rmsnorm_rows(x, gpre_ref[...]).astype(u_ref.dtype)
            return carry
        lax.fori_loop(0, n_chunks, body, 0)

    u = u_ref[...]

    def conv_branch(w_ref, cw_ref, cb_ref):
        x = jnp.dot(u, w_ref[...], preferred_element_type=_F32)
        cw = cw_ref[...]
        y = cb_ref[...] + cw[0:1, :] * pltpu.roll(x, 2, 0)
        y = y + cw[1:2, :] * pltpu.roll(x, 1, 0)
        y = y + cw[2:3, :] * x
        return y[CONV_HALO:, :]

    gate = conv_branch(wg_ref, cwg_ref, cbg_ref)
    value = conv_branch(wv_ref, cwv_ref, cbv_ref)
    act = (jax.nn.gelu(gate, approximate=True) * value).astype(_BF16)
    contrib = jnp.dot(act, wd_ref[...], preferred_element_type=_F32)

    @pl.when(j == 0)
    def _():
        o_ref[...] = contrib

    @pl.when(j > 0)
    def _():
        o_ref[...] += contrib

    @pl.when(j == pl.num_programs(1) - 1)
    def _():
        def body(c, carry):
            r0 = pl.multiple_of(c * chunk, chunk)
            f = o_ref[pl.ds(r0, chunk), :]
            o_ref[pl.ds(r0, chunk), :] = h_ref[pl.ds(r0, chunk), :] + _rmsnorm_rows(f, gpost_ref[...])
            return carry
        lax.fori_loop(0, n_chunks, body, 0)


def _ffn(h2d, gpre, w_up, conv_w, conv_b, w_down, gpost, *, tm, tf, seq):
    t, d = h2d.shape
    d_ff = w_down.shape[0]
    assert t % tm == 0 and seq % tm == 0 and d_ff % tf == 0 and tm % CONV_HALO == 0
    nj = d_ff // tf
    halo_blocks = tm // CONV_HALO
    return pl.pallas_call(
        functools.partial(_ffn_kernel, tiles_per_seq=seq // tm, chunk=min(tm, 128)),
        grid=(t // tm, nj),
        in_specs=[
            pl.BlockSpec((tm, d), lambda i, j: (i, 0)),
            pl.BlockSpec((CONV_HALO, d), lambda i, j: (jnp.maximum(i * halo_blocks - 1, 0), 0)),
            pl.BlockSpec((1, d), lambda i, j: (0, 0)),
            pl.BlockSpec((d, tf), lambda i, j: (0, j)),
            pl.BlockSpec((d, tf), lambda i, j: (0, nj + j)),
            pl.BlockSpec((CONV_WIDTH, tf), lambda i, j: (0, j)),
            pl.BlockSpec((CONV_WIDTH, tf), lambda i, j: (0, nj + j)),
            pl.BlockSpec((1, tf), lambda i, j: (0, j)),
            pl.BlockSpec((1, tf), lambda i, j: (0, nj + j)),
            pl.BlockSpec((tf, d), lambda i, j: (j, 0)),
            pl.BlockSpec((1, d), lambda i, j: (0, 0)),
        ],
        out_specs=pl.BlockSpec((tm, d), lambda i, j: (i, 0)),
        out_shape=jax.ShapeDtypeStruct((t, d), _F32),
        scratch_shapes=[pltpu.VMEM((CONV_HALO + tm, d), _BF16)],
        compiler_params=_params(("parallel", "arbitrary")),
        name="conv_ffn",
    )(h2d, h2d, gpre, w_up, w_up, conv_w, conv_w, conv_b, conv_b, w_down, gpost)


def _ple_kernel(h_ref, p_ref, wg_ref, wi_ref, o_ref):
    h = h_ref[...]
    gate = jax.nn.sigmoid(jnp.dot(h.astype(_BF16), wg_ref[...], preferred_element_type=_F32))
    e = jnp.dot(p_ref[...].astype(_BF16), wi_ref[...], preferred_element_type=_F32)
    o_ref[...] = h + gate * e


def _ple(h2d, p2d, w_gate, w_in, *, tm):
    t, d = h2d.shape
    pd = p2d.shape[1]
    assert t % tm == 0
    return pl.pallas_call(
        _ple_kernel,
        grid=(t // tm,),
        in_specs=[
            pl.BlockSpec((tm, d), lambda i: (i, 0)),
            pl.BlockSpec((tm, pd), lambda i: (i, 0)),
            pl.BlockSpec((d, d), lambda i: (0, 0)),
            pl.BlockSpec((pd, d), lambda i: (0, 0)),
        ],
        out_specs=pl.BlockSpec((tm, d), lambda i: (i, 0)),
        out_shape=jax.ShapeDtypeStruct((t, d), _F32),
        compiler_params=_params(("parallel",)),
        name="layer_embedding",
    )(h2d, p2d, w_gate, w_in)


def _tiles(t, seq, d, d_in, d_ff):
    def largest(n, cands):
        return next(c for c in cands if n % c == 0)
    return dict(
        proj_tm=largest(t, (1024, 512, 256, 128)),
        proj_tn=largest(math.gcd(d_in, QKV_WIDTH), (1024, 512, 256, 128)),
        stick_tq=largest(seq, (256, 128)),
        merge_tm=largest(t, (256, 128)),
        ffn_tm=largest(seq, (512, 256, 128)),
        ffn_tf=largest(d_ff, (512, 256, 128)),
        ple_tm=largest(t, (512, 256, 128)),
    )


def kernel(x, p, g_mix_pre, w_in, w_branch_a, w_branch_b, w_out, g_mix_post, g_ffn_pre, w_up, conv_w,
           conv_b, w_down, g_ffn_post, w_ple_in, w_ple_gate):
    b, s, d = x.shape
    depth = w_in.shape[0]
    d_in = w_in.shape[2]
    d_ff = w_down.shape[1]
    t = b * s
    assert d_in == QKV_WIDTH + 2 * d
    tiles = _tiles(t, s, d, d_in, d_ff)

    q_scale = 1.0 / math.sqrt(HEAD_DIM)
    col = jnp.arange(d_in)
    is_q = (col < WIDTH_A) | ((col >= 3 * WIDTH_A) & (col < 3 * WIDTH_A + WIDTH_B))
    colscale = jnp.where(is_q, q_scale, 1.0).astype(_F32)[None, :]

    h = x.reshape(t, d)
    for i in range(depth):
        row = lambda v: v[i][None, :]
        proj = _in_proj(h, row(g_mix_pre), colscale, w_in[i].astype(_BF16),
                        tm=tiles["proj_tm"], tn=tiles["proj_tn"])
        proj3 = proj.reshape(b, s, d_in)
        ya = _dilated_attention(proj3).reshape(t, A_OUT)
        yb = _stick_attention(proj3, tq=tiles["stick_tq"]).reshape(t, WIDTH_B)
        h = _merge(ya, yb, proj, h, w_branch_a[i].astype(_BF16), w_branch_b[i].astype(_BF16),
                   w_out[i].astype(_BF16), row(g_mix_post), tm=tiles["merge_tm"])
        h = _ffn(h, row(g_ffn_pre), w_up[i].astype(_BF16), conv_w[i], conv_b[i][None, :],
                 w_down[i].astype(_BF16), row(g_ffn_post), tm=tiles["ffn_tm"], tf=tiles["ffn_tf"], seq=s)
        h = _ple(h, p[i].reshape(t, -1), w_ple_gate[i].astype(_BF16), w_ple_in[i].astype(_BF16),
                 tm=tiles["ple_tm"])
    return h.reshape(b, s, d)
```

```python
import functools
import math

import jax
import jax.numpy as jnp
from jax import lax
from jax.experimental import pallas as pl
from jax.experimental.pallas import tpu as pltpu

HEAD_DIM = 128
DIL_GROUPS = ((128, 1), (512, 4), (2048, 16))
HEADS_PER_GROUP = 4
N_HEADS_A = HEADS_PER_GROUP * len(DIL_GROUPS)
N_HEADS_B = 4
WIDTH_A = N_HEADS_A * HEAD_DIM
WIDTH_B = N_HEADS_B * HEAD_DIM
A_OUT = HEADS_PER_GROUP * HEAD_DIM
QKV_WIDTH = 3 * WIDTH_A + 3 * WIDTH_B
CONV_WIDTH = 3
RMS_EPS = 1e-6
ALIBI_MAX = 8.0
MASKED = -1e30
EXP_UNDERFLOW = -110.0
QBLK = 128
SPAN = 2048
CONV_HALO = 16
VMEM_LIMIT = 56 * 1024 * 1024

_F32 = jnp.float32
_BF16 = jnp.bfloat16


def _rmsnorm_rows(x, g):
    ms = jnp.mean(x * x, axis=-1, keepdims=True)
    return x * lax.rsqrt(ms + RMS_EPS) * g


def _params(sem):
    return pltpu.CompilerParams(dimension_semantics=sem, vmem_limit_bytes=VMEM_LIMIT)


def _in_proj_kernel(x_ref, g_ref, cs_ref, w_ref, o_ref, u_ref, *, gate_tile0, chunk):
    j = pl.program_id(1)
    tm = x_ref.shape[0]

    @pl.when(j == 0)
    def _():
        def body(c, carry):
            r0 = pl.multiple_of(c * chunk, chunk)
            x = x_ref[pl.ds(r0, chunk), :]
            u_ref[pl.ds(r0, chunk), :] = _rmsnorm_rows(x, g_ref[...]).astype(u_ref.dtype)
            return carry
        lax.fori_loop(0, tm // chunk, body, 0)

    acc = jnp.dot(u_ref[...], w_ref[...], preferred_element_type=_F32)

    @pl.when(j < gate_tile0)
    def _():
        o_ref[...] = (acc * cs_ref[...]).astype(o_ref.dtype)

    @pl.when(j >= gate_tile0)
    def _():
        o_ref[...] = jax.nn.sigmoid(acc).astype(o_ref.dtype)


def _in_proj(h2d, g, colscale, w, *, tm, tn):
    t, d = h2d.shape
    d_in = w.shape[1]
    assert t % tm == 0 and d_in % tn == 0 and QKV_WIDTH % tn == 0
    return pl.pallas_call(
        functools.partial(_in_proj_kernel, gate_tile0=QKV_WIDTH // tn, chunk=min(tm, 128)),
        grid=(t // tm, d_in // tn),
        in_specs=[
            pl.BlockSpec((tm, d), lambda i, j: (i, 0)),
            pl.BlockSpec((1, d), lambda i, j: (0, 0)),
            pl.BlockSpec((1, tn), lambda i, j: (0, j)),
            pl.BlockSpec((d, tn), lambda i, j: (0, j)),
        ],
        out_specs=pl.BlockSpec((tm, tn), lambda i, j: (i, j)),
        out_shape=jax.ShapeDtypeStruct((t, d_in), _BF16),
        scratch_shapes=[pltpu.VMEM((tm, d), _BF16)],
        compiler_params=_params(("parallel", "arbitrary")),
        name="in_proj",
    )(h2d, g, colscale, w)


def _dilated_kernel(q0, q1, q2, k0, k1, k2, v0, v1, v2, b0, b1, b2, o_ref,
                    kb0, kb1, kb2, vb0, vb1, vb2):
    i = pl.program_id(2)
    q_refs, bias_refs = (q0, q1, q2), (b0, b1, b2)
    k_bufs, v_bufs = (kb0, kb1, kb2), (vb0, vb1, vb2)

    for buf, new in zip(k_bufs + v_bufs, (k0, k1, k2, v0, v1, v2)):
        @pl.when(i == 0)
        def _(buf=buf):
            buf[0:SPAN, :] = jnp.zeros((SPAN, HEAD_DIM), buf.dtype)

        @pl.when(i > 0)
        def _(buf=buf):
            buf[0:SPAN, :] = buf[SPAN:2 * SPAN, :]

        buf[SPAN:2 * SPAN, :] = new[0]

    first_valid_row = jnp.where(i == 0, SPAN, 0)

    def body(s, carry):
        o = pl.multiple_of(s * QBLK, QBLK)
        scores, starts = [], []
        for g, (window, _) in enumerate(DIL_GROUPS):
            n = window + QBLK
            start = pl.multiple_of(o + SPAN - window, QBLK)
            q = q_refs[g][0, pl.ds(o, QBLK), :]
            kk = k_bufs[g][pl.ds(start, n), :]
            sc = lax.dot_general(q, kk, (((1,), (1,)), ((), ())), preferred_element_type=_F32)
            sc = sc + bias_refs[g][0]
            kj = lax.broadcasted_iota(jnp.int32, (QBLK, n), 1)
            sc = jnp.where(kj >= first_valid_row - start, sc, MASKED)
            scores.append(sc)
            starts.append(start)
        m = functools.reduce(jnp.maximum, [jnp.max(sc, axis=-1, keepdims=True) for sc in scores])
        denom = jnp.zeros((QBLK, 1), _F32)
        acc = jnp.zeros((QBLK, HEAD_DIM), _F32)
        for g, (window, _) in enumerate(DIL_GROUPS):
            p = jnp.exp(scores[g] - m)
            denom = denom + jnp.sum(p, axis=-1, keepdims=True)
            vv = v_bufs[g][pl.ds(starts[g], window + QBLK), :]
            acc = acc + jnp.dot(p.astype(_BF16), vv, preferred_element_type=_F32)
        o_ref[0, pl.ds(o, QBLK), :] = (acc / denom).astype(o_ref.dtype)
        return carry

    lax.fori_loop(0, SPAN // QBLK, body, 0)


def _dilated_bias_tables():
    slopes = jnp.exp2(-ALIBI_MAX * jnp.arange(1, N_HEADS_A + 1, dtype=_F32) / N_HEADS_A)
    tables = []
    for g, (window, dilation) in enumerate(DIL_GROUPS):
        qi = jnp.arange(QBLK)[:, None]
        kj = jnp.arange(window + QBLK)[None, :]
        dist = qi + window - kj
        valid = (dist >= 0) & (dist <= window) & (dist % dilation == 0)
        sl = slopes[g * HEADS_PER_GROUP:(g + 1) * HEADS_PER_GROUP]
        bias = -sl[:, None, None] * dist.astype(_F32)[None]
        tables.append(jnp.where(valid[None], bias, MASKED))
    return tables


def _dilated_attention(proj3):
    b, s, _ = proj3.shape
    assert s % SPAN == 0
    n_heads = N_HEADS_A

    def col_spec(base):
        return [pl.BlockSpec((1, SPAN, HEAD_DIM), (lambda bi, hh, i, c=base + g * HEADS_PER_GROUP: (bi, i, c + hh)))
                for g in range(len(DIL_GROUPS))]

    tables = _dilated_bias_tables()
    bias_specs = [pl.BlockSpec((1, QBLK, w + QBLK), lambda bi, hh, i: (hh, 0, 0)) for w, _ in DIL_GROUPS]
    return pl.pallas_call(
        _dilated_kernel,
        grid=(b, HEADS_PER_GROUP, s // SPAN),
        in_specs=col_spec(0) + col_spec(n_heads) + col_spec(2 * n_heads) + bias_specs,
        out_specs=pl.BlockSpec((1, SPAN, HEAD_DIM), lambda bi, hh, i: (bi, i, hh)),
        out_shape=jax.ShapeDtypeStruct((b, s, A_OUT), _BF16),
        scratch_shapes=[pltpu.VMEM((2 * SPAN, HEAD_DIM), _BF16) for _ in range(6)],
        compiler_params=_params(("arbitrary", "arbitrary", "arbitrary")),
        name="dilated_attention",
    )(*([proj3] * 9), *tables)


def _stick_kernel(q_ref, k_ref, v_ref, tri_ref, o_ref, *, tq):
    i = pl.program_id(2)
    q = q_ref[0]
    row = lax.broadcasted_iota(jnp.int32, (tq, tq), 0)
    col = lax.broadcasted_iota(jnp.int32, (tq, tq), 1)
    before = col < row

    def chunk(c, after, acc, diagonal):
        k0 = pl.multiple_of(c * tq, tq)
        kk = k_ref[0, pl.ds(k0, tq), :]
        z = lax.dot_general(q, kk, (((1,), (1,)), ((), ())), preferred_element_type=_F32)
        ln = -(jnp.maximum(z, 0.0) + jnp.log1p(jnp.exp(-jnp.abs(z))))
        if diagonal:
            ln = jnp.where(before, ln, 0.0)
        hi = ln.astype(_BF16)
        lo = (ln - hi.astype(_F32)).astype(_BF16)
        tri = tri_ref[...]
        csum = (jnp.dot(hi, tri, preferred_element_type=_F32)
                + jnp.dot(lo, tri, preferred_element_type=_F32))
        w = jnp.exp(z + csum + after)
        if diagonal:
            w = jnp.where(before, w, 0.0)
        acc = acc + jnp.dot(w.astype(_BF16), v_ref[0, pl.ds(k0, tq), :], preferred_element_type=_F32)
        return after + csum[:, 0:1], acc

    after, acc = chunk(i, jnp.zeros((tq, 1), _F32), jnp.zeros((tq, HEAD_DIM), _F32), True)

    def keep_going(carry):
        c, after, _ = carry
        return jnp.logical_and(c >= 0, jnp.max(after) > EXP_UNDERFLOW)

    def body(carry):
        c, after, acc = carry
        after, acc = chunk(c, after, acc, False)
        return c - 1, after, acc

    _, _, acc = lax.while_loop(keep_going, body, (i - 1, after, acc))
    o_ref[0] = acc.astype(o_ref.dtype)


def _stick_attention(proj3, *, tq):
    b, s, _ = proj3.shape
    assert s % tq == 0
    base = 3 * N_HEADS_A
    idx = jnp.arange(tq)
    tri = (idx[:, None] >= idx[None, :]).astype(_BF16)
    return pl.pallas_call(
        functools.partial(_stick_kernel, tq=tq),
        grid=(b, N_HEADS_B, s // tq),
        in_specs=[
            pl.BlockSpec((1, tq, HEAD_DIM), lambda bi, hh, i: (bi, i, base + hh)),
            pl.BlockSpec((1, s, HEAD_DIM), lambda bi, hh, i: (bi, 0, base + N_HEADS_B + hh)),
            pl.BlockSpec((1, s, HEAD_DIM), lambda bi, hh, i: (bi, 0, base + 2 * N_HEADS_B + hh)),
            pl.BlockSpec((tq, tq), lambda bi, hh, i: (0, 0)),
        ],
        out_specs=pl.BlockSpec((1, tq, HEAD_DIM), lambda bi, hh, i: (bi, i, hh)),
        out_shape=jax.ShapeDtypeStruct((b, s, WIDTH_B), _BF16),
        compiler_params=_params(("parallel", "parallel", "arbitrary")),
        name="stick_breaking_attention",
    )(proj3, proj3, proj3, tri)


def _merge_kernel(ya_ref, yb_ref, ga_ref, gb_ref, h_ref, wa_ref, wb_ref, wo_ref, g_ref, o_ref):
    a = jnp.dot(ya_ref[...], wa_ref[...], preferred_element_type=_F32)
    bb = jnp.dot(yb_ref[...], wb_ref[...], preferred_element_type=_F32)
    m = (ga_ref[...].astype(_F32) * a + gb_ref[...].astype(_F32) * bb).astype(_BF16)
    mix = jnp.dot(m, wo_ref[...], preferred_element_type=_F32)
    o_ref[...] = h_ref[...] + _rmsnorm_rows(mix, g_ref[...])


def _merge(ya, yb, proj, h2d, wa, wb, wo, g, *, tm):
    t, d = h2d.shape
    assert t % tm == 0 and QKV_WIDTH % d == 0
    gate_blk = QKV_WIDTH // d
    full = lambda shape: pl.BlockSpec(shape, lambda i: (0, 0))
    return pl.pallas_call(
        _merge_kernel,
        grid=(t // tm,),
        in_specs=[
            pl.BlockSpec((tm, A_OUT), lambda i: (i, 0)),
            pl.BlockSpec((tm, WIDTH_B), lambda i: (i, 0)),
            pl.BlockSpec((tm, d), lambda i: (i, gate_blk)),
            pl.BlockSpec((tm, d), lambda i: (i, gate_blk + 1)),
            pl.BlockSpec((tm, d), lambda i: (i, 0)),
            full(wa.shape), full(wb.shape), full(wo.shape), full(g.shape),
        ],
        out_specs=pl.BlockSpec((tm, d), lambda i: (i, 0)),
        out_shape=jax.ShapeDtypeStruct((t, d), _F32),
        compiler_params=_params(("parallel",)),
        name="gated_merge",
    )(ya, yb, proj, proj, h2d, wa, wb, wo, g)


def _ffn_kernel(h_ref, hp_ref, gpre_ref, wg_ref, wv_ref, cwg_ref, cwv_ref, cbg_ref, cbv_ref, wd_ref,
                gpost_ref, o_ref, u_ref, *, tiles_per_seq, chunk):
    i = pl.program_id(0)
    j = pl.program_id(1)
    tm = h_ref.shape[0]
    n_chunks = tm // chunk

    @pl.when(j == 0)
    def _():
        halo = _rmsnorm_rows(hp_ref[...], gpre_ref[...])
        at_sequence_start = (i % tiles_per_seq) == 0
        u_ref[0:CONV_HALO, :] = jnp.where(at_sequence_start, 0.0, halo).astype(u_ref.dtype)

        def body(c, carry):
            r0 = pl.multiple_of(c * chunk, chunk)
            x = h_ref[pl.ds(r0, chunk), :]
            u_ref[pl.ds(CONV_HALO + r0, chunk), :] = _rmsnorm_rows(x, gpre_ref[...]).astype(u_ref.dtype)
            return carry
        lax.fori_loop(0, n_chunks, body, 0)

    u = u_ref[...]

    def conv_branch(w_ref, cw_ref, cb_ref):
        x = jnp.dot(u, w_ref[...], preferred_element_type=_F32)
        cw = cw_ref[...]
        y = cb_ref[...] + cw[0:1, :] * pltpu.roll(x, 2, 0)
        y = y + cw[1:2, :] * pltpu.roll(x, 1, 0)
        y = y + cw[2:3, :] * x
        return y[CONV_HALO:, :]

    gate = conv_branch(wg_ref, cwg_ref, cbg_ref)
    value = conv_branch(wv_ref, cwv_ref, cbv_ref)
    act = (jax.nn.gelu(gate, approximate=True) * value).astype(_BF16)
    contrib = jnp.dot(act, wd_ref[...], preferred_element_type=_F32)

    @pl.when(j == 0)
    def _():
        o_ref[...] = contrib

    @pl.when(j > 0)
    def _():
        o_ref[...] += contrib

    @pl.when(j == pl.num_programs(1) - 1)
    def _():
        def body(c, carry):
            r0 = pl.multiple_of(c * chunk, chunk)
            f = o_ref[pl.ds(r0, chunk), :]
            o_ref[pl.ds(r0, chunk), :] = h_ref[pl.ds(r0, chunk), :] + _rmsnorm_rows(f, gpost_ref[...])
            return carry
        lax.fori_loop(0, n_chunks, body, 0)


def _ffn(h2d, gpre, w_up, conv_w, conv_b, w_down, gpost, *, tm, tf, seq):
    t, d = h2d.shape
    d_ff = w_down.shape[0]
    assert t % tm == 0 and seq % tm == 0 and d_ff % tf == 0 and tm % CONV_HALO == 0
    nj = d_ff // tf
    halo_blocks = tm // CONV_HALO
    return pl.pallas_call(
        functools.partial(_ffn_kernel, tiles_per_seq=seq // tm, chunk=min(tm, 128)),
        grid=(t // tm, nj),
        in_specs=[
            pl.BlockSpec((tm, d), lambda i, j: (i, 0)),
            pl.BlockSpec((CONV_HALO, d), lambda i, j: (jnp.maximum(i * halo_blocks - 1, 0), 0)),
            pl.BlockSpec((1, d), lambda i, j: (0, 0)),
            pl.BlockSpec((d, tf), lambda i, j: (0, j)),
            pl.BlockSpec((d, tf), lambda i, j: (0, nj + j)),
            pl.BlockSpec((CONV_WIDTH, tf), lambda i, j: (0, j)),
            pl.BlockSpec((CONV_WIDTH, tf), lambda i, j: (0, nj + j)),
            pl.BlockSpec((1, tf), lambda i, j: (0, j)),
            pl.BlockSpec((1, tf), lambda i, j: (0, nj + j)),
            pl.BlockSpec((tf, d), lambda i, j: (j, 0)),
            pl.BlockSpec((1, d), lambda i, j: (0, 0)),
        ],
        out_specs=pl.BlockSpec((tm, d), lambda i, j: (i, 0)),
        out_shape=jax.ShapeDtypeStruct((t, d), _F32),
        scratch_shapes=[pltpu.VMEM((CONV_HALO + tm, d), _BF16)],
        compiler_params=_params(("parallel", "arbitrary")),
        name="conv_ffn",
    )(h2d, h2d, gpre, w_up, w_up, conv_w, conv_w, conv_b, conv_b, w_down, gpost)


def _ple_kernel(h_ref, p_ref, wg_ref, wi_ref, o_ref):
    h = h_ref[...]
    gate = jax.nn.sigmoid(jnp.dot(h.astype(_BF16), wg_ref[...], preferred_element_type=_F32))
    e = jnp.dot(p_ref[...].astype(_BF16), wi_ref[...], preferred_element_type=_F32)
    o_ref[...] = h + gate * e


def _ple(h2d, p2d, w_gate, w_in, *, tm):
    t, d = h2d.shape
    pd = p2d.shape[1]
    assert t % tm == 0
    return pl.pallas_call(
        _ple_kernel,
        grid=(t // tm,),
        in_specs=[
            pl.BlockSpec((tm, d), lambda i: (i, 0)),
            pl.BlockSpec((tm, pd), lambda i: (i, 0)),
            pl.BlockSpec((d, d), lambda i: (0, 0)),
            pl.BlockSpec((pd, d), lambda i: (0, 0)),
        ],
        out_specs=pl.BlockSpec((tm, d), lambda i: (i, 0)),
        out_shape=jax.ShapeDtypeStruct((t, d), _F32),
        compiler_params=_params(("parallel",)),
        name="layer_embedding",
    )(h2d, p2d, w_gate, w_in)


def _tiles(t, seq, d, d_in, d_ff):
    def largest(n, cands):
        return next(c for c in cands if n % c == 0)
    return dict(
        proj_tm=largest(t, (1024, 512, 256, 128)),
        proj_tn=largest(math.gcd(d_in, QKV_WIDTH), (1024, 512, 256, 128)),
        stick_tq=largest(seq, (256, 128)),
        merge_tm=largest(t, (256, 128)),
        ffn_tm=largest(seq, (512, 256, 128)),
        ffn_tf=largest(d_ff, (512, 256, 128)),
        ple_tm=largest(t, (512, 256, 128)),
    )


def kernel(x, p, g_mix_pre, w_in, w_branch_a, w_branch_b, w_out, g_mix_post, g_ffn_pre, w_up, conv_w,
           conv_b, w_down, g_ffn_post, w_ple_in, w_ple_gate):
    b, s, d = x.shape
    depth = w_in.shape[0]
    d_in = w_in.shape[2]
    d_ff = w_down.shape[1]
    t = b * s
    assert d_in == QKV_WIDTH + 2 * d
    tiles = _tiles(t, s, d, d_in, d_ff)

    q_scale = 1.0 / math.sqrt(HEAD_DIM)
    col = jnp.arange(d_in)
    is_q = (col < WIDTH_A) | ((col >= 3 * WIDTH_A) & (col < 3 * WIDTH_A + WIDTH_B))
    colscale = jnp.where(is_q, q_scale, 1.0).astype(_F32)[None, :]

    h = x.reshape(t, d)
    for i in range(depth):
        row = lambda v: v[i][None, :]
        proj = _in_proj(h, row(g_mix_pre), colscale, w_in[i].astype(_BF16),
                        tm=tiles["proj_tm"], tn=tiles["proj_tn"])
        proj3 = proj.reshape(b, s, d_in)
        ya = _dilated_attention(proj3).reshape(t, A_OUT)
        yb = _stick_attention(proj3, tq=tiles["stick_tq"]).reshape(t, WIDTH_B)
        h = _merge(ya, yb, proj, h, w_branch_a[i].astype(_BF16), w_branch_b[i].astype(_BF16),
                   w_out[i].astype(_BF16), row(g_mix_post), tm=tiles["merge_tm"])
        h = _ffn(h, row(g_ffn_pre), w_up[i].astype(_BF16), conv_w[i], conv_b[i][None, :],
                 w_down[i].astype(_BF16), row(g_ffn_post), tm=tiles["ffn_tm"], tf=tiles["ffn_tf"], seq=s)
        h = _ple(h, p[i].reshape(t, -1), w_ple_gate[i].astype(_BF16), w_ple_in[i].astype(_BF16),
                 tm=tiles["ple_tm"])
    return h.reshape(b, s, d)
```

```python
import functools
import math

import jax
import jax.numpy as jnp
from jax import lax
from jax.experimental import pallas as pl
from jax.experimental.pallas import tpu as pltpu

HEAD_DIM = 128
DIL_GROUPS = ((128, 1), (512, 4), (2048, 16))
HEADS_PER_GROUP = 4
N_HEADS_A = HEADS_PER_GROUP * len(DIL_GROUPS)
N_HEADS_B = 4
WIDTH_A = N_HEADS_A * HEAD_DIM
WIDTH_B = N_HEADS_B * HEAD_DIM
A_OUT = HEADS_PER_GROUP * HEAD_DIM
QKV_WIDTH = 3 * WIDTH_A + 3 * WIDTH_B
CONV_WIDTH = 3
RMS_EPS = 1e-6
ALIBI_MAX = 8.0
MASKED = -1e30
EXP_UNDERFLOW = -110.0
QBLK = 128
SPAN = 2048
CONV_HALO = 16
VMEM_LIMIT = 56 * 1024 * 1024

_F32 = jnp.float32
_BF16 = jnp.bfloat16


def _rmsnorm_rows(x, g):
    ms = jnp.mean(x * x, axis=-1, keepdims=True)
    return x * lax.rsqrt(ms + RMS_EPS) * g


def _params(sem):
    return pltpu.CompilerParams(dimension_semantics=sem, vmem_limit_bytes=VMEM_LIMIT)


def _in_proj_kernel(x_ref, g_ref, cs_ref, w_ref, o_ref, u_ref, *, gate_tile0, chunk):
    j = pl.program_id(1)
    tm = x_ref.shape[0]

    @pl.when(j == 0)
    def _():
        def body(c, carry):
            r0 = pl.multiple_of(c * chunk, chunk)
            x = x_ref[pl.ds(r0, chunk), :]
            u_ref[pl.ds(r0, chunk), :] = _rmsnorm_rows(x, g_ref[...]).astype(u_ref.dtype)
            return carry
        lax.fori_loop(0, tm // chunk, body, 0)

    acc = jnp.dot(u_ref[...], w_ref[...], preferred_element_type=_F32)

    o_ref[...] = jnp.where(j >= gate_tile0, jax.nn.sigmoid(acc), acc * cs_ref[...]).astype(o_ref.dtype)


def _in_proj(h2d, g, colscale, w, *, tm, tn):
    t, d = h2d.shape
    d_in = w.shape[1]
    assert t % tm == 0 and d_in % tn == 0 and QKV_WIDTH % tn == 0
    return pl.pallas_call(
        functools.partial(_in_proj_kernel, gate_tile0=QKV_WIDTH // tn, chunk=min(tm, 128)),
        grid=(t // tm, d_in // tn),
        in_specs=[
            pl.BlockSpec((tm, d), lambda i, j: (i, 0)),
            pl.BlockSpec((1, d), lambda i, j: (0, 0)),
            pl.BlockSpec((1, tn), lambda i, j: (0, j)),
            pl.BlockSpec((d, tn), lambda i, j: (0, j)),
        ],
        out_specs=pl.BlockSpec((tm, tn), lambda i, j: (i, j)),
        out_shape=jax.ShapeDtypeStruct((t, d_in), _BF16),
        scratch_shapes=[pltpu.VMEM((tm, d), _BF16)],
        compiler_params=_params(("parallel", "arbitrary")),
        name="in_proj",
    )(h2d, g, colscale, w)


def _dilated_kernel(q0, q1, q2, k0, k1, k2, v0, v1, v2, b0, b1, b2, o_ref,
                    kb0, kb1, kb2, vb0, vb1, vb2):
    i = pl.program_id(2)
    q_refs, bias_refs = (q0, q1, q2), (b0, b1, b2)
    k_bufs, v_bufs = (kb0, kb1, kb2), (vb0, vb1, vb2)

    for buf, new in zip(k_bufs + v_bufs, (k0, k1, k2, v0, v1, v2)):
        @pl.when(i == 0)
        def _(buf=buf):
            buf[0:SPAN, :] = jnp.zeros((SPAN, HEAD_DIM), buf.dtype)

        @pl.when(i > 0)
        def _(buf=buf):
            buf[0:SPAN, :] = buf[SPAN:2 * SPAN, :]

        buf[SPAN:2 * SPAN, :] = new[0]

    first_valid_row = jnp.where(i == 0, SPAN, 0)

    def body(s, carry):
        o = pl.multiple_of(s * QBLK, QBLK)
        scores, starts = [], []
        for g, (window, _) in enumerate(DIL_GROUPS):
            n = window + QBLK
            start = pl.multiple_of(o + SPAN - window, QBLK)
            q = q_refs[g][0, pl.ds(o, QBLK), :]
            kk = k_bufs[g][pl.ds(start, n), :]
            sc = lax.dot_general(q, kk, (((1,), (1,)), ((), ())), preferred_element_type=_F32)
            sc = sc + bias_refs[g][0]
            kj = lax.broadcasted_iota(jnp.int32, (QBLK, n), 1)
            sc = jnp.where(kj >= first_valid_row - start, sc, MASKED)
            scores.append(sc)
            starts.append(start)
        m = functools.reduce(jnp.maximum, [jnp.max(sc, axis=-1, keepdims=True) for sc in scores])
        denom = jnp.zeros((QBLK, 1), _F32)
        acc = jnp.zeros((QBLK, HEAD_DIM), _F32)
        for g, (window, _) in enumerate(DIL_GROUPS):
            p = jnp.exp(scores[g] - m)
            denom = denom + jnp.sum(p, axis=-1, keepdims=True)
            vv = v_bufs[g][pl.ds(starts[g], window + QBLK), :]
            acc = acc + jnp.dot(p.astype(_BF16), vv, preferred_element_type=_F32)
        o_ref[0, pl.ds(o, QBLK), :] = (acc / denom).astype(o_ref.dtype)
        return carry

    lax.fori_loop(0, SPAN // QBLK, body, 0)


def _dilated_bias_tables():
    slopes = jnp.exp2(-ALIBI_MAX * jnp.arange(1, N_HEADS_A + 1, dtype=_F32) / N_HEADS_A)
    tables = []
    for g, (window, dilation) in enumerate(DIL_GROUPS):
        qi = jnp.arange(QBLK)[:, None]
        kj = jnp.arange(window + QBLK)[None, :]
        dist = qi + window - kj
        valid = (dist >= 0) & (dist <= window) & (dist % dilation == 0)
        sl = slopes[g * HEADS_PER_GROUP:(g + 1) * HEADS_PER_GROUP]
        bias = -sl[:, None, None] * dist.astype(_F32)[None]
        tables.append(jnp.where(valid[None], bias, MASKED))
    return tables


def _dilated_attention(proj3):
    b, s, _ = proj3.shape
    assert s % SPAN == 0
    n_heads = N_HEADS_A

    def col_spec(base):
        return [pl.BlockSpec((1, SPAN, HEAD_DIM), (lambda bi, hh, i, c=base + g * HEADS_PER_GROUP: (bi, i, c + hh)))
                for g in range(len(DIL_GROUPS))]

    tables = _dilated_bias_tables()
    bias_specs = [pl.BlockSpec((1, QBLK, w + QBLK), lambda bi, hh, i: (hh, 0, 0)) for w, _ in DIL_GROUPS]
    return pl.pallas_call(
        _dilated_kernel,
        grid=(b, HEADS_PER_GROUP, s // SPAN),
        in_specs=col_spec(0) + col_spec(n_heads) + col_spec(2 * n_heads) + bias_specs,
        out_specs=pl.BlockSpec((1, SPAN, HEAD_DIM), lambda bi, hh, i: (bi, i, hh)),
        out_shape=jax.ShapeDtypeStruct((b, s, A_OUT), _BF16),
        scratch_shapes=[pltpu.VMEM((2 * SPAN, HEAD_DIM), _BF16) for _ in range(6)],
        compiler_params=_params(("arbitrary", "arbitrary", "arbitrary")),
        name="dilated_attention",
    )(*([proj3] * 9), *tables)


def _stick_kernel(q_ref, k_ref, v_ref, tri_ref, o_ref, *, tq, heads):
    i = pl.program_id(2)
    row = lax.broadcasted_iota(jnp.int32, (tq, tq), 0)
    col = lax.broadcasted_iota(jnp.int32, (tq, tq), 1)
    before = col < row

    def chunk(c, afters, accs, diagonal):
        k0 = pl.multiple_of(c * tq, tq)
        tri = tri_ref[...]
        lanes = [slice(h * HEAD_DIM, (h + 1) * HEAD_DIM) for h in range(heads)]
        zs = [lax.dot_general(q_ref[0, :, lanes[h]], k_ref[0, pl.ds(k0, tq), lanes[h]],
                              (((1,), (1,)), ((), ())), preferred_element_type=_F32) for h in range(heads)]
        csums = []
        for h in range(heads):
            z = zs[h]
            ln = -(jnp.maximum(z, 0.0) + jnp.log(1.0 + jnp.exp(-jnp.abs(z))))
            if diagonal:
                ln = jnp.where(before, ln, 0.0)
            hi = ln.astype(_BF16)
            lo = (ln - hi.astype(_F32)).astype(_BF16)
            csums.append(jnp.dot(hi, tri, preferred_element_type=_F32)
                         + jnp.dot(lo, tri, preferred_element_type=_F32))
        new_afters, new_accs = [], []
        for h in range(heads):
            w = jnp.exp(zs[h] + csums[h] + afters[h])
            if diagonal:
                w = jnp.where(before, w, 0.0)
            new_accs.append(accs[h] + jnp.dot(w.astype(_BF16), v_ref[0, pl.ds(k0, tq), lanes[h]],
                                              preferred_element_type=_F32))
            new_afters.append(afters[h] + csums[h][:, 0:1])
        return tuple(new_afters), tuple(new_accs)

    afters, accs = chunk(i, (jnp.zeros((tq, 1), _F32),) * heads, (jnp.zeros((tq, HEAD_DIM), _F32),) * heads, True)

    def keep_going(carry):
        c, afters, _ = carry
        return jnp.logical_and(c >= 0, jnp.max(functools.reduce(jnp.maximum, afters)) > EXP_UNDERFLOW)

    def body(carry):
        c, afters, accs = carry
        afters, accs = chunk(c, afters, accs, False)
        return c - 1, afters, accs

    _, _, accs = lax.while_loop(keep_going, body, (i - 1, afters, accs))
    for h in range(heads):
        o_ref[0, :, h * HEAD_DIM:(h + 1) * HEAD_DIM] = accs[h].astype(o_ref.dtype)


def _stick_attention(proj3, *, tq, heads):
    b, s, _ = proj3.shape
    width = heads * HEAD_DIM
    assert s % tq == 0 and N_HEADS_B % heads == 0 and (3 * WIDTH_A) % width == 0
    groups = N_HEADS_B // heads
    base = 3 * WIDTH_A // width
    idx = jnp.arange(tq)
    tri = (idx[:, None] >= idx[None, :]).astype(_BF16)
    whole_seq = lambda part: pl.BlockSpec((1, s, width), lambda bi, hg, i: (bi, 0, base + part * groups + hg),
                                          pipeline_mode=pl.Buffered(1))
    return pl.pallas_call(
        functools.partial(_stick_kernel, tq=tq, heads=heads),
        grid=(b, groups, s // tq),
        in_specs=[
            pl.BlockSpec((1, tq, width), lambda bi, hg, i: (bi, i, base + hg)),
            whole_seq(1),
            whole_seq(2),
            pl.BlockSpec((tq, tq), lambda bi, hg, i: (0, 0)),
        ],
        out_specs=pl.BlockSpec((1, tq, width), lambda bi, hg, i: (bi, i, hg)),
        out_shape=jax.ShapeDtypeStruct((b, s, WIDTH_B), _BF16),
        compiler_params=_params(("parallel", "parallel", "arbitrary")),
        name="stick_breaking_attention",
    )(proj3, proj3, proj3, tri)


def _merge_kernel(ya_ref, yb_ref, ga_ref, gb_ref, h_ref, wa_ref, wb_ref, wo_ref, g_ref, o_ref):
    a = jnp.dot(ya_ref[...], wa_ref[...], preferred_element_type=_F32)
    bb = jnp.dot(yb_ref[...], wb_ref[...], preferred_element_type=_F32)
    m = (ga_ref[...].astype(_F32) * a + gb_ref[...].astype(_F32) * bb).astype(_BF16)
    mix = jnp.dot(m, wo_ref[...], preferred_element_type=_F32)
    o_ref[...] = h_ref[...] + _rmsnorm_rows(mix, g_ref[...])


def _merge(ya, yb, proj, h2d, wa, wb, wo, g, *, tm):
    t, d = h2d.shape
    assert t % tm == 0 and QKV_WIDTH % d == 0
    gate_blk = QKV_WIDTH // d
    full = lambda shape: pl.BlockSpec(shape, lambda i: (0, 0))
    return pl.pallas_call(
        _merge_kernel,
        grid=(t // tm,),
        in_specs=[
            pl.BlockSpec((tm, A_OUT), lambda i: (i, 0)),
            pl.BlockSpec((tm, WIDTH_B), lambda i: (i, 0)),
            pl.BlockSpec((tm, d), lambda i: (i, gate_blk)),
            pl.BlockSpec((tm, d), lambda i: (i, gate_blk + 1)),
            pl.BlockSpec((tm, d), lambda i: (i, 0)),
            full(wa.shape), full(wb.shape), full(wo.shape), full(g.shape),
        ],
        out_specs=pl.BlockSpec((tm, d), lambda i: (i, 0)),
        out_shape=jax.ShapeDtypeStruct((t, d), _F32),
        compiler_params=_params(("parallel",)),
        name="gated_merge",
    )(ya, yb, proj, proj, h2d, wa, wb, wo, g)


def _ffn_kernel(h_ref, hp_ref, gpre_ref, wg_ref, wv_ref, cwg_ref, cwv_ref, cbg_ref, cbv_ref, wd_ref,
                gpost_ref, o_ref, u_ref, *, tiles_per_seq, chunk):
    i = pl.program_id(0)
    j = pl.program_id(1)
    tm = h_ref.shape[0]
    n_chunks = tm // chunk

    @pl.when(j == 0)
    def _():
        halo = _rmsnorm_rows(hp_ref[...], gpre_ref[...])
        at_sequence_start = (i % tiles_per_seq) == 0
        u_ref[0:CONV_HALO, :] = jnp.where(at_sequence_start, 0.0, halo).astype(u_ref.dtype)
        o_ref[...] = jnp.zeros(o_ref.shape, o_ref.dtype)

        def body(c, carry):
            r0 = pl.multiple_of(c * chunk, chunk)
            x = h_ref[pl.ds(r0, chunk), :]
            u_ref[pl.ds(CONV_HALO + r0, chunk), :] = _rmsnorm_rows(x, gpre_ref[...]).astype(u_ref.dtype)
            return carry
        lax.fori_loop(0, n_chunks, body, 0)

    u = u_ref[...]

    def conv_branch(w_ref, cw_ref, cb_ref):
        x = jnp.dot(u, w_ref[...], preferred_element_type=_F32)
        cw = cw_ref[...]
        y = cb_ref[...] + cw[0:1, :] * pltpu.roll(x, 2, 0)
        y = y + cw[1:2, :] * pltpu.roll(x, 1, 0)
        y = y + cw[2:3, :] * x
        return y[CONV_HALO:, :]

    gate = conv_branch(wg_ref, cwg_ref, cbg_ref)
    value = conv_branch(wv_ref, cwv_ref, cbv_ref)
    act = (jax.nn.gelu(gate, approximate=True) * value).astype(_BF16)
    o_ref[...] += jnp.dot(act, wd_ref[...], preferred_element_type=_F32)

    @pl.when(j == pl.num_programs(1) - 1)
    def _():
        def body(c, carry):
            r0 = pl.multiple_of(c * chunk, chunk)
            f = o_ref[pl.ds(r0, chunk), :]
            o_ref[pl.ds(r0, chunk), :] = h_ref[pl.ds(r0, chunk), :] + _rmsnorm_rows(f, gpost_ref[...])
            return carry
        lax.fori_loop(0, n_chunks, body, 0)


def _ffn(h2d, gpre, w_up, conv_w, conv_b, w_down, gpost, *, tm, tf, seq):
    t, d = h2d.shape
    d_ff = w_down.shape[0]
    assert t % tm == 0 and seq % tm == 0 and d_ff % tf == 0 and tm % CONV_HALO == 0
    nj = d_ff // tf
    halo_blocks = tm // CONV_HALO
    return pl.pallas_call(
        functools.partial(_ffn_kernel, tiles_per_seq=seq // tm, chunk=min(tm, 128)),
        grid=(t // tm, nj),
        in_specs=[
            pl.BlockSpec((tm, d), lambda i, j: (i, 0)),
            pl.BlockSpec((CONV_HALO, d), lambda i, j: (jnp.maximum(i * halo_blocks - 1, 0), 0)),
            pl.BlockSpec((1, d), lambda i, j: (0, 0)),
            pl.BlockSpec((d, tf), lambda i, j: (0, j)),
            pl.BlockSpec((d, tf), lambda i, j: (0, nj + j)),
            pl.BlockSpec((CONV_WIDTH, tf), lambda i, j: (0, j)),
            pl.BlockSpec((CONV_WIDTH, tf), lambda i, j: (0, nj + j)),
            pl.BlockSpec((1, tf), lambda i, j: (0, j)),
            pl.BlockSpec((1, tf), lambda i, j: (0, nj + j)),
            pl.BlockSpec((tf, d), lambda i, j: (j, 0)),
            pl.BlockSpec((1, d), lambda i, j: (0, 0)),
        ],
        out_specs=pl.BlockSpec((tm, d), lambda i, j: (i, 0)),
        out_shape=jax.ShapeDtypeStruct((t, d), _F32),
        scratch_shapes=[pltpu.VMEM((CONV_HALO + tm, d), _BF16)],
        compiler_params=_params(("parallel", "arbitrary")),
        name="conv_ffn",
    )(h2d, h2d, gpre, w_up, w_up, conv_w, conv_w, conv_b, conv_b, w_down, gpost)


def _ple_kernel(h_ref, p_ref, wg_ref, wi_ref, o_ref):
    h = h_ref[...]
    gate = jax.nn.sigmoid(jnp.dot(h.astype(_BF16), wg_ref[...], preferred_element_type=_F32))
    e = jnp.dot(p_ref[...].astype(_BF16), wi_ref[...], preferred_element_type=_F32)
    o_ref[...] = h + gate * e


def _ple(h2d, p2d, w_gate, w_in, *, tm):
    t, d = h2d.shape
    pd = p2d.shape[1]
    assert t % tm == 0
    return pl.pallas_call(
        _ple_kernel,
        grid=(t // tm,),
        in_specs=[
            pl.BlockSpec((tm, d), lambda i: (i, 0)),
            pl.BlockSpec((tm, pd), lambda i: (i, 0)),
            pl.BlockSpec((d, d), lambda i: (0, 0)),
            pl.BlockSpec((pd, d), lambda i: (0, 0)),
        ],
        out_specs=pl.BlockSpec((tm, d), lambda i: (i, 0)),
        out_shape=jax.ShapeDtypeStruct((t, d), _F32),
        compiler_params=_params(("parallel",)),
        name="layer_embedding",
    )(h2d, p2d, w_gate, w_in)


def _tiles(t, seq, d, d_in, d_ff):
    def largest(n, cands):
        return next(c for c in cands if n % c == 0)
    return dict(
        proj_tm=largest(t, (1024, 512, 256, 128)),
        proj_tn=largest(math.gcd(d_in, QKV_WIDTH), (1024, 512, 256, 128)),
        stick_tq=largest(seq, (256, 128)),
        merge_tm=largest(t, (256, 128)),
        ffn_tm=largest(seq, (512, 256, 128)),
        ffn_tf=largest(d_ff, (512, 256, 128)),
        ple_tm=largest(t, (512, 256, 128)),
    )


def kernel(x, p, g_mix_pre, w_in, w_branch_a, w_branch_b, w_out, g_mix_post, g_ffn_pre, w_up, conv_w,
           conv_b, w_down, g_ffn_post, w_ple_in, w_ple_gate):
    b, s, d = x.shape
    depth = w_in.shape[0]
    d_in = w_in.shape[2]
    d_ff = w_down.shape[1]
    t = b * s
    assert d_in == QKV_WIDTH + 2 * d
    tiles = _tiles(t, s, d, d_in, d_ff)

    q_scale = 1.0 / math.sqrt(HEAD_DIM)
    col = jnp.arange(d_in)
    is_q = (col < WIDTH_A) | ((col >= 3 * WIDTH_A) & (col < 3 * WIDTH_A + WIDTH_B))
    colscale = jnp.where(is_q, q_scale, 1.0).astype(_F32)[None, :]

    h = x.reshape(t, d)
    for i in range(depth):
        row = lambda v: v[i][None, :]
        proj = _in_proj(h, row(g_mix_pre), colscale, w_in[i].astype(_BF16),
                        tm=tiles["proj_tm"], tn=tiles["proj_tn"])
        proj3 = proj.reshape(b, s, d_in)
        ya = _dilated_attention(proj3).reshape(t, A_OUT)
        yb = _stick_attention(proj3, tq=tiles["stick_tq"], heads=N_HEADS_B).reshape(t, WIDTH_B)
        h = _merge(ya, yb, proj, h, w_branch_a[i].astype(_BF16), w_branch_b[i].astype(_BF16),
                   w_out[i].astype(_BF16), row(g_mix_post), tm=tiles["merge_tm"])
        h = _ffn(h, row(g_ffn_pre), w_up[i].astype(_BF16), conv_w[i], conv_b[i][None, :],
                 w_down[i].astype(_BF16), row(g_ffn_post), tm=tiles["ffn_tm"], tf=tiles["ffn_tf"], seq=s)
        h = _ple(h, p[i].reshape(t, -1), w_ple_gate[i].astype(_BF16), w_ple_in[i].astype(_BF16),
                 tm=tiles["ple_tm"])
    return h.reshape(b, s, d)
```

```python
import functools
import math

import jax
import jax.numpy as jnp
from jax import lax
from jax.experimental import pallas as pl
from jax.experimental.pallas import tpu as pltpu

HEAD_DIM = 128
DIL_GROUPS = ((128, 1), (512, 4), (2048, 16))
HEADS_PER_GROUP = 4
N_HEADS_A = HEADS_PER_GROUP * len(DIL_GROUPS)
N_HEADS_B = 4
WIDTH_A = N_HEADS_A * HEAD_DIM
WIDTH_B = N_HEADS_B * HEAD_DIM
A_OUT = HEADS_PER_GROUP * HEAD_DIM
QKV_WIDTH = 3 * WIDTH_A + 3 * WIDTH_B
CONV_WIDTH = 3
RMS_EPS = 1e-6
ALIBI_MAX = 8.0
MASKED = -1e30
EXP_UNDERFLOW = -110.0
QBLK = 128
SPAN = 2048
DIL_BLOCKS_PER_TRIP = 4
CONV_HALO = 16
VMEM_LIMIT = 56 * 1024 * 1024

_F32 = jnp.float32
_BF16 = jnp.bfloat16


def _rmsnorm_rows(x, g):
    ms = jnp.mean(x * x, axis=-1, keepdims=True)
    return x * lax.rsqrt(ms + RMS_EPS) * g


def _params(sem):
    return pltpu.CompilerParams(dimension_semantics=sem, vmem_limit_bytes=VMEM_LIMIT)


def _in_proj_kernel(x_ref, g_ref, cs_ref, w_ref, o_ref, u_ref, *, gate_tile0, chunk):
    j = pl.program_id(1)
    tm = x_ref.shape[0]

    @pl.when(j == 0)
    def _():
        def body(c, carry):
            r0 = pl.multiple_of(c * chunk, chunk)
            x = x_ref[pl.ds(r0, chunk), :]
            u_ref[pl.ds(r0, chunk), :] = _rmsnorm_rows(x, g_ref[...]).astype(u_ref.dtype)
            return carry
        lax.fori_loop(0, tm // chunk, body, 0)

    acc = jnp.dot(u_ref[...], w_ref[...], preferred_element_type=_F32)

    o_ref[...] = jnp.where(j >= gate_tile0, jax.nn.sigmoid(acc), acc * cs_ref[...]).astype(o_ref.dtype)


def _in_proj(h2d, g, colscale, w, *, tm, tn):
    t, d = h2d.shape
    d_in = w.shape[1]
    assert t % tm == 0 and d_in % tn == 0 and QKV_WIDTH % tn == 0
    return pl.pallas_call(
        functools.partial(_in_proj_kernel, gate_tile0=QKV_WIDTH // tn, chunk=min(tm, 128)),
        grid=(t // tm, d_in // tn),
        in_specs=[
            pl.BlockSpec((tm, d), lambda i, j: (i, 0)),
            pl.BlockSpec((1, d), lambda i, j: (0, 0)),
            pl.BlockSpec((1, tn), lambda i, j: (0, j)),
            pl.BlockSpec((d, tn), lambda i, j: (0, j)),
        ],
        out_specs=pl.BlockSpec((tm, tn), lambda i, j: (i, j)),
        out_shape=jax.ShapeDtypeStruct((t, d_in), _BF16),
        scratch_shapes=[pltpu.VMEM((tm, d), _BF16)],
        compiler_params=_params(("parallel", "arbitrary")),
        name="in_proj",
    )(h2d, g, colscale, w)


def _dilated_kernel(q0, q1, q2, k0, k1, k2, v0, v1, v2, b0, b1, b2, o_ref,
                    kb0, kb1, kb2, vb0, vb1, vb2):
    i = pl.program_id(2)
    q_refs, bias_refs = (q0, q1, q2), (b0, b1, b2)
    k_bufs, v_bufs = (kb0, kb1, kb2), (vb0, vb1, vb2)

    for buf, new in zip(k_bufs + v_bufs, (k0, k1, k2, v0, v1, v2)):
        @pl.when(i == 0)
        def _(buf=buf):
            buf[0:SPAN, :] = jnp.zeros((SPAN, HEAD_DIM), buf.dtype)

        @pl.when(i > 0)
        def _(buf=buf):
            buf[0:SPAN, :] = buf[SPAN:2 * SPAN, :]

        buf[SPAN:2 * SPAN, :] = new[0]

    first_valid_row = jnp.where(i == 0, SPAN, 0)

    def masked_scores(o):
        scores, starts = [], []
        for g, (window, _) in enumerate(DIL_GROUPS):
            n = window + QBLK
            start = pl.multiple_of(o + SPAN - window, QBLK)
            q = q_refs[g][0, pl.ds(o, QBLK), :]
            kk = k_bufs[g][pl.ds(start, n), :]
            sc = lax.dot_general(q, kk, (((1,), (1,)), ((), ())), preferred_element_type=_F32)
            sc = sc + bias_refs[g][0]
            kj = lax.broadcasted_iota(jnp.int32, (QBLK, n), 1)
            scores.append(jnp.where(kj >= first_valid_row - start, sc, MASKED))
            starts.append(start)
        return scores, starts

    def joint_softmax_weights(scores):
        m = functools.reduce(jnp.maximum, [jnp.max(sc, axis=-1, keepdims=True) for sc in scores])
        ps = [jnp.exp(sc - m) for sc in scores]
        denom = functools.reduce(jnp.add, [jnp.sum(p, axis=-1, keepdims=True) for p in ps])
        return [p.astype(_BF16) for p in ps], denom

    def weighted_values(ps, starts):
        acc = jnp.zeros((QBLK, HEAD_DIM), _F32)
        for g, (window, _) in enumerate(DIL_GROUPS):
            vv = v_bufs[g][pl.ds(starts[g], window + QBLK), :]
            acc = acc + jnp.dot(ps[g], vv, preferred_element_type=_F32)
        return acc

    def body(s, carry):
        offs = [pl.multiple_of((s * DIL_BLOCKS_PER_TRIP + t) * QBLK, QBLK) for t in range(DIL_BLOCKS_PER_TRIP)]
        staged = [masked_scores(o) for o in offs]
        weights = [joint_softmax_weights(scores) for scores, _ in staged]
        for o, (_, starts), (ps, denom) in zip(offs, staged, weights):
            o_ref[0, pl.ds(o, QBLK), :] = (weighted_values(ps, starts) / denom).astype(o_ref.dtype)
        return carry

    lax.fori_loop(0, SPAN // (QBLK * DIL_BLOCKS_PER_TRIP), body, 0)


def _dilated_bias_tables():
    slopes = jnp.exp2(-ALIBI_MAX * jnp.arange(1, N_HEADS_A + 1, dtype=_F32) / N_HEADS_A)
    tables = []
    for g, (window, dilation) in enumerate(DIL_GROUPS):
        qi = jnp.arange(QBLK)[:, None]
        kj = jnp.arange(window + QBLK)[None, :]
        dist = qi + window - kj
        valid = (dist >= 0) & (dist <= window) & (dist % dilation == 0)
        sl = slopes[g * HEADS_PER_GROUP:(g + 1) * HEADS_PER_GROUP]
        bias = -sl[:, None, None] * dist.astype(_F32)[None]
        tables.append(jnp.where(valid[None], bias, MASKED))
    return tables


def _dilated_attention(proj3):
    b, s, _ = proj3.shape
    assert s % SPAN == 0
    n_heads = N_HEADS_A

    def col_spec(base):
        return [pl.BlockSpec((1, SPAN, HEAD_DIM), (lambda bi, hh, i, c=base + g * HEADS_PER_GROUP: (bi, i, c + hh)))
                for g in range(len(DIL_GROUPS))]

    tables = _dilated_bias_tables()
    bias_specs = [pl.BlockSpec((1, QBLK, w + QBLK), lambda bi, hh, i: (hh, 0, 0)) for w, _ in DIL_GROUPS]
    return pl.pallas_call(
        _dilated_kernel,
        grid=(b, HEADS_PER_GROUP, s // SPAN),
        in_specs=col_spec(0) + col_spec(n_heads) + col_spec(2 * n_heads) + bias_specs,
        out_specs=pl.BlockSpec((1, SPAN, HEAD_DIM), lambda bi, hh, i: (bi, i, hh)),
        out_shape=jax.ShapeDtypeStruct((b, s, A_OUT), _BF16),
        scratch_shapes=[pltpu.VMEM((2 * SPAN, HEAD_DIM), _BF16) for _ in range(6)],
        compiler_params=_params(("arbitrary", "arbitrary", "arbitrary")),
        name="dilated_attention",
    )(*([proj3] * 9), *tables)


def _stick_kernel(q_ref, k_ref, v_ref, tri_ref, o_ref, *, tq, heads):
    i = pl.program_id(2)
    row = lax.broadcasted_iota(jnp.int32, (tq, tq), 0)
    col = lax.broadcasted_iota(jnp.int32, (tq, tq), 1)
    before = col < row

    def chunk(c, afters, accs, diagonal):
        k0 = pl.multiple_of(c * tq, tq)
        tri = tri_ref[...]
        lanes = [slice(h * HEAD_DIM, (h + 1) * HEAD_DIM) for h in range(heads)]
        zs = [lax.dot_general(q_ref[0, :, lanes[h]], k_ref[0, pl.ds(k0, tq), lanes[h]],
                              (((1,), (1,)), ((), ())), preferred_element_type=_F32) for h in range(heads)]
        csums = []
        for h in range(heads):
            z = zs[h]
            ln = -(jnp.maximum(z, 0.0) + jnp.log(1.0 + jnp.exp(-jnp.abs(z))))
            if diagonal:
                ln = jnp.where(before, ln, 0.0)
            hi = ln.astype(_BF16)
            lo = (ln - hi.astype(_F32)).astype(_BF16)
            csums.append(jnp.dot(hi, tri, preferred_element_type=_F32)
                         + jnp.dot(lo, tri, preferred_element_type=_F32))
        new_afters, new_accs = [], []
        for h in range(heads):
            w = jnp.exp(zs[h] + csums[h] + afters[h])
            if diagonal:
                w = jnp.where(before, w, 0.0)
            new_accs.append(accs[h] + jnp.dot(w.astype(_BF16), v_ref[0, pl.ds(k0, tq), lanes[h]],
                                              preferred_element_type=_F32))
            new_afters.append(afters[h] + csums[h][:, 0:1])
        return tuple(new_afters), tuple(new_accs)

    afters, accs = chunk(i, (jnp.zeros((tq, 1), _F32),) * heads, (jnp.zeros((tq, HEAD_DIM), _F32),) * heads, True)

    def keep_going(carry):
        c, afters, _ = carry
        return jnp.logical_and(c >= 0, jnp.max(functools.reduce(jnp.maximum, afters)) > EXP_UNDERFLOW)

    def body(carry):
        c, afters, accs = carry
        afters, accs = chunk(c, afters, accs, False)
        return c - 1, afters, accs

    _, _, accs = lax.while_loop(keep_going, body, (i - 1, afters, accs))
    for h in range(heads):
        o_ref[0, :, h * HEAD_DIM:(h + 1) * HEAD_DIM] = accs[h].astype(o_ref.dtype)


def _stick_attention(proj3, *, tq, heads):
    b, s, _ = proj3.shape
    width = heads * HEAD_DIM
    assert s % tq == 0 and N_HEADS_B % heads == 0 and (3 * WIDTH_A) % width == 0
    groups = N_HEADS_B // heads
    base = 3 * WIDTH_A // width
    idx = jnp.arange(tq)
    tri = (idx[:, None] >= idx[None, :]).astype(_BF16)
    whole_seq = lambda part: pl.BlockSpec((1, s, width), lambda bi, hg, i: (bi, 0, base + part * groups + hg),
                                          pipeline_mode=pl.Buffered(1))
    return pl.pallas_call(
        functools.partial(_stick_kernel, tq=tq, heads=heads),
        grid=(b, groups, s // tq),
        in_specs=[
            pl.BlockSpec((1, tq, width), lambda bi, hg, i: (bi, i, base + hg)),
            whole_seq(1),
            whole_seq(2),
            pl.BlockSpec((tq, tq), lambda bi, hg, i: (0, 0)),
        ],
        out_specs=pl.BlockSpec((1, tq, width), lambda bi, hg, i: (bi, i, hg)),
        out_shape=jax.ShapeDtypeStruct((b, s, WIDTH_B), _BF16),
        compiler_params=_params(("parallel", "parallel", "arbitrary")),
        name="stick_breaking_attention",
    )(proj3, proj3, proj3, tri)


def _merge_kernel(ya_ref, yb_ref, ga_ref, gb_ref, h_ref, wa_ref, wb_ref, wo_ref, g_ref, o_ref):
    a = jnp.dot(ya_ref[...], wa_ref[...], preferred_element_type=_F32)
    bb = jnp.dot(yb_ref[...], wb_ref[...], preferred_element_type=_F32)
    m = (ga_ref[...].astype(_F32) * a + gb_ref[...].astype(_F32) * bb).astype(_BF16)
    mix = jnp.dot(m, wo_ref[...], preferred_element_type=_F32)
    o_ref[...] = h_ref[...] + _rmsnorm_rows(mix, g_ref[...])


def _merge(ya, yb, proj, h2d, wa, wb, wo, g, *, tm):
    t, d = h2d.shape
    assert t % tm == 0 and QKV_WIDTH % d == 0
    gate_blk = QKV_WIDTH // d
    full = lambda shape: pl.BlockSpec(shape, lambda i: (0, 0), pipeline_mode=pl.Buffered(1))
    return pl.pallas_call(
        _merge_kernel,
        grid=(t // tm,),
        in_specs=[
            pl.BlockSpec((tm, A_OUT), lambda i: (i, 0)),
            pl.BlockSpec((tm, WIDTH_B), lambda i: (i, 0)),
            pl.BlockSpec((tm, d), lambda i: (i, gate_blk)),
            pl.BlockSpec((tm, d), lambda i: (i, gate_blk + 1)),
            pl.BlockSpec((tm, d), lambda i: (i, 0)),
            full(wa.shape), full(wb.shape), full(wo.shape), full(g.shape),
        ],
        out_specs=pl.BlockSpec((tm, d), lambda i: (i, 0)),
        out_shape=jax.ShapeDtypeStruct((t, d), _F32),
        compiler_params=_params(("parallel",)),
        name="gated_merge",
    )(ya, yb, proj, proj, h2d, wa, wb, wo, g)


def _ffn_kernel(h_ref, hp_ref, gpre_ref, wg_ref, wv_ref, cwg_ref, cwv_ref, cbg_ref, cbv_ref, wd_ref,
                gpost_ref, o_ref, u_ref, *, tiles_per_seq, chunk):
    i = pl.program_id(0)
    j = pl.program_id(1)
    tm = h_ref.shape[0]
    n_chunks = tm // chunk

    @pl.when(j == 0)
    def _():
        halo = _rmsnorm_rows(hp_ref[...], gpre_ref[...])
        at_sequence_start = (i % tiles_per_seq) == 0
        u_ref[0:CONV_HALO, :] = jnp.where(at_sequence_start, 0.0, halo).astype(u_ref.dtype)
        o_ref[...] = jnp.zeros(o_ref.shape, o_ref.dtype)

        def body(c, carry):
            r0 = pl.multiple_of(c * chunk, chunk)
            x = h_ref[pl.ds(r0, chunk), :]
            u_ref[pl.ds(CONV_HALO + r0, chunk), :] = _rmsnorm_rows(x, gpre_ref[...]).astype(u_ref.dtype)
            return carry
        lax.fori_loop(0, n_chunks, body, 0)

    u = u_ref[...]

    def conv_branch(w_ref, cw_ref, cb_ref):
        x = jnp.dot(u, w_ref[...], preferred_element_type=_F32)
        cw = cw_ref[...]
        y = cb_ref[...] + cw[0:1, :] * pltpu.roll(x, 2, 0)
        y = y + cw[1:2, :] * pltpu.roll(x, 1, 0)
        y = y + cw[2:3, :] * x
        return y[CONV_HALO:, :]

    gate = conv_branch(wg_ref, cwg_ref, cbg_ref)
    value = conv_branch(wv_ref, cwv_ref, cbv_ref)
    act = (jax.nn.gelu(gate, approximate=True) * value).astype(_BF16)
    o_ref[...] += jnp.dot(act, wd_ref[...], preferred_element_type=_F32)

    @pl.when(j == pl.num_programs(1) - 1)
    def _():
        def body(c, carry):
            r0 = pl.multiple_of(c * chunk, chunk)
            f = o_ref[pl.ds(r0, chunk), :]
            o_ref[pl.ds(r0, chunk), :] = h_ref[pl.ds(r0, chunk), :] + _rmsnorm_rows(f, gpost_ref[...])
            return carry
        lax.fori_loop(0, n_chunks, body, 0)


def _ffn(h2d, gpre, w_up, conv_w, conv_b, w_down, gpost, *, tm, tf, seq):
    t, d = h2d.shape
    d_ff = w_down.shape[0]
    assert t % tm == 0 and seq % tm == 0 and d_ff % tf == 0 and tm % CONV_HALO == 0
    nj = d_ff // tf
    halo_blocks = tm // CONV_HALO
    return pl.pallas_call(
        functools.partial(_ffn_kernel, tiles_per_seq=seq // tm, chunk=min(tm, 128)),
        grid=(t // tm, nj),
        in_specs=[
            pl.BlockSpec((tm, d), lambda i, j: (i, 0)),
            pl.BlockSpec((CONV_HALO, d), lambda i, j: (jnp.maximum(i * halo_blocks - 1, 0), 0)),
            pl.BlockSpec((1, d), lambda i, j: (0, 0)),
            pl.BlockSpec((d, tf), lambda i, j: (0, j)),
            pl.BlockSpec((d, tf), lambda i, j: (0, nj + j)),
            pl.BlockSpec((CONV_WIDTH, tf), lambda i, j: (0, j)),
            pl.BlockSpec((CONV_WIDTH, tf), lambda i, j: (0, nj + j)),
            pl.BlockSpec((1, tf), lambda i, j: (0, j)),
            pl.BlockSpec((1, tf), lambda i, j: (0, nj + j)),
            pl.BlockSpec((tf, d), lambda i, j: (j, 0)),
            pl.BlockSpec((1, d), lambda i, j: (0, 0)),
        ],
        out_specs=pl.BlockSpec((tm, d), lambda i, j: (i, 0)),
        out_shape=jax.ShapeDtypeStruct((t, d), _F32),
        scratch_shapes=[pltpu.VMEM((CONV_HALO + tm, d), _BF16)],
        compiler_params=_params(("parallel", "arbitrary")),
        name="conv_ffn",
    )(h2d, h2d, gpre, w_up, w_up, conv_w, conv_w, conv_b, conv_b, w_down, gpost)


def _ple_kernel(h_ref, p_ref, wg_ref, wi_ref, o_ref):
    h = h_ref[...]
    gate = jax.nn.sigmoid(jnp.dot(h.astype(_BF16), wg_ref[...], preferred_element_type=_F32))
    e = jnp.dot(p_ref[...].astype(_BF16), wi_ref[...], preferred_element_type=_F32)
    o_ref[...] = h + gate * e


def _ple(h2d, p2d, w_gate, w_in, *, tm):
    t, d = h2d.shape
    pd = p2d.shape[1]
    assert t % tm == 0
    return pl.pallas_call(
        _ple_kernel,
        grid=(t // tm,),
        in_specs=[
            pl.BlockSpec((tm, d), lambda i: (i, 0)),
            pl.BlockSpec((tm, pd), lambda i: (i, 0)),
            pl.BlockSpec((d, d), lambda i: (0, 0)),
            pl.BlockSpec((pd, d), lambda i: (0, 0)),
        ],
        out_specs=pl.BlockSpec((tm, d), lambda i: (i, 0)),
        out_shape=jax.ShapeDtypeStruct((t, d), _F32),
        compiler_params=_params(("parallel",)),
        name="layer_embedding",
    )(h2d, p2d, w_gate, w_in)


def _tiles(t, seq, d, d_in, d_ff):
    def largest(n, cands):
        return next(c for c in cands if n % c == 0)
    return dict(
        proj_tm=largest(t, (1024, 512, 256, 128)),
        proj_tn=largest(math.gcd(d_in, QKV_WIDTH), (2048, 1024, 512, 256, 128)),
        stick_tq=largest(seq, (256, 128)),
        merge_tm=largest(t, (512, 256, 128)),
        ffn_tm=largest(seq, (512, 256, 128)),
        ffn_tf=largest(d_ff, (512, 256, 128)),
        ple_tm=largest(t, (512, 256, 128)),
    )


def kernel(x, p, g_mix_pre, w_in, w_branch_a, w_branch_b, w_out, g_mix_post, g_ffn_pre, w_up, conv_w,
           conv_b, w_down, g_ffn_post, w_ple_in, w_ple_gate):
    b, s, d = x.shape
    depth = w_in.shape[0]
    d_in = w_in.shape[2]
    d_ff = w_down.shape[1]
    t = b * s
    assert d_in == QKV_WIDTH + 2 * d
    tiles = _tiles(t, s, d, d_in, d_ff)

    q_scale = 1.0 / math.sqrt(HEAD_DIM)
    col = jnp.arange(d_in)
    is_q = (col < WIDTH_A) | ((col >= 3 * WIDTH_A) & (col < 3 * WIDTH_A + WIDTH_B))
    colscale = jnp.where(is_q, q_scale, 1.0).astype(_F32)[None, :]

    h = x.reshape(t, d)
    for i in range(depth):
        row = lambda v: v[i][None, :]
        proj = _in_proj(h, row(g_mix_pre), colscale, w_in[i].astype(_BF16),
                        tm=tiles["proj_tm"], tn=tiles["proj_tn"])
        proj3 = proj.reshape(b, s, d_in)
        ya = _dilated_attention(proj3).reshape(t, A_OUT)
        yb = _stick_attention(proj3, tq=tiles["stick_tq"], heads=N_HEADS_B).reshape(t, WIDTH_B)
        h = _merge(ya, yb, proj, h, w_branch_a[i].astype(_BF16), w_branch_b[i].astype(_BF16),
                   w_out[i].astype(_BF16), row(g_mix_post), tm=tiles["merge_tm"])
        h = _ffn(h, row(g_ffn_pre), w_up[i].astype(_BF16), conv_w[i], conv_b[i][None, :],
                 w_down[i].astype(_BF16), row(g_ffn_post), tm=tiles["ffn_tm"], tf=tiles["ffn_tf"], seq=s)
        h = _ple(h, p[i].reshape(t, -1), w_ple_gate[i].astype(_BF16), w_ple_in[i].astype(_BF16),
                 tm=tiles["ple_tm"])
    return h.reshape(b, s, d)
```

```python
import functools
import math

import jax
import jax.numpy as jnp
from jax import lax
from jax.experimental import pallas as pl
from jax.experimental.pallas import tpu as pltpu

HEAD_DIM = 128
DIL_GROUPS = ((128, 1), (512, 4), (2048, 16))
HEADS_PER_GROUP = 4
N_HEADS_A = HEADS_PER_GROUP * len(DIL_GROUPS)
N_HEADS_B = 4
WIDTH_A = N_HEADS_A * HEAD_DIM
WIDTH_B = N_HEADS_B * HEAD_DIM
A_OUT = HEADS_PER_GROUP * HEAD_DIM
QKV_WIDTH = 3 * WIDTH_A + 3 * WIDTH_B
CONV_WIDTH = 3
RMS_EPS = 1e-6
ALIBI_MAX = 8.0
MASKED = -1e30
EXP_UNDERFLOW = -110.0
QBLK = 128
SPAN = 2048
DIL_BLOCKS_PER_TRIP = 4
CONV_HALO = 16
VMEM_LIMIT = 56 * 1024 * 1024

_F32 = jnp.float32
_BF16 = jnp.bfloat16


def _rmsnorm_rows(x, g):
    ms = jnp.mean(x * x, axis=-1, keepdims=True)
    return x * lax.rsqrt(ms + RMS_EPS) * g


def _params(sem):
    return pltpu.CompilerParams(dimension_semantics=sem, vmem_limit_bytes=VMEM_LIMIT)


def _in_proj_kernel(x_ref, g_ref, cs_ref, w_ref, o_ref, u_ref, *, gate_tile0, chunk):
    j = pl.program_id(1)
    tm = x_ref.shape[0]

    @pl.when(j == 0)
    def _():
        def body(c, carry):
            r0 = pl.multiple_of(c * chunk, chunk)
            x = x_ref[pl.ds(r0, chunk), :]
            u_ref[pl.ds(r0, chunk), :] = _rmsnorm_rows(x, g_ref[...]).astype(u_ref.dtype)
            return carry
        lax.fori_loop(0, tm // chunk, body, 0)

    acc = jnp.dot(u_ref[...], w_ref[...], preferred_element_type=_F32)

    o_ref[...] = jnp.where(j >= gate_tile0, jax.nn.sigmoid(acc), acc * cs_ref[...]).astype(o_ref.dtype)


def _in_proj(h2d, g, colscale, w, *, tm, tn):
    t, d = h2d.shape
    d_in = w.shape[1]
    assert t % tm == 0 and d_in % tn == 0 and QKV_WIDTH % tn == 0
    return pl.pallas_call(
        functools.partial(_in_proj_kernel, gate_tile0=QKV_WIDTH // tn, chunk=min(tm, 128)),
        grid=(t // tm, d_in // tn),
        in_specs=[
            pl.BlockSpec((tm, d), lambda i, j: (i, 0)),
            pl.BlockSpec((1, d), lambda i, j: (0, 0)),
            pl.BlockSpec((1, tn), lambda i, j: (0, j)),
            pl.BlockSpec((d, tn), lambda i, j: (0, j)),
        ],
        out_specs=pl.BlockSpec((tm, tn), lambda i, j: (i, j)),
        out_shape=jax.ShapeDtypeStruct((t, d_in), _BF16),
        scratch_shapes=[pltpu.VMEM((tm, d), _BF16)],
        compiler_params=_params(("parallel", "arbitrary")),
        name="in_proj",
    )(h2d, g, colscale, w)


def _dilated_kernel(q0, q1, q2, k0, k1, k2, v0, v1, v2, b0, b1, b2, o_ref,
                    kb0, kb1, kb2, vb0, vb1, vb2):
    i = pl.program_id(2)
    q_refs, bias_refs = (q0, q1, q2), (b0, b1, b2)
    k_bufs, v_bufs = (kb0, kb1, kb2), (vb0, vb1, vb2)

    for buf, new in zip(k_bufs + v_bufs, (k0, k1, k2, v0, v1, v2)):
        @pl.when(i == 0)
        def _(buf=buf):
            buf[0:SPAN, :] = jnp.zeros((SPAN, HEAD_DIM), buf.dtype)

        @pl.when(i > 0)
        def _(buf=buf):
            buf[0:SPAN, :] = buf[SPAN:2 * SPAN, :]

        buf[SPAN:2 * SPAN, :] = new[0]

    first_valid_row = jnp.where(i == 0, SPAN, 0)

    def masked_scores(o):
        scores, starts = [], []
        for g, (window, _) in enumerate(DIL_GROUPS):
            n = window + QBLK
            start = pl.multiple_of(o + SPAN - window, QBLK)
            q = q_refs[g][0, pl.ds(o, QBLK), :]
            kk = k_bufs[g][pl.ds(start, n), :]
            sc = lax.dot_general(q, kk, (((1,), (1,)), ((), ())), preferred_element_type=_F32)
            sc = sc + bias_refs[g][0]
            kj = lax.broadcasted_iota(jnp.int32, (QBLK, n), 1)
            scores.append(jnp.where(kj >= first_valid_row - start, sc, MASKED))
            starts.append(start)
        return scores, starts

    def joint_softmax_weights(scores):
        m = functools.reduce(jnp.maximum, [jnp.max(sc, axis=-1, keepdims=True) for sc in scores])
        ps = [jnp.exp(sc - m) for sc in scores]
        denom = functools.reduce(jnp.add, [jnp.sum(p, axis=-1, keepdims=True) for p in ps])
        return [p.astype(_BF16) for p in ps], denom

    def weighted_values(ps, starts):
        acc = jnp.zeros((QBLK, HEAD_DIM), _F32)
        for g, (window, _) in enumerate(DIL_GROUPS):
            vv = v_bufs[g][pl.ds(starts[g], window + QBLK), :]
            acc = acc + jnp.dot(ps[g], vv, preferred_element_type=_F32)
        return acc

    def body(s, carry):
        offs = [pl.multiple_of((s * DIL_BLOCKS_PER_TRIP + t) * QBLK, QBLK) for t in range(DIL_BLOCKS_PER_TRIP)]
        staged = [masked_scores(o) for o in offs]
        weights = [joint_softmax_weights(scores) for scores, _ in staged]
        for o, (_, starts), (ps, denom) in zip(offs, staged, weights):
            o_ref[0, pl.ds(o, QBLK), :] = (weighted_values(ps, starts) / denom).astype(o_ref.dtype)
        return carry

    lax.fori_loop(0, SPAN // (QBLK * DIL_BLOCKS_PER_TRIP), body, 0)


def _dilated_bias_tables():
    slopes = jnp.exp2(-ALIBI_MAX * jnp.arange(1, N_HEADS_A + 1, dtype=_F32) / N_HEADS_A)
    tables = []
    for g, (window, dilation) in enumerate(DIL_GROUPS):
        qi = jnp.arange(QBLK)[:, None]
        kj = jnp.arange(window + QBLK)[None, :]
        dist = qi + window - kj
        valid = (dist >= 0) & (dist <= window) & (dist % dilation == 0)
        sl = slopes[g * HEADS_PER_GROUP:(g + 1) * HEADS_PER_GROUP]
        bias = -sl[:, None, None] * dist.astype(_F32)[None]
        tables.append(jnp.where(valid[None], bias, MASKED))
    return tables


def _dilated_attention(proj3):
    b, s, _ = proj3.shape
    assert s % SPAN == 0
    n_heads = N_HEADS_A

    def col_spec(base):
        return [pl.BlockSpec((1, SPAN, HEAD_DIM), (lambda bi, hh, i, c=base + g * HEADS_PER_GROUP: (bi, i, c + hh)))
                for g in range(len(DIL_GROUPS))]

    tables = _dilated_bias_tables()
    bias_specs = [pl.BlockSpec((1, QBLK, w + QBLK), lambda bi, hh, i: (hh, 0, 0)) for w, _ in DIL_GROUPS]
    return pl.pallas_call(
        _dilated_kernel,
        grid=(b, HEADS_PER_GROUP, s // SPAN),
        in_specs=col_spec(0) + col_spec(n_heads) + col_spec(2 * n_heads) + bias_specs,
        out_specs=pl.BlockSpec((1, SPAN, HEAD_DIM), lambda bi, hh, i: (bi, i, hh)),
        out_shape=jax.ShapeDtypeStruct((b, s, A_OUT), _BF16),
        scratch_shapes=[pltpu.VMEM((2 * SPAN, HEAD_DIM), _BF16) for _ in range(6)],
        compiler_params=_params(("arbitrary", "arbitrary", "arbitrary")),
        name="dilated_attention",
    )(*([proj3] * 9), *tables)


def _stick_kernel(q_ref, k_ref, v_ref, tri_ref, o_ref, *, tq, heads):
    i = pl.program_id(2)
    row = lax.broadcasted_iota(jnp.int32, (tq, tq), 0)
    col = lax.broadcasted_iota(jnp.int32, (tq, tq), 1)
    before = col < row

    def chunk(c, afters, accs, diagonal):
        k0 = pl.multiple_of(c * tq, tq)
        tri = tri_ref[...]
        lanes = [slice(h * HEAD_DIM, (h + 1) * HEAD_DIM) for h in range(heads)]
        zs = [lax.dot_general(q_ref[0, :, lanes[h]], k_ref[0, pl.ds(k0, tq), lanes[h]],
                              (((1,), (1,)), ((), ())), preferred_element_type=_F32) for h in range(heads)]
        csums = []
        for h in range(heads):
            z = zs[h]
            ln = -(jnp.maximum(z, 0.0) + jnp.log(1.0 + jnp.exp(-jnp.abs(z))))
            if diagonal:
                ln = jnp.where(before, ln, 0.0)
            hi = ln.astype(_BF16)
            lo = (ln - hi.astype(_F32)).astype(_BF16)
            csums.append(jnp.dot(hi, tri, preferred_element_type=_F32)
                         + jnp.dot(lo, tri, preferred_element_type=_F32))
        new_afters, new_accs = [], []
        for h in range(heads):
            w = jnp.exp(zs[h] + csums[h] + afters[h])
            if diagonal:
                w = jnp.where(before, w, 0.0)
            new_accs.append(accs[h] + jnp.dot(w.astype(_BF16), v_ref[0, pl.ds(k0, tq), lanes[h]],
                                              preferred_element_type=_F32))
            new_afters.append(afters[h] + csums[h][:, 0:1])
        return tuple(new_afters), tuple(new_accs)

    afters, accs = chunk(i, (jnp.zeros((tq, 1), _F32),) * heads, (jnp.zeros((tq, HEAD_DIM), _F32),) * heads, True)

    def keep_going(carry):
        c, afters, _ = carry
        return jnp.logical_and(c >= 0, jnp.max(functools.reduce(jnp.maximum, afters)) > EXP_UNDERFLOW)

    def body(carry):
        c, afters, accs = carry
        afters, accs = chunk(c, afters, accs, False)
        return c - 1, afters, accs

    _, _, accs = lax.while_loop(keep_going, body, (i - 1, afters, accs))
    for h in range(heads):
        o_ref[0, :, h * HEAD_DIM:(h + 1) * HEAD_DIM] = accs[h].astype(o_ref.dtype)


def _stick_attention(proj3, *, tq, heads):
    b, s, _ = proj3.shape
    width = heads * HEAD_DIM
    assert s % tq == 0 and N_HEADS_B % heads == 0 and (3 * WIDTH_A) % width == 0
    groups = N_HEADS_B // heads
    base = 3 * WIDTH_A // width
    idx = jnp.arange(tq)
    tri = (idx[:, None] >= idx[None, :]).astype(_BF16)
    whole_seq = lambda part: pl.BlockSpec((1, s, width), lambda bi, hg, i: (bi, 0, base + part * groups + hg),
                                          pipeline_mode=pl.Buffered(1))
    return pl.pallas_call(
        functools.partial(_stick_kernel, tq=tq, heads=heads),
        grid=(b, groups, s // tq),
        in_specs=[
            pl.BlockSpec((1, tq, width), lambda bi, hg, i: (bi, i, base + hg)),
            whole_seq(1),
            whole_seq(2),
            pl.BlockSpec((tq, tq), lambda bi, hg, i: (0, 0)),
        ],
        out_specs=pl.BlockSpec((1, tq, width), lambda bi, hg, i: (bi, i, hg)),
        out_shape=jax.ShapeDtypeStruct((b, s, WIDTH_B), _BF16),
        compiler_params=_params(("parallel", "parallel", "arbitrary")),
        name="stick_breaking_attention",
    )(proj3, proj3, proj3, tri)


def _merge_kernel(ya_ref, yb_ref, ga_ref, gb_ref, h_ref, wa_ref, wb_ref, wo_ref, g_ref, o_ref):
    a = jnp.dot(ya_ref[...], wa_ref[...], preferred_element_type=_F32)
    bb = jnp.dot(yb_ref[...], wb_ref[...], preferred_element_type=_F32)
    m = (ga_ref[...].astype(_F32) * a + gb_ref[...].astype(_F32) * bb).astype(_BF16)
    mix = jnp.dot(m, wo_ref[...], preferred_element_type=_F32)
    o_ref[...] = h_ref[...] + _rmsnorm_rows(mix, g_ref[...])


def _merge(ya, yb, proj, h2d, wa, wb, wo, g, *, tm):
    t, d = h2d.shape
    assert t % tm == 0 and QKV_WIDTH % d == 0
    gate_blk = QKV_WIDTH // d
    full = lambda shape: pl.BlockSpec(shape, lambda i: (0, 0), pipeline_mode=pl.Buffered(1))
    return pl.pallas_call(
        _merge_kernel,
        grid=(t // tm,),
        in_specs=[
            pl.BlockSpec((tm, A_OUT), lambda i: (i, 0)),
            pl.BlockSpec((tm, WIDTH_B), lambda i: (i, 0)),
            pl.BlockSpec((tm, d), lambda i: (i, gate_blk)),
            pl.BlockSpec((tm, d), lambda i: (i, gate_blk + 1)),
            pl.BlockSpec((tm, d), lambda i: (i, 0)),
            full(wa.shape), full(wb.shape), full(wo.shape), full(g.shape),
        ],
        out_specs=pl.BlockSpec((tm, d), lambda i: (i, 0)),
        out_shape=jax.ShapeDtypeStruct((t, d), _F32),
        compiler_params=_params(("parallel",)),
        name="gated_merge",
    )(ya, yb, proj, proj, h2d, wa, wb, wo, g)


def _ffn_kernel(h_ref, hp_ref, gpre_ref, wg_ref, wv_ref, cwg_ref, cwv_ref, cbg_ref, cbv_ref, wd_ref,
                gpost_ref, o_ref, u_ref, *, tiles_per_seq, chunk):
    i = pl.program_id(0)
    j = pl.program_id(1)
    tm = h_ref.shape[0]
    n_chunks = tm // chunk

    @pl.when(j == 0)
    def _():
        halo = _rmsnorm_rows(hp_ref[...], gpre_ref[...])
        at_sequence_start = (i % tiles_per_seq) == 0
        u_ref[0:CONV_HALO, :] = jnp.where(at_sequence_start, 0.0, halo).astype(u_ref.dtype)
        o_ref[...] = jnp.zeros(o_ref.shape, o_ref.dtype)

        def body(c, carry):
            r0 = pl.multiple_of(c * chunk, chunk)
            x = h_ref[pl.ds(r0, chunk), :]
            u_ref[pl.ds(CONV_HALO + r0, chunk), :] = _rmsnorm_rows(x, gpre_ref[...]).astype(u_ref.dtype)
            return carry
        lax.fori_loop(0, n_chunks, body, 0)

    u = u_ref[...]

    def conv_branch(w_ref, cw_ref, cb_ref):
        x = jnp.dot(u, w_ref[...], preferred_element_type=_F32)
        cw = cw_ref[...]
        y = cb_ref[...] + cw[0:1, :] * pltpu.roll(x, 2, 0)
        y = y + cw[1:2, :] * pltpu.roll(x, 1, 0)
        y = y + cw[2:3, :] * x
        return y[CONV_HALO:, :]

    gate = conv_branch(wg_ref, cwg_ref, cbg_ref)
    value = conv_branch(wv_ref, cwv_ref, cbv_ref)
    act = (jax.nn.gelu(gate, approximate=True) * value).astype(_BF16)
    o_ref[...] += jnp.dot(act, wd_ref[...], preferred_element_type=_F32)

    @pl.when(j == pl.num_programs(1) - 1)
    def _():
        def body(c, carry):
            r0 = pl.multiple_of(c * chunk, chunk)
            f = o_ref[pl.ds(r0, chunk), :]
            o_ref[pl.ds(r0, chunk), :] = h_ref[pl.ds(r0, chunk), :] + _rmsnorm_rows(f, gpost_ref[...])
            return carry
        lax.fori_loop(0, n_chunks, body, 0)


def _ffn(h2d, gpre, w_up, conv_w, conv_b, w_down, gpost, *, tm, tf, seq):
    t, d = h2d.shape
    d_ff = w_down.shape[0]
    assert t % tm == 0 and seq % tm == 0 and d_ff % tf == 0 and tm % CONV_HALO == 0
    nj = d_ff // tf
    halo_blocks = tm // CONV_HALO
    return pl.pallas_call(
        functools.partial(_ffn_kernel, tiles_per_seq=seq // tm, chunk=min(tm, 128)),
        grid=(t // tm, nj),
        in_specs=[
            pl.BlockSpec((tm, d), lambda i, j: (i, 0)),
            pl.BlockSpec((CONV_HALO, d), lambda i, j: (jnp.maximum(i * halo_blocks - 1, 0), 0)),
            pl.BlockSpec((1, d), lambda i, j: (0, 0)),
            pl.BlockSpec((d, tf), lambda i, j: (0, j)),
            pl.BlockSpec((d, tf), lambda i, j: (0, nj + j)),
            pl.BlockSpec((CONV_WIDTH, tf), lambda i, j: (0, j)),
            pl.BlockSpec((CONV_WIDTH, tf), lambda i, j: (0, nj + j)),
            pl.BlockSpec((1, tf), lambda i, j: (0, j)),
            pl.BlockSpec((1, tf), lambda i, j: (0, nj + j)),
            pl.BlockSpec((tf, d), lambda i, j: (j, 0)),
            pl.BlockSpec((1, d), lambda i, j: (0, 0)),
        ],
        out_specs=pl.BlockSpec((tm, d), lambda i, j: (i, 0)),
        out_shape=jax.ShapeDtypeStruct((t, d), _F32),
        scratch_shapes=[pltpu.VMEM((CONV_HALO + tm, d), _BF16)],
        compiler_params=_params(("parallel", "arbitrary")),
        name="conv_ffn",
    )(h2d, h2d, gpre, w_up, w_up, conv_w, conv_w, conv_b, conv_b, w_down, gpost)


def _ple_kernel(h_ref, p_ref, wg_ref, wi_ref, o_ref):
    h = h_ref[...]
    gate = jax.nn.sigmoid(jnp.dot(h.astype(_BF16), wg_ref[...], preferred_element_type=_F32))
    e = jnp.dot(p_ref[...].astype(_BF16), wi_ref[...], preferred_element_type=_F32)
    o_ref[...] = h + gate * e


def _ple(h2d, p2d, w_gate, w_in, *, tm):
    t, d = h2d.shape
    pd = p2d.shape[1]
    assert t % tm == 0
    return pl.pallas_call(
        _ple_kernel,
        grid=(t // tm,),
        in_specs=[
            pl.BlockSpec((tm, d), lambda i: (i, 0)),
            pl.BlockSpec((tm, pd), lambda i: (i, 0)),
            pl.BlockSpec((d, d), lambda i: (0, 0)),
            pl.BlockSpec((pd, d), lambda i: (0, 0)),
        ],
        out_specs=pl.BlockSpec((tm, d), lambda i: (i, 0)),
        out_shape=jax.ShapeDtypeStruct((t, d), _F32),
        compiler_params=_params(("parallel",)),
        name="layer_embedding",
    )(h2d, p2d, w_gate, w_in)


def _tiles(t, seq, d, d_in, d_ff):
    def largest(n, cands):
        return next(c for c in cands if n % c == 0)
    return dict(
        proj_tm=largest(t, (1024, 512, 256, 128)),
        proj_tn=largest(math.gcd(d_in, QKV_WIDTH), (2048, 1024, 512, 256, 128)),
        stick_tq=largest(seq, (256, 128)),
        merge_tm=largest(t, (512, 256, 128)),
        ffn_tm=largest(seq, (1024, 512, 256, 128)),
        ffn_tf=largest(d_ff, (512, 256, 128)),
        ple_tm=largest(t, (512, 256, 128)),
    )


def kernel(x, p, g_mix_pre, w_in, w_branch_a, w_branch_b, w_out, g_mix_post, g_ffn_pre, w_up, conv_w,
           conv_b, w_down, g_ffn_post, w_ple_in, w_ple_gate):
    b, s, d = x.shape
    depth = w_in.shape[0]
    d_in = w_in.shape[2]
    d_ff = w_down.shape[1]
    t = b * s
    assert d_in == QKV_WIDTH + 2 * d
    tiles = _tiles(t, s, d, d_in, d_ff)

    q_scale = 1.0 / math.sqrt(HEAD_DIM)
    col = jnp.arange(d_in)
    is_q = (col < WIDTH_A) | ((col >= 3 * WIDTH_A) & (col < 3 * WIDTH_A + WIDTH_B))
    colscale = jnp.where(is_q, q_scale, 1.0).astype(_F32)[None, :]

    h = x.reshape(t, d)
    for i in range(depth):
        row = lambda v: v[i][None, :]
        proj = _in_proj(h, row(g_mix_pre), colscale, w_in[i].astype(_BF16),
                        tm=tiles["proj_tm"], tn=tiles["proj_tn"])
        proj3 = proj.reshape(b, s, d_in)
        ya = _dilated_attention(proj3).reshape(t, A_OUT)
        yb = _stick_attention(proj3, tq=tiles["stick_tq"], heads=N_HEADS_B).reshape(t, WIDTH_B)
        h = _merge(ya, yb, proj, h, w_branch_a[i].astype(_BF16), w_branch_b[i].astype(_BF16),
                   w_out[i].astype(_BF16), row(g_mix_post), tm=tiles["merge_tm"])
        h = _ffn(h, row(g_ffn_pre), w_up[i].astype(_BF16), conv_w[i], conv_b[i][None, :],
                 w_down[i].astype(_BF16), row(g_ffn_post), tm=tiles["ffn_tm"], tf=tiles["ffn_tf"], seq=s)
        h = _ple(h, p[i].reshape(t, -1), w_ple_gate[i].astype(_BF16), w_ple_in[i].astype(_BF16),
                 tm=tiles["ple_tm"])
    return h.reshape(b, s, d)
```

```python
import functools
import math

import jax
import jax.numpy as jnp
from jax import lax
from jax.experimental import pallas as pl
from jax.experimental.pallas import tpu as pltpu

HEAD_DIM = 128
DIL_GROUPS = ((128, 1), (512, 4), (2048, 16))
HEADS_PER_GROUP = 4
N_HEADS_A = HEADS_PER_GROUP * len(DIL_GROUPS)
N_HEADS_B = 4
WIDTH_A = N_HEADS_A * HEAD_DIM
WIDTH_B = N_HEADS_B * HEAD_DIM
A_OUT = HEADS_PER_GROUP * HEAD_DIM
QKV_WIDTH = 3 * WIDTH_A + 3 * WIDTH_B
CONV_WIDTH = 3
RMS_EPS = 1e-6
ALIBI_MAX = 8.0
MASKED = -1e30
EXP_UNDERFLOW = -110.0
QBLK = 128
SPAN = 2048
DIL_BLOCKS_PER_TRIP = 4
CONV_HALO = 16
VMEM_LIMIT = 56 * 1024 * 1024

_F32 = jnp.float32
_BF16 = jnp.bfloat16


def _rmsnorm_rows(x, g):
    ms = jnp.mean(x * x, axis=-1, keepdims=True)
    return x * lax.rsqrt(ms + RMS_EPS) * g


def _params(sem):
    return pltpu.CompilerParams(dimension_semantics=sem, vmem_limit_bytes=VMEM_LIMIT)


def _layer_rows(layer, width):
    return pl.BlockSpec((None, 1, width), lambda *_: (layer, 0, 0))


def _in_proj_kernel(x_ref, g_ref, cs_ref, w_ref, o_ref, u_ref, *, gate_tile0, chunk):
    j = pl.program_id(1)
    tm = x_ref.shape[0]

    @pl.when(j == 0)
    def _():
        def body(c, carry):
            r0 = pl.multiple_of(c * chunk, chunk)
            x = x_ref[pl.ds(r0, chunk), :]
            u_ref[pl.ds(r0, chunk), :] = _rmsnorm_rows(x, g_ref[...]).astype(u_ref.dtype)
            return carry
        lax.fori_loop(0, tm // chunk, body, 0)

    acc = jnp.dot(u_ref[...], w_ref[...], preferred_element_type=_F32)

    o_ref[...] = jnp.where(j >= gate_tile0, jax.nn.sigmoid(acc), acc * cs_ref[...]).astype(o_ref.dtype)


def _in_proj(h2d, g, colscale, w, layer, *, tm, tn):
    t, d = h2d.shape
    d_in = w.shape[2]
    assert t % tm == 0 and d_in % tn == 0 and QKV_WIDTH % tn == 0
    return pl.pallas_call(
        functools.partial(_in_proj_kernel, gate_tile0=QKV_WIDTH // tn, chunk=min(tm, 128)),
        grid=(t // tm, d_in // tn),
        in_specs=[
            pl.BlockSpec((tm, d), lambda i, j: (i, 0)),
            _layer_rows(layer, d),
            pl.BlockSpec((1, tn), lambda i, j: (0, j)),
            pl.BlockSpec((None, d, tn), lambda i, j: (layer, 0, j)),
        ],
        out_specs=pl.BlockSpec((tm, tn), lambda i, j: (i, j)),
        out_shape=jax.ShapeDtypeStruct((t, d_in), _BF16),
        scratch_shapes=[pltpu.VMEM((tm, d), _BF16)],
        compiler_params=_params(("parallel", "arbitrary")),
        name="in_proj",
    )(h2d, g, colscale, w)


def _dilated_kernel(q0, q1, q2, k0, k1, k2, v0, v1, v2, b0, b1, b2, o_ref,
                    kb0, kb1, kb2, vb0, vb1, vb2):
    i = pl.program_id(2)
    q_refs, bias_refs = (q0, q1, q2), (b0, b1, b2)
    k_bufs, v_bufs = (kb0, kb1, kb2), (vb0, vb1, vb2)

    for buf, new in zip(k_bufs + v_bufs, (k0, k1, k2, v0, v1, v2)):
        @pl.when(i == 0)
        def _(buf=buf):
            buf[0:SPAN, :] = jnp.zeros((SPAN, HEAD_DIM), buf.dtype)

        @pl.when(i > 0)
        def _(buf=buf):
            buf[0:SPAN, :] = buf[SPAN:2 * SPAN, :]

        buf[SPAN:2 * SPAN, :] = new[0]

    first_valid_row = jnp.where(i == 0, SPAN, 0)

    def masked_scores(o):
        scores, starts = [], []
        for g, (window, _) in enumerate(DIL_GROUPS):
            n = window + QBLK
            start = pl.multiple_of(o + SPAN - window, QBLK)
            q = q_refs[g][0, pl.ds(o, QBLK), :]
            kk = k_bufs[g][pl.ds(start, n), :]
            sc = lax.dot_general(q, kk, (((1,), (1,)), ((), ())), preferred_element_type=_F32)
            sc = sc + bias_refs[g][0]
            kj = lax.broadcasted_iota(jnp.int32, (QBLK, n), 1)
            scores.append(jnp.where(kj >= first_valid_row - start, sc, MASKED))
            starts.append(start)
        return scores, starts

    def joint_softmax_weights(scores):
        m = functools.reduce(jnp.maximum, [jnp.max(sc, axis=-1, keepdims=True) for sc in scores])
        ps = [jnp.exp(sc - m) for sc in scores]
        denom = functools.reduce(jnp.add, [jnp.sum(p, axis=-1, keepdims=True) for p in ps])
        return [p.astype(_BF16) for p in ps], denom

    def weighted_values(ps, starts):
        acc = jnp.zeros((QBLK, HEAD_DIM), _F32)
        for g, (window, _) in enumerate(DIL_GROUPS):
            vv = v_bufs[g][pl.ds(starts[g], window + QBLK), :]
            acc = acc + jnp.dot(ps[g], vv, preferred_element_type=_F32)
        return acc

    def body(s, carry):
        offs = [pl.multiple_of((s * DIL_BLOCKS_PER_TRIP + t) * QBLK, QBLK) for t in range(DIL_BLOCKS_PER_TRIP)]
        staged = [masked_scores(o) for o in offs]
        weights = [joint_softmax_weights(scores) for scores, _ in staged]
        for o, (_, starts), (ps, denom) in zip(offs, staged, weights):
            o_ref[0, pl.ds(o, QBLK), :] = (weighted_values(ps, starts) / denom).astype(o_ref.dtype)
        return carry

    lax.fori_loop(0, SPAN // (QBLK * DIL_BLOCKS_PER_TRIP), body, 0)


def _dilated_bias_tables():
    slopes = jnp.exp2(-ALIBI_MAX * jnp.arange(1, N_HEADS_A + 1, dtype=_F32) / N_HEADS_A)
    tables = []
    for g, (window, dilation) in enumerate(DIL_GROUPS):
        qi = jnp.arange(QBLK)[:, None]
        kj = jnp.arange(window + QBLK)[None, :]
        dist = qi + window - kj
        valid = (dist >= 0) & (dist <= window) & (dist % dilation == 0)
        sl = slopes[g * HEADS_PER_GROUP:(g + 1) * HEADS_PER_GROUP]
        bias = -sl[:, None, None] * dist.astype(_F32)[None]
        tables.append(jnp.where(valid[None], bias, MASKED))
    return tables


def _dilated_attention(proj3):
    b, s, _ = proj3.shape
    assert s % SPAN == 0
    n_heads = N_HEADS_A

    def col_spec(base):
        return [pl.BlockSpec((1, SPAN, HEAD_DIM), (lambda bi, hh, i, c=base + g * HEADS_PER_GROUP: (bi, i, c + hh)))
                for g in range(len(DIL_GROUPS))]

    tables = _dilated_bias_tables()
    bias_specs = [pl.BlockSpec((1, QBLK, w + QBLK), lambda bi, hh, i: (hh, 0, 0)) for w, _ in DIL_GROUPS]
    return pl.pallas_call(
        _dilated_kernel,
        grid=(b, HEADS_PER_GROUP, s // SPAN),
        in_specs=col_spec(0) + col_spec(n_heads) + col_spec(2 * n_heads) + bias_specs,
        out_specs=pl.BlockSpec((1, SPAN, HEAD_DIM), lambda bi, hh, i: (bi, i, hh)),
        out_shape=jax.ShapeDtypeStruct((b, s, A_OUT), _BF16),
        scratch_shapes=[pltpu.VMEM((2 * SPAN, HEAD_DIM), _BF16) for _ in range(6)],
        compiler_params=_params(("arbitrary", "arbitrary", "arbitrary")),
        name="dilated_attention",
    )(*([proj3] * 9), *tables)


def _stick_kernel(q_ref, k_ref, v_ref, tri_ref, o_ref, *, tq, heads):
    i = pl.program_id(2)
    row = lax.broadcasted_iota(jnp.int32, (tq, tq), 0)
    col = lax.broadcasted_iota(jnp.int32, (tq, tq), 1)
    before = col < row

    def chunk(c, afters, accs, diagonal):
        k0 = pl.multiple_of(c * tq, tq)
        tri = tri_ref[...]
        lanes = [slice(h * HEAD_DIM, (h + 1) * HEAD_DIM) for h in range(heads)]
        zs = [lax.dot_general(q_ref[0, :, lanes[h]], k_ref[0, pl.ds(k0, tq), lanes[h]],
                              (((1,), (1,)), ((), ())), preferred_element_type=_F32) for h in range(heads)]
        csums = []
        for h in range(heads):
            z = zs[h]
            ln = -(jnp.maximum(z, 0.0) + jnp.log(1.0 + jnp.exp(-jnp.abs(z))))
            if diagonal:
                ln = jnp.where(before, ln, 0.0)
            hi = ln.astype(_BF16)
            lo = (ln - hi.astype(_F32)).astype(_BF16)
            csums.append(jnp.dot(hi, tri, preferred_element_type=_F32)
                         + jnp.dot(lo, tri, preferred_element_type=_F32))
        new_afters, new_accs = [], []
        for h in range(heads):
            w = jnp.exp(zs[h] + csums[h] + afters[h])
            if diagonal:
                w = jnp.where(before, w, 0.0)
            new_accs.append(accs[h] + jnp.dot(w.astype(_BF16), v_ref[0, pl.ds(k0, tq), lanes[h]],
                                              preferred_element_type=_F32))
            new_afters.append(afters[h] + csums[h][:, 0:1])
        return tuple(new_afters), tuple(new_accs)

    afters, accs = chunk(i, (jnp.zeros((tq, 1), _F32),) * heads, (jnp.zeros((tq, HEAD_DIM), _F32),) * heads, True)

    def keep_going(carry):
        c, afters, _ = carry
        return jnp.logical_and(c >= 0, jnp.max(functools.reduce(jnp.maximum, afters)) > EXP_UNDERFLOW)

    def body(carry):
        c, afters, accs = carry
        afters, accs = chunk(c, afters, accs, False)
        return c - 1, afters, accs

    _, _, accs = lax.while_loop(keep_going, body, (i - 1, afters, accs))
    for h in range(heads):
        o_ref[0, :, h * HEAD_DIM:(h + 1) * HEAD_DIM] = accs[h].astype(o_ref.dtype)


def _stick_attention(proj3, *, tq, heads):
    b, s, _ = proj3.shape
    width = heads * HEAD_DIM
    assert s % tq == 0 and N_HEADS_B % heads == 0 and (3 * WIDTH_A) % width == 0
    groups = N_HEADS_B // heads
    base = 3 * WIDTH_A // width
    idx = jnp.arange(tq)
    tri = (idx[:, None] >= idx[None, :]).astype(_BF16)
    whole_seq = lambda part: pl.BlockSpec((1, s, width), lambda bi, hg, i: (bi, 0, base + part * groups + hg),
                                          pipeline_mode=pl.Buffered(1))
    return pl.pallas_call(
        functools.partial(_stick_kernel, tq=tq, heads=heads),
        grid=(b, groups, s // tq),
        in_specs=[
            pl.BlockSpec((1, tq, width), lambda bi, hg, i: (bi, i, base + hg)),
            whole_seq(1),
            whole_seq(2),
            pl.BlockSpec((tq, tq), lambda bi, hg, i: (0, 0)),
        ],
        out_specs=pl.BlockSpec((1, tq, width), lambda bi, hg, i: (bi, i, hg)),
        out_shape=jax.ShapeDtypeStruct((b, s, WIDTH_B), _BF16),
        compiler_params=_params(("parallel", "parallel", "arbitrary")),
        name="stick_breaking_attention",
    )(proj3, proj3, proj3, tri)


def _merge_kernel(ya_ref, yb_ref, ga_ref, gb_ref, h_ref, wa_ref, wb_ref, wo_ref, g_ref, o_ref):
    a = jnp.dot(ya_ref[...], wa_ref[...], preferred_element_type=_F32)
    bb = jnp.dot(yb_ref[...], wb_ref[...], preferred_element_type=_F32)
    m = (ga_ref[...].astype(_F32) * a + gb_ref[...].astype(_F32) * bb).astype(_BF16)
    mix = jnp.dot(m, wo_ref[...], preferred_element_type=_F32)
    o_ref[...] = h_ref[...] + _rmsnorm_rows(mix, g_ref[...])


def _merge(ya, yb, proj, h2d, wa, wb, wo, g, layer, *, tm):
    t, d = h2d.shape
    assert t % tm == 0 and QKV_WIDTH % d == 0
    gate_blk = QKV_WIDTH // d
    resident = lambda w: pl.BlockSpec((None,) + w.shape[1:], lambda i: (layer, 0, 0), pipeline_mode=pl.Buffered(1))
    return pl.pallas_call(
        _merge_kernel,
        grid=(t // tm,),
        in_specs=[
            pl.BlockSpec((tm, A_OUT), lambda i: (i, 0)),
            pl.BlockSpec((tm, WIDTH_B), lambda i: (i, 0)),
            pl.BlockSpec((tm, d), lambda i: (i, gate_blk)),
            pl.BlockSpec((tm, d), lambda i: (i, gate_blk + 1)),
            pl.BlockSpec((tm, d), lambda i: (i, 0)),
            resident(wa), resident(wb), resident(wo), _layer_rows(layer, d),
        ],
        out_specs=pl.BlockSpec((tm, d), lambda i: (i, 0)),
        out_shape=jax.ShapeDtypeStruct((t, d), _F32),
        compiler_params=_params(("parallel",)),
        name="gated_merge",
    )(ya, yb, proj, proj, h2d, wa, wb, wo, g)


def _ffn_kernel(h_ref, hp_ref, gpre_ref, wg_ref, wv_ref, cwg_ref, cwv_ref, cbg_ref, cbv_ref, wd_ref,
                gpost_ref, o_ref, u_ref, *, tiles_per_seq, chunk):
    i = pl.program_id(0)
    j = pl.program_id(1)
    tm = h_ref.shape[0]
    n_chunks = tm // chunk

    @pl.when(j == 0)
    def _():
        halo = _rmsnorm_rows(hp_ref[...], gpre_ref[...])
        at_sequence_start = (i % tiles_per_seq) == 0
        u_ref[0:CONV_HALO, :] = jnp.where(at_sequence_start, 0.0, halo).astype(u_ref.dtype)
        o_ref[...] = jnp.zeros(o_ref.shape, o_ref.dtype)

        def body(c, carry):
            r0 = pl.multiple_of(c * chunk, chunk)
            x = h_ref[pl.ds(r0, chunk), :]
            u_ref[pl.ds(CONV_HALO + r0, chunk), :] = _rmsnorm_rows(x, gpre_ref[...]).astype(u_ref.dtype)
            return carry
        lax.fori_loop(0, n_chunks, body, 0)

    u = u_ref[...]

    def conv_branch(w_ref, cw_ref, cb_ref):
        x = jnp.dot(u, w_ref[...], preferred_element_type=_F32)
        cw = cw_ref[...]
        y = cb_ref[...] + cw[0:1, :] * pltpu.roll(x, 2, 0)
        y = y + cw[1:2, :] * pltpu.roll(x, 1, 0)
        y = y + cw[2:3, :] * x
        return y[CONV_HALO:, :]

    gate = conv_branch(wg_ref, cwg_ref, cbg_ref)
    value = conv_branch(wv_ref, cwv_ref, cbv_ref)
    act = (jax.nn.gelu(gate, approximate=True) * value).astype(_BF16)
    o_ref[...] += jnp.dot(act, wd_ref[...], preferred_element_type=_F32)

    @pl.when(j == pl.num_programs(1) - 1)
    def _():
        def body(c, carry):
            r0 = pl.multiple_of(c * chunk, chunk)
            f = o_ref[pl.ds(r0, chunk), :]
            o_ref[pl.ds(r0, chunk), :] = h_ref[pl.ds(r0, chunk), :] + _rmsnorm_rows(f, gpost_ref[...])
            return carry
        lax.fori_loop(0, n_chunks, body, 0)


def _ffn(h2d, gpre, w_up, conv_w, conv_b, w_down, gpost, layer, *, tm, tf, seq):
    t, d = h2d.shape
    d_ff = w_down.shape[1]
    assert t % tm == 0 and seq % tm == 0 and d_ff % tf == 0 and tm % CONV_HALO == 0
    nj = d_ff // tf
    halo_blocks = tm // CONV_HALO
    return pl.pallas_call(
        functools.partial(_ffn_kernel, tiles_per_seq=seq // tm, chunk=min(tm, 128)),
        grid=(t // tm, nj),
        in_specs=[
            pl.BlockSpec((tm, d), lambda i, j: (i, 0)),
            pl.BlockSpec((CONV_HALO, d), lambda i, j: (jnp.maximum(i * halo_blocks - 1, 0), 0)),
            _layer_rows(layer, d),
            pl.BlockSpec((None, d, tf), lambda i, j: (layer, 0, j)),
            pl.BlockSpec((None, d, tf), lambda i, j: (layer, 0, nj + j)),
            pl.BlockSpec((None, CONV_WIDTH, tf), lambda i, j: (layer, 0, j)),
            pl.BlockSpec((None, CONV_WIDTH, tf), lambda i, j: (layer, 0, nj + j)),
            pl.BlockSpec((None, 1, tf), lambda i, j: (layer, 0, j)),
            pl.BlockSpec((None, 1, tf), lambda i, j: (layer, 0, nj + j)),
            pl.BlockSpec((None, tf, d), lambda i, j: (layer, j, 0)),
            _layer_rows(layer, d),
        ],
        out_specs=pl.BlockSpec((tm, d), lambda i, j: (i, 0)),
        out_shape=jax.ShapeDtypeStruct((t, d), _F32),
        scratch_shapes=[pltpu.VMEM((CONV_HALO + tm, d), _BF16)],
        compiler_params=_params(("parallel", "arbitrary")),
        name="conv_ffn",
    )(h2d, h2d, gpre, w_up, w_up, conv_w, conv_w, conv_b, conv_b, w_down, gpost)


def _ple_kernel(h_ref, p_ref, wg_ref, wi_ref, o_ref):
    h = h_ref[...]
    gate = jax.nn.sigmoid(jnp.dot(h.astype(_BF16), wg_ref[...], preferred_element_type=_F32))
    e = jnp.dot(p_ref[...].astype(_BF16), wi_ref[...], preferred_element_type=_F32)
    o_ref[...] = h + gate * e


def _ple(h2d, p3d, w_gate, w_in, layer, *, tm):
    t, d = h2d.shape
    pd = p3d.shape[2]
    assert t % tm == 0
    resident = lambda w: pl.BlockSpec((None,) + w.shape[1:], lambda i: (layer, 0, 0), pipeline_mode=pl.Buffered(1))
    return pl.pallas_call(
        _ple_kernel,
        grid=(t // tm,),
        in_specs=[
            pl.BlockSpec((tm, d), lambda i: (i, 0)),
            pl.BlockSpec((None, tm, pd), lambda i: (layer, i, 0)),
            resident(w_gate),
            resident(w_in),
        ],
        out_specs=pl.BlockSpec((tm, d), lambda i: (i, 0)),
        out_shape=jax.ShapeDtypeStruct((t, d), _F32),
        compiler_params=_params(("parallel",)),
        name="layer_embedding",
    )(h2d, p3d, w_gate, w_in)


def _tiles(t, seq, d, d_in, d_ff):
    def largest(n, cands):
        return next(c for c in cands if n % c == 0)
    return dict(
        proj_tm=largest(t, (1024, 512, 256, 128)),
        proj_tn=largest(math.gcd(d_in, QKV_WIDTH), (2048, 1024, 512, 256, 128)),
        stick_tq=largest(seq, (256, 128)),
        merge_tm=largest(t, (512, 256, 128)),
        ffn_tm=largest(seq, (1024, 512, 256, 128)),
        ffn_tf=largest(d_ff, (512, 256, 128)),
        ple_tm=largest(t, (1024, 512, 256, 128)),
    )


def kernel(x, p, g_mix_pre, w_in, w_branch_a, w_branch_b, w_out, g_mix_post, g_ffn_pre, w_up, conv_w,
           conv_b, w_down, g_ffn_post, w_ple_in, w_ple_gate):
    b, s, d = x.shape
    depth = w_in.shape[0]
    d_in = w_in.shape[2]
    d_ff = w_down.shape[1]
    t = b * s
    assert d_in == QKV_WIDTH + 2 * d
    tiles = _tiles(t, s, d, d_in, d_ff)

    q_scale = 1.0 / math.sqrt(HEAD_DIM)
    col = jnp.arange(d_in)
    is_q = (col < WIDTH_A) | ((col >= 3 * WIDTH_A) & (col < 3 * WIDTH_A + WIDTH_B))
    colscale = jnp.where(is_q, q_scale, 1.0).astype(_F32)[None, :]

    bf16 = lambda w: w.astype(_BF16)
    rows = lambda v: v[:, None, :]
    w_in, w_branch_a, w_branch_b, w_out = bf16(w_in), bf16(w_branch_a), bf16(w_branch_b), bf16(w_out)
    w_up, w_down, w_ple_in, w_ple_gate = bf16(w_up), bf16(w_down), bf16(w_ple_in), bf16(w_ple_gate)
    p3d = p.reshape(depth, t, p.shape[-1])

    h = x.reshape(t, d)
    for i in range(depth):
        proj = _in_proj(h, rows(g_mix_pre), colscale, w_in, i, tm=tiles["proj_tm"], tn=tiles["proj_tn"])
        proj3 = proj.reshape(b, s, d_in)
        ya = _dilated_attention(proj3).reshape(t, A_OUT)
        yb = _stick_attention(proj3, tq=tiles["stick_tq"], heads=N_HEADS_B).reshape(t, WIDTH_B)
        h = _merge(ya, yb, proj, h, w_branch_a, w_branch_b, w_out, rows(g_mix_post), i, tm=tiles["merge_tm"])
        h = _ffn(h, rows(g_ffn_pre), w_up, conv_w, rows(conv_b), w_down, rows(g_ffn_post), i,
                 tm=tiles["ffn_tm"], tf=tiles["ffn_tf"], seq=s)
        h = _ple(h, p3d, w_ple_gate, w_ple_in, i, tm=tiles["ple_tm"])
    return h.reshape(b, s, d)
```

```python
import functools
import math

import jax
import jax.numpy as jnp
from jax import lax
from jax.experimental import pallas as pl
from jax.experimental.pallas import tpu as pltpu

HEAD_DIM = 128
DIL_GROUPS = ((128, 1), (512, 4), (2048, 16))
HEADS_PER_GROUP = 4
N_HEADS_A = HEADS_PER_GROUP * len(DIL_GROUPS)
N_HEADS_B = 4
WIDTH_A = N_HEADS_A * HEAD_DIM
WIDTH_B = N_HEADS_B * HEAD_DIM
A_OUT = HEADS_PER_GROUP * HEAD_DIM
QKV_WIDTH = 3 * WIDTH_A + 3 * WIDTH_B
CONV_WIDTH = 3
RMS_EPS = 1e-6
ALIBI_MAX = 8.0
MASKED = -1e30
EXP_UNDERFLOW = -110.0
QBLK = 128
SPAN = 2048
DIL_BLOCKS_PER_TRIP = 4
CONV_HALO = 16
VMEM_LIMIT = 56 * 1024 * 1024

_F32 = jnp.float32
_BF16 = jnp.bfloat16


def _rmsnorm_rows(x, g):
    ms = jnp.mean(x * x, axis=-1, keepdims=True)
    return x * lax.rsqrt(ms + RMS_EPS) * g


def _params(sem):
    return pltpu.CompilerParams(dimension_semantics=sem, vmem_limit_bytes=VMEM_LIMIT)


def _layer_rows(layer, width):
    return pl.BlockSpec((None, 1, width), lambda *_: (layer, 0, 0))


def _in_proj_kernel(x_ref, g_ref, cs_ref, w_ref, o_ref, u_ref, *, gate_tile0, chunk):
    j = pl.program_id(1)
    tm = x_ref.shape[0]

    @pl.when(j == 0)
    def _():
        def body(c, carry):
            r0 = pl.multiple_of(c * chunk, chunk)
            x = x_ref[pl.ds(r0, chunk), :]
            u_ref[pl.ds(r0, chunk), :] = _rmsnorm_rows(x, g_ref[...]).astype(u_ref.dtype)
            return carry
        lax.fori_loop(0, tm // chunk, body, 0)

    acc = jnp.dot(u_ref[...], w_ref[...], preferred_element_type=_F32)

    o_ref[...] = jnp.where(j >= gate_tile0, jax.nn.sigmoid(acc), acc * cs_ref[...]).astype(o_ref.dtype)


def _in_proj(h2d, g, colscale, w, layer, *, tm, tn):
    t, d = h2d.shape
    d_in = w.shape[2]
    assert t % tm == 0 and d_in % tn == 0 and QKV_WIDTH % tn == 0
    return pl.pallas_call(
        functools.partial(_in_proj_kernel, gate_tile0=QKV_WIDTH // tn, chunk=min(tm, 128)),
        grid=(t // tm, d_in // tn),
        in_specs=[
            pl.BlockSpec((tm, d), lambda i, j: (i, 0)),
            _layer_rows(layer, d),
            pl.BlockSpec((1, tn), lambda i, j: (0, j)),
            pl.BlockSpec((None, d, tn), lambda i, j: (layer, 0, j)),
        ],
        out_specs=pl.BlockSpec((tm, tn), lambda i, j: (i, j)),
        out_shape=jax.ShapeDtypeStruct((t, d_in), _BF16),
        scratch_shapes=[pltpu.VMEM((tm, d), _BF16)],
        compiler_params=_params(("parallel", "arbitrary")),
        name="in_proj",
    )(h2d, g, colscale, w)


def _dilated_kernel(q0, q1, q2, k0, k1, k2, v0, v1, v2, b0, b1, b2, o_ref,
                    kb0, kb1, kb2, vb0, vb1, vb2):
    i = pl.program_id(2)
    q_refs, bias_refs = (q0, q1, q2), (b0, b1, b2)
    k_bufs, v_bufs = (kb0, kb1, kb2), (vb0, vb1, vb2)

    for buf, new in zip(k_bufs + v_bufs, (k0, k1, k2, v0, v1, v2)):
        @pl.when(i == 0)
        def _(buf=buf):
            buf[0:SPAN, :] = jnp.zeros((SPAN, HEAD_DIM), buf.dtype)

        @pl.when(i > 0)
        def _(buf=buf):
            buf[0:SPAN, :] = buf[SPAN:2 * SPAN, :]

        buf[SPAN:2 * SPAN, :] = new[0]

    first_valid_row = jnp.where(i == 0, SPAN, 0)

    def masked_scores(o):
        scores, starts = [], []
        for g, (window, _) in enumerate(DIL_GROUPS):
            n = window + QBLK
            start = pl.multiple_of(o + SPAN - window, QBLK)
            q = q_refs[g][0, pl.ds(o, QBLK), :]
            kk = k_bufs[g][pl.ds(start, n), :]
            sc = lax.dot_general(q, kk, (((1,), (1,)), ((), ())), preferred_element_type=_F32)
            sc = sc + bias_refs[g][0]
            kj = lax.broadcasted_iota(jnp.int32, (QBLK, n), 1)
            scores.append(jnp.where(kj >= first_valid_row - start, sc, MASKED))
            starts.append(start)
        return scores, starts

    def joint_softmax_weights(scores):
        m = functools.reduce(jnp.maximum, [jnp.max(sc, axis=-1, keepdims=True) for sc in scores])
        ps = [jnp.exp(sc - m) for sc in scores]
        denom = functools.reduce(jnp.add, [jnp.sum(p, axis=-1, keepdims=True) for p in ps])
        return [p.astype(_BF16) for p in ps], denom

    def weighted_values(ps, starts):
        acc = jnp.zeros((QBLK, HEAD_DIM), _F32)
        for g, (window, _) in enumerate(DIL_GROUPS):
            vv = v_bufs[g][pl.ds(starts[g], window + QBLK), :]
            acc = acc + jnp.dot(ps[g], vv, preferred_element_type=_F32)
        return acc

    def body(s, carry):
        offs = [pl.multiple_of((s * DIL_BLOCKS_PER_TRIP + t) * QBLK, QBLK) for t in range(DIL_BLOCKS_PER_TRIP)]
        staged = [masked_scores(o) for o in offs]
        weights = [joint_softmax_weights(scores) for scores, _ in staged]
        for o, (_, starts), (ps, denom) in zip(offs, staged, weights):
            o_ref[0, pl.ds(o, QBLK), :] = (weighted_values(ps, starts) / denom).astype(o_ref.dtype)
        return carry

    lax.fori_loop(0, SPAN // (QBLK * DIL_BLOCKS_PER_TRIP), body, 0)


def _dilated_bias_tables():
    slopes = jnp.exp2(-ALIBI_MAX * jnp.arange(1, N_HEADS_A + 1, dtype=_F32) / N_HEADS_A)
    tables = []
    for g, (window, dilation) in enumerate(DIL_GROUPS):
        qi = jnp.arange(QBLK)[:, None]
        kj = jnp.arange(window + QBLK)[None, :]
        dist = qi + window - kj
        valid = (dist >= 0) & (dist <= window) & (dist % dilation == 0)
        sl = slopes[g * HEADS_PER_GROUP:(g + 1) * HEADS_PER_GROUP]
        bias = -sl[:, None, None] * dist.astype(_F32)[None]
        tables.append(jnp.where(valid[None], bias, MASKED))
    return tables


def _dilated_attention(proj3):
    b, s, _ = proj3.shape
    assert s % SPAN == 0
    n_heads = N_HEADS_A

    def col_spec(base):
        return [pl.BlockSpec((1, SPAN, HEAD_DIM), (lambda bi, hh, i, c=base + g * HEADS_PER_GROUP: (bi, i, c + hh)))
                for g in range(len(DIL_GROUPS))]

    tables = _dilated_bias_tables()
    bias_specs = [pl.BlockSpec((1, QBLK, w + QBLK), lambda bi, hh, i: (hh, 0, 0)) for w, _ in DIL_GROUPS]
    return pl.pallas_call(
        _dilated_kernel,
        grid=(b, HEADS_PER_GROUP, s // SPAN),
        in_specs=col_spec(0) + col_spec(n_heads) + col_spec(2 * n_heads) + bias_specs,
        out_specs=pl.BlockSpec((1, SPAN, HEAD_DIM), lambda bi, hh, i: (bi, i, hh)),
        out_shape=jax.ShapeDtypeStruct((b, s, A_OUT), _BF16),
        scratch_shapes=[pltpu.VMEM((2 * SPAN, HEAD_DIM), _BF16) for _ in range(6)],
        compiler_params=_params(("arbitrary", "arbitrary", "arbitrary")),
        name="dilated_attention",
    )(*([proj3] * 9), *tables)


def _stick_kernel(q_ref, k_ref, v_ref, tri_ref, o_ref, *, tq, heads):
    i = pl.program_id(2)
    row = lax.broadcasted_iota(jnp.int32, (tq, tq), 0)
    col = lax.broadcasted_iota(jnp.int32, (tq, tq), 1)
    before = col < row

    def chunk(c, afters, accs, diagonal):
        k0 = pl.multiple_of(c * tq, tq)
        tri = tri_ref[...]
        lanes = [slice(h * HEAD_DIM, (h + 1) * HEAD_DIM) for h in range(heads)]
        zs = [lax.dot_general(q_ref[0, :, lanes[h]], k_ref[0, pl.ds(k0, tq), lanes[h]],
                              (((1,), (1,)), ((), ())), preferred_element_type=_F32) for h in range(heads)]
        csums = []
        for h in range(heads):
            z = zs[h]
            ln = -(jnp.maximum(z, 0.0) + jnp.log(1.0 + jnp.exp(-jnp.abs(z))))
            if diagonal:
                ln = jnp.where(before, ln, 0.0)
            hi = ln.astype(_BF16)
            lo = (ln - hi.astype(_F32)).astype(_BF16)
            csums.append(jnp.dot(hi, tri, preferred_element_type=_F32)
                         + jnp.dot(lo, tri, preferred_element_type=_F32))
        new_afters, new_accs = [], []
        for h in range(heads):
            w = jnp.exp(zs[h] + csums[h] + afters[h])
            if diagonal:
                w = jnp.where(before, w, 0.0)
            new_accs.append(accs[h] + jnp.dot(w.astype(_BF16), v_ref[0, pl.ds(k0, tq), lanes[h]],
                                              preferred_element_type=_F32))
            new_afters.append(afters[h] + csums[h][:, 0:1])
        return tuple(new_afters), tuple(new_accs)

    afters, accs = chunk(i, (jnp.zeros((tq, 1), _F32),) * heads, (jnp.zeros((tq, HEAD_DIM), _F32),) * heads, True)

    def keep_going(carry):
        c, afters, _ = carry
        return c >= 0

    def body(carry):
        c, afters, accs = carry
        afters, accs = chunk(c, afters, accs, False)
        return c - 1, afters, accs

    _, _, accs = lax.while_loop(keep_going, body, (i - 1, afters, accs))
    for h in range(heads):
        o_ref[0, :, h * HEAD_DIM:(h + 1) * HEAD_DIM] = accs[h].astype(o_ref.dtype)


def _stick_attention(proj3, *, tq, heads):
    b, s, _ = proj3.shape
    width = heads * HEAD_DIM
    assert s % tq == 0 and N_HEADS_B % heads == 0 and (3 * WIDTH_A) % width == 0
    groups = N_HEADS_B // heads
    base = 3 * WIDTH_A // width
    idx = jnp.arange(tq)
    tri = (idx[:, None] >= idx[None, :]).astype(_BF16)
    whole_seq = lambda part: pl.BlockSpec((1, s, width), lambda bi, hg, i: (bi, 0, base + part * groups + hg),
                                          pipeline_mode=pl.Buffered(1))
    return pl.pallas_call(
        functools.partial(_stick_kernel, tq=tq, heads=heads),
        grid=(b, groups, s // tq),
        in_specs=[
            pl.BlockSpec((1, tq, width), lambda bi, hg, i: (bi, i, base + hg)),
            whole_seq(1),
            whole_seq(2),
            pl.BlockSpec((tq, tq), lambda bi, hg, i: (0, 0)),
        ],
        out_specs=pl.BlockSpec((1, tq, width), lambda bi, hg, i: (bi, i, hg)),
        out_shape=jax.ShapeDtypeStruct((b, s, WIDTH_B), _BF16),
        compiler_params=_params(("parallel", "parallel", "arbitrary")),
        name="stick_breaking_attention",
    )(proj3, proj3, proj3, tri)


def _merge_kernel(ya_ref, yb_ref, ga_ref, gb_ref, h_ref, wa_ref, wb_ref, wo_ref, g_ref, o_ref):
    a = jnp.dot(ya_ref[...], wa_ref[...], preferred_element_type=_F32)
    bb = jnp.dot(yb_ref[...], wb_ref[...], preferred_element_type=_F32)
    m = (ga_ref[...].astype(_F32) * a + gb_ref[...].astype(_F32) * bb).astype(_BF16)
    mix = jnp.dot(m, wo_ref[...], preferred_element_type=_F32)
    o_ref[...] = h_ref[...] + _rmsnorm_rows(mix, g_ref[...])


def _merge(ya, yb, proj, h2d, wa, wb, wo, g, layer, *, tm):
    t, d = h2d.shape
    assert t % tm == 0 and QKV_WIDTH % d == 0
    gate_blk = QKV_WIDTH // d
    resident = lambda w: pl.BlockSpec((None,) + w.shape[1:], lambda i: (layer, 0, 0), pipeline_mode=pl.Buffered(1))
    return pl.pallas_call(
        _merge_kernel,
        grid=(t // tm,),
        in_specs=[
            pl.BlockSpec((tm, A_OUT), lambda i: (i, 0)),
            pl.BlockSpec((tm, WIDTH_B), lambda i: (i, 0)),
            pl.BlockSpec((tm, d), lambda i: (i, gate_blk)),
            pl.BlockSpec((tm, d), lambda i: (i, gate_blk + 1)),
            pl.BlockSpec((tm, d), lambda i: (i, 0)),
            resident(wa), resident(wb), resident(wo), _layer_rows(layer, d),
        ],
        out_specs=pl.BlockSpec((tm, d), lambda i: (i, 0)),
        out_shape=jax.ShapeDtypeStruct((t, d), _F32),
        compiler_params=_params(("parallel",)),
        name="gated_merge",
    )(ya, yb, proj, proj, h2d, wa, wb, wo, g)


def _ffn_kernel(h_ref, hp_ref, gpre_ref, wg_ref, wv_ref, cwg_ref, cwv_ref, cbg_ref, cbv_ref, wd_ref,
                gpost_ref, o_ref, u_ref, *, tiles_per_seq, chunk):
    i = pl.program_id(0)
    j = pl.program_id(1)
    tm = h_ref.shape[0]
    n_chunks = tm // chunk

    @pl.when(j == 0)
    def _():
        halo = _rmsnorm_rows(hp_ref[...], gpre_ref[...])
        at_sequence_start = (i % tiles_per_seq) == 0
        u_ref[0:CONV_HALO, :] = jnp.where(at_sequence_start, 0.0, halo).astype(u_ref.dtype)
        o_ref[...] = jnp.zeros(o_ref.shape, o_ref.dtype)

        def body(c, carry):
            r0 = pl.multiple_of(c * chunk, chunk)
            x = h_ref[pl.ds(r0, chunk), :]
            u_ref[pl.ds(CONV_HALO + r0, chunk), :] = _rmsnorm_rows(x, gpre_ref[...]).astype(u_ref.dtype)
            return carry
        lax.fori_loop(0, n_chunks, body, 0)

    u = u_ref[...]

    def conv_branch(w_ref, cw_ref, cb_ref):
        x = jnp.dot(u, w_ref[...], preferred_element_type=_F32)
        cw = cw_ref[...]
        y = cb_ref[...] + cw[0:1, :] * pltpu.roll(x, 2, 0)
        y = y + cw[1:2, :] * pltpu.roll(x, 1, 0)
        y = y + cw[2:3, :] * x
        return y[CONV_HALO:, :]

    gate = conv_branch(wg_ref, cwg_ref, cbg_ref)
    value = conv_branch(wv_ref, cwv_ref, cbv_ref)
    act = (jax.nn.gelu(gate, approximate=True) * value).astype(_BF16)
    o_ref[...] += jnp.dot(act, wd_ref[...], preferred_element_type=_F32)

    @pl.when(j == pl.num_programs(1) - 1)
    def _():
        def body(c, carry):
            r0 = pl.multiple_of(c * chunk, chunk)
            f = o_ref[pl.ds(r0, chunk), :]
            o_ref[pl.ds(r0, chunk), :] = h_ref[pl.ds(r0, chunk), :] + _rmsnorm_rows(f, gpost_ref[...])
            return carry
        lax.fori_loop(0, n_chunks, body, 0)


def _ffn(h2d, gpre, w_up, conv_w, conv_b, w_down, gpost, layer, *, tm, tf, seq):
    t, d = h2d.shape
    d_ff = w_down.shape[1]
    assert t % tm == 0 and seq % tm == 0 and d_ff % tf == 0 and tm % CONV_HALO == 0
    nj = d_ff // tf
    halo_blocks = tm // CONV_HALO
    return pl.pallas_call(
        functools.partial(_ffn_kernel, tiles_per_seq=seq // tm, chunk=min(tm, 128)),
        grid=(t // tm, nj),
        in_specs=[
            pl.BlockSpec((tm, d), lambda i, j: (i, 0)),
            pl.BlockSpec((CONV_HALO, d), lambda i, j: (jnp.maximum(i * halo_blocks - 1, 0), 0)),
            _layer_rows(layer, d),
            pl.BlockSpec((None, d, tf), lambda i, j: (layer, 0, j)),
            pl.BlockSpec((None, d, tf), lambda i, j: (layer, 0, nj + j)),
            pl.BlockSpec((None, CONV_WIDTH, tf), lambda i, j: (layer, 0, j)),
            pl.BlockSpec((None, CONV_WIDTH, tf), lambda i, j: (layer, 0, nj + j)),
            pl.BlockSpec((None, 1, tf), lambda i, j: (layer, 0, j)),
            pl.BlockSpec((None, 1, tf), lambda i, j: (layer, 0, nj + j)),
            pl.BlockSpec((None, tf, d), lambda i, j: (layer, j, 0)),
            _layer_rows(layer, d),
        ],
        out_specs=pl.BlockSpec((tm, d), lambda i, j: (i, 0)),
        out_shape=jax.ShapeDtypeStruct((t, d), _F32),
        scratch_shapes=[pltpu.VMEM((CONV_HALO + tm, d), _BF16)],
        compiler_params=_params(("parallel", "arbitrary")),
        name="conv_ffn",
    )(h2d, h2d, gpre, w_up, w_up, conv_w, conv_w, conv_b, conv_b, w_down, gpost)


def _ple_kernel(h_ref, p_ref, wg_ref, wi_ref, o_ref):
    h = h_ref[...]
    gate = jax.nn.sigmoid(jnp.dot(h.astype(_BF16), wg_ref[...], preferred_element_type=_F32))
    e = jnp.dot(p_ref[...].astype(_BF16), wi_ref[...], preferred_element_type=_F32)
    o_ref[...] = h + gate * e


def _ple(h2d, p3d, w_gate, w_in, layer, *, tm):
    t, d = h2d.shape
    pd = p3d.shape[2]
    assert t % tm == 0
    resident = lambda w: pl.BlockSpec((None,) + w.shape[1:], lambda i: (layer, 0, 0), pipeline_mode=pl.Buffered(1))
    return pl.pallas_call(
        _ple_kernel,
        grid=(t // tm,),
        in_specs=[
            pl.BlockSpec((tm, d), lambda i: (i, 0)),
            pl.BlockSpec((None, tm, pd), lambda i: (layer, i, 0)),
            resident(w_gate),
            resident(w_in),
        ],
        out_specs=pl.BlockSpec((tm, d), lambda i: (i, 0)),
        out_shape=jax.ShapeDtypeStruct((t, d), _F32),
        compiler_params=_params(("parallel",)),
        name="layer_embedding",
    )(h2d, p3d, w_gate, w_in)


def _tiles(t, seq, d, d_in, d_ff):
    def largest(n, cands):
        return next(c for c in cands if n % c == 0)
    return dict(
        proj_tm=largest(t, (1024, 512, 256, 128)),
        proj_tn=largest(math.gcd(d_in, QKV_WIDTH), (2048, 1024, 512, 256, 128)),
        stick_tq=largest(seq, (256, 128)),
        merge_tm=largest(t, (512, 256, 128)),
        ffn_tm=largest(seq, (1024, 512, 256, 128)),
        ffn_tf=largest(d_ff, (512, 256, 128)),
        ple_tm=largest(t, (1024, 512, 256, 128)),
    )


def kernel(x, p, g_mix_pre, w_in, w_branch_a, w_branch_b, w_out, g_mix_post, g_ffn_pre, w_up, conv_w,
           conv_b, w_down, g_ffn_post, w_ple_in, w_ple_gate):
    b, s, d = x.shape
    depth = w_in.shape[0]
    d_in = w_in.shape[2]
    d_ff = w_down.shape[1]
    t = b * s
    assert d_in == QKV_WIDTH + 2 * d
    tiles = _tiles(t, s, d, d_in, d_ff)

    q_scale = 1.0 / math.sqrt(HEAD_DIM)
    col = jnp.arange(d_in)
    is_q = (col < WIDTH_A) | ((col >= 3 * WIDTH_A) & (col < 3 * WIDTH_A + WIDTH_B))
    colscale = jnp.where(is_q, q_scale, 1.0).astype(_F32)[None, :]

    bf16 = lambda w: w.astype(_BF16)
    rows = lambda v: v[:, None, :]
    w_in, w_branch_a, w_branch_b, w_out = bf16(w_in), bf16(w_branch_a), bf16(w_branch_b), bf16(w_out)
    w_up, w_down, w_ple_in, w_ple_gate = bf16(w_up), bf16(w_down), bf16(w_ple_in), bf16(w_ple_gate)
    p3d = p.reshape(depth, t, p.shape[-1])

    h = x.reshape(t, d)
    for i in range(depth):
        proj = _in_proj(h, rows(g_mix_pre), colscale, w_in, i, tm=tiles["proj_tm"], tn=tiles["proj_tn"])
        proj3 = proj.reshape(b, s, d_in)
        ya = _dilated_attention(proj3).reshape(t, A_OUT)
        yb = _stick_attention(proj3, tq=tiles["stick_tq"], heads=N_HEADS_B).reshape(t, WIDTH_B)
        h = _merge(ya, yb, proj, h, w_branch_a, w_branch_b, w_out, rows(g_mix_post), i, tm=tiles["merge_tm"])
        h = _ffn(h, rows(g_ffn_pre), w_up, conv_w, rows(conv_b), w_down, rows(g_ffn_post), i,
                 tm=tiles["ffn_tm"], tf=tiles["ffn_tf"], seq=s)
        h = _ple(h, p3d, w_ple_gate, w_ple_in, i, tm=tiles["ple_tm"])
    return h.reshape(b, s, d)
```

```python
import functools
import math

import jax
import jax.numpy as jnp
from jax import lax
from jax.experimental import pallas as pl
from jax.experimental.pallas import tpu as pltpu

HEAD_DIM = 128
DIL_GROUPS = ((128, 1), (512, 4), (2048, 16))
HEADS_PER_GROUP = 4
N_HEADS_A = HEADS_PER_GROUP * len(DIL_GROUPS)
N_HEADS_B = 4
WIDTH_A = N_HEADS_A * HEAD_DIM
WIDTH_B = N_HEADS_B * HEAD_DIM
A_OUT = HEADS_PER_GROUP * HEAD_DIM
QKV_WIDTH = 3 * WIDTH_A + 3 * WIDTH_B
CONV_WIDTH = 3
RMS_EPS = 1e-6
ALIBI_MAX = 8.0
MASKED = -1e30
EXP_UNDERFLOW = -110.0
QBLK = 128
SPAN = 2048
DIL_BLOCKS_PER_TRIP = 4
CONV_HALO = 16
VMEM_LIMIT = 56 * 1024 * 1024

_F32 = jnp.float32
_BF16 = jnp.bfloat16


def _rmsnorm_rows(x, g):
    ms = jnp.mean(x * x, axis=-1, keepdims=True)
    return x * lax.rsqrt(ms + RMS_EPS) * g


def _params(sem):
    return pltpu.CompilerParams(dimension_semantics=sem, vmem_limit_bytes=VMEM_LIMIT)


def _layer_rows(layer, width):
    return pl.BlockSpec((None, 1, width), lambda *_: (layer, 0, 0))


def _in_proj_kernel(x_ref, g_ref, cs_ref, w_ref, o_ref, u_ref, *, gate_tile0, chunk):
    j = pl.program_id(1)
    tm = x_ref.shape[0]

    @pl.when(j == 0)
    def _():
        def body(c, carry):
            r0 = pl.multiple_of(c * chunk, chunk)
            x = x_ref[pl.ds(r0, chunk), :]
            u_ref[pl.ds(r0, chunk), :] = _rmsnorm_rows(x, g_ref[...]).astype(u_ref.dtype)
            return carry
        lax.fori_loop(0, tm // chunk, body, 0)

    acc = jnp.dot(u_ref[...], w_ref[...], preferred_element_type=_F32)

    tn = w_ref.shape[1]
    colscale = cs_ref[:, pl.ds(pl.multiple_of(j * tn, 128), tn)]
    o_ref[...] = jnp.where(j >= gate_tile0, jax.nn.sigmoid(acc), acc * colscale).astype(o_ref.dtype)


def _in_proj(h2d, g, colscale, w, layer, *, tm, tn):
    t, d = h2d.shape
    d_in = w.shape[2]
    assert t % tm == 0 and d_in % tn == 0 and QKV_WIDTH % tn == 0
    return pl.pallas_call(
        functools.partial(_in_proj_kernel, gate_tile0=QKV_WIDTH // tn, chunk=min(tm, 128)),
        grid=(t // tm, d_in // tn),
        in_specs=[
            pl.BlockSpec((tm, d), lambda i, j: (i, 0)),
            _layer_rows(layer, d),
            pl.BlockSpec((1, d_in), lambda i, j: (0, 0)),
            pl.BlockSpec((None, d, tn), lambda i, j: (layer, 0, j)),
        ],
        out_specs=pl.BlockSpec((tm, tn), lambda i, j: (i, j)),
        out_shape=jax.ShapeDtypeStruct((t, d_in), _BF16),
        scratch_shapes=[pltpu.VMEM((tm, d), _BF16)],
        compiler_params=_params(("parallel", "arbitrary")),
        name="in_proj",
    )(h2d, g, colscale, w)


def _dilated_kernel(q0, q1, q2, k0, k1, k2, v0, v1, v2, b0, b1, b2, o_ref,
                    kb0, kb1, kb2, vb0, vb1, vb2):
    i = pl.program_id(2)
    q_refs, bias_refs = (q0, q1, q2), (b0, b1, b2)
    k_bufs, v_bufs = (kb0, kb1, kb2), (vb0, vb1, vb2)

    for buf, new in zip(k_bufs + v_bufs, (k0, k1, k2, v0, v1, v2)):
        @pl.when(i == 0)
        def _(buf=buf):
            buf[0:SPAN, :] = jnp.zeros((SPAN, HEAD_DIM), buf.dtype)

        @pl.when(i > 0)
        def _(buf=buf):
            buf[0:SPAN, :] = buf[SPAN:2 * SPAN, :]

        buf[SPAN:2 * SPAN, :] = new[0]

    first_valid_row = jnp.where(i == 0, SPAN, 0)

    def masked_scores(o):
        scores, starts = [], []
        for g, (window, _) in enumerate(DIL_GROUPS):
            n = window + QBLK
            start = pl.multiple_of(o + SPAN - window, QBLK)
            q = q_refs[g][0, pl.ds(o, QBLK), :]
            kk = k_bufs[g][pl.ds(start, n), :]
            sc = lax.dot_general(q, kk, (((1,), (1,)), ((), ())), preferred_element_type=_F32)
            sc = sc + bias_refs[g][0]
            kj = lax.broadcasted_iota(jnp.int32, (QBLK, n), 1)
            scores.append(jnp.where(kj >= first_valid_row - start, sc, MASKED))
            starts.append(start)
        return scores, starts

    def joint_softmax_weights(scores):
        m = functools.reduce(jnp.maximum, [jnp.max(sc, axis=-1, keepdims=True) for sc in scores])
        ps = [jnp.exp(sc - m) for sc in scores]
        denom = functools.reduce(jnp.add, [jnp.sum(p, axis=-1, keepdims=True) for p in ps])
        return [p.astype(_BF16) for p in ps], denom

    def weighted_values(ps, starts):
        acc = jnp.zeros((QBLK, HEAD_DIM), _F32)
        for g, (window, _) in enumerate(DIL_GROUPS):
            vv = v_bufs[g][pl.ds(starts[g], window + QBLK), :]
            acc = acc + jnp.dot(ps[g], vv, preferred_element_type=_F32)
        return acc

    def body(s, carry):
        offs = [pl.multiple_of((s * DIL_BLOCKS_PER_TRIP + t) * QBLK, QBLK) for t in range(DIL_BLOCKS_PER_TRIP)]
        staged = [masked_scores(o) for o in offs]
        weights = [joint_softmax_weights(scores) for scores, _ in staged]
        for o, (_, starts), (ps, denom) in zip(offs, staged, weights):
            o_ref[0, pl.ds(o, QBLK), :] = (weighted_values(ps, starts) / denom).astype(o_ref.dtype)
        return carry

    lax.fori_loop(0, SPAN // (QBLK * DIL_BLOCKS_PER_TRIP), body, 0)


def _dilated_bias_tables():
    slopes = jnp.exp2(-ALIBI_MAX * jnp.arange(1, N_HEADS_A + 1, dtype=_F32) / N_HEADS_A)
    tables = []
    for g, (window, dilation) in enumerate(DIL_GROUPS):
        qi = jnp.arange(QBLK)[:, None]
        kj = jnp.arange(window + QBLK)[None, :]
        dist = qi + window - kj
        valid = (dist >= 0) & (dist <= window) & (dist % dilation == 0)
        sl = slopes[g * HEADS_PER_GROUP:(g + 1) * HEADS_PER_GROUP]
        bias = -sl[:, None, None] * dist.astype(_F32)[None]
        tables.append(jnp.where(valid[None], bias, MASKED))
    return tables


def _dilated_attention(proj3):
    b, s, _ = proj3.shape
    assert s % SPAN == 0
    n_heads = N_HEADS_A

    def col_spec(base):
        return [pl.BlockSpec((1, SPAN, HEAD_DIM), (lambda bi, hh, i, c=base + g * HEADS_PER_GROUP: (bi, i, c + hh)))
                for g in range(len(DIL_GROUPS))]

    tables = _dilated_bias_tables()
    bias_specs = [pl.BlockSpec((1, QBLK, w + QBLK), lambda bi, hh, i: (hh, 0, 0)) for w, _ in DIL_GROUPS]
    return pl.pallas_call(
        _dilated_kernel,
        grid=(b, HEADS_PER_GROUP, s // SPAN),
        in_specs=col_spec(0) + col_spec(n_heads) + col_spec(2 * n_heads) + bias_specs,
        out_specs=pl.BlockSpec((1, SPAN, HEAD_DIM), lambda bi, hh, i: (bi, i, hh)),
        out_shape=jax.ShapeDtypeStruct((b, s, A_OUT), _BF16),
        scratch_shapes=[pltpu.VMEM((2 * SPAN, HEAD_DIM), _BF16) for _ in range(6)],
        compiler_params=_params(("arbitrary", "arbitrary", "arbitrary")),
        name="dilated_attention",
    )(*([proj3] * 9), *tables)


def _stick_kernel(q_ref, k_ref, v_ref, tri_ref, o_ref, *, tq, heads):
    i = pl.program_id(2)
    row = lax.broadcasted_iota(jnp.int32, (tq, tq), 0)
    col = lax.broadcasted_iota(jnp.int32, (tq, tq), 1)
    before = col < row

    def chunk(c, afters, accs, diagonal):
        k0 = pl.multiple_of(c * tq, tq)
        tri = tri_ref[...]
        lanes = [slice(h * HEAD_DIM, (h + 1) * HEAD_DIM) for h in range(heads)]
        zs = [lax.dot_general(q_ref[0, :, lanes[h]], k_ref[0, pl.ds(k0, tq), lanes[h]],
                              (((1,), (1,)), ((), ())), preferred_element_type=_F32) for h in range(heads)]
        csums = []
        for h in range(heads):
            z = zs[h]
            ln = -(jnp.maximum(z, 0.0) + jnp.log(1.0 + jnp.exp(-jnp.abs(z))))
            if diagonal:
                ln = jnp.where(before, ln, 0.0)
            hi = ln.astype(_BF16)
            lo = (ln - hi.astype(_F32)).astype(_BF16)
            csums.append(jnp.dot(hi, tri, preferred_element_type=_F32)
                         + jnp.dot(lo, tri, preferred_element_type=_F32))
        new_afters, new_accs = [], []
        for h in range(heads):
            w = jnp.exp(zs[h] + csums[h] + afters[h])
            if diagonal:
                w = jnp.where(before, w, 0.0)
            new_accs.append(accs[h] + jnp.dot(w.astype(_BF16), v_ref[0, pl.ds(k0, tq), lanes[h]],
                                              preferred_element_type=_F32))
            new_afters.append(afters[h] + csums[h][:, 0:1])
        return tuple(new_afters), tuple(new_accs)

    afters, accs = chunk(i, (jnp.zeros((tq, 1), _F32),) * heads, (jnp.zeros((tq, HEAD_DIM), _F32),) * heads, True)

    def keep_going(carry):
        c, afters, _ = carry
        return jnp.logical_and(c >= 0, jnp.max(functools.reduce(jnp.maximum, afters)) > EXP_UNDERFLOW)

    def body(carry):
        c, afters, accs = carry
        afters, accs = chunk(c, afters, accs, False)
        return c - 1, afters, accs

    _, _, accs = lax.while_loop(keep_going, body, (i - 1, afters, accs))
    for h in range(heads):
        o_ref[0, :, h * HEAD_DIM:(h + 1) * HEAD_DIM] = accs[h].astype(o_ref.dtype)


def _stick_attention(proj3, *, tq, heads):
    b, s, _ = proj3.shape
    width = heads * HEAD_DIM
    assert s % tq == 0 and N_HEADS_B % heads == 0 and (3 * WIDTH_A) % width == 0
    groups = N_HEADS_B // heads
    base = 3 * WIDTH_A // width
    idx = jnp.arange(tq)
    tri = (idx[:, None] >= idx[None, :]).astype(_BF16)
    whole_seq = lambda part: pl.BlockSpec((1, s, width), lambda bi, hg, i: (bi, 0, base + part * groups + hg),
                                          pipeline_mode=pl.Buffered(1))
    return pl.pallas_call(
        functools.partial(_stick_kernel, tq=tq, heads=heads),
        grid=(b, groups, s // tq),
        in_specs=[
            pl.BlockSpec((1, tq, width), lambda bi, hg, i: (bi, i, base + hg)),
            whole_seq(1),
            whole_seq(2),
            pl.BlockSpec((tq, tq), lambda bi, hg, i: (0, 0)),
        ],
        out_specs=pl.BlockSpec((1, tq, width), lambda bi, hg, i: (bi, i, hg)),
        out_shape=jax.ShapeDtypeStruct((b, s, WIDTH_B), _BF16),
        compiler_params=_params(("parallel", "parallel", "arbitrary")),
        name="stick_breaking_attention",
    )(proj3, proj3, proj3, tri)


def _merge_kernel(ya_ref, yb_ref, ga_ref, gb_ref, h_ref, wa_ref, wb_ref, wo_ref, g_ref, o_ref):
    a = jnp.dot(ya_ref[...], wa_ref[...], preferred_element_type=_F32)
    bb = jnp.dot(yb_ref[...], wb_ref[...], preferred_element_type=_F32)
    m = (ga_ref[...].astype(_F32) * a + gb_ref[...].astype(_F32) * bb).astype(_BF16)
    mix = jnp.dot(m, wo_ref[...], preferred_element_type=_F32)
    o_ref[...] = h_ref[...] + _rmsnorm_rows(mix, g_ref[...])


def _merge(ya, yb, proj, h2d, wa, wb, wo, g, layer, *, tm):
    t, d = h2d.shape
    assert t % tm == 0 and QKV_WIDTH % d == 0
    gate_blk = QKV_WIDTH // d
    resident = lambda w: pl.BlockSpec((None,) + w.shape[1:], lambda i: (layer, 0, 0), pipeline_mode=pl.Buffered(1))
    return pl.pallas_call(
        _merge_kernel,
        grid=(t // tm,),
        in_specs=[
            pl.BlockSpec((tm, A_OUT), lambda i: (i, 0)),
            pl.BlockSpec((tm, WIDTH_B), lambda i: (i, 0)),
            pl.BlockSpec((tm, d), lambda i: (i, gate_blk)),
            pl.BlockSpec((tm, d), lambda i: (i, gate_blk + 1)),
            pl.BlockSpec((tm, d), lambda i: (i, 0)),
            resident(wa), resident(wb), resident(wo), _layer_rows(layer, d),
        ],
        out_specs=pl.BlockSpec((tm, d), lambda i: (i, 0)),
        out_shape=jax.ShapeDtypeStruct((t, d), _F32),
        compiler_params=_params(("parallel",)),
        name="gated_merge",
    )(ya, yb, proj, proj, h2d, wa, wb, wo, g)


def _ffn_kernel(h_ref, hp_ref, gpre_ref, wg_ref, wv_ref, cw_ref, cb_ref, wd_ref,
                gpost_ref, o_ref, u_ref, *, tiles_per_seq, chunk):
    i = pl.program_id(0)
    j = pl.program_id(1)
    tm = h_ref.shape[0]
    n_chunks = tm // chunk

    @pl.when(j == 0)
    def _():
        halo = _rmsnorm_rows(hp_ref[...], gpre_ref[...])
        at_sequence_start = (i % tiles_per_seq) == 0
        u_ref[0:CONV_HALO, :] = jnp.where(at_sequence_start, 0.0, halo).astype(u_ref.dtype)
        o_ref[...] = jnp.zeros(o_ref.shape, o_ref.dtype)

        def body(c, carry):
            r0 = pl.multiple_of(c * chunk, chunk)
            x = h_ref[pl.ds(r0, chunk), :]
            u_ref[pl.ds(CONV_HALO + r0, chunk), :] = _rmsnorm_rows(x, gpre_ref[...]).astype(u_ref.dtype)
            return carry
        lax.fori_loop(0, n_chunks, body, 0)

    u = u_ref[...]
    tf = wg_ref.shape[1]
    d_ff = cw_ref.shape[1] // 2

    def conv_branch(w_ref, half):
        cols = pl.ds(pl.multiple_of(half * d_ff + j * tf, 128), tf)
        x = jnp.dot(u, w_ref[...], preferred_element_type=_F32)
        cw = cw_ref[:, cols]
        y = cb_ref[:, cols] + cw[0:1, :] * pltpu.roll(x, 2, 0)
        y = y + cw[1:2, :] * pltpu.roll(x, 1, 0)
        y = y + cw[2:3, :] * x
        return y[CONV_HALO:, :]

    gate = conv_branch(wg_ref, 0)
    value = conv_branch(wv_ref, 1)
    act = (jax.nn.gelu(gate, approximate=True) * value).astype(_BF16)
    o_ref[...] += jnp.dot(act, wd_ref[...], preferred_element_type=_F32)

    @pl.when(j == pl.num_programs(1) - 1)
    def _():
        def body(c, carry):
            r0 = pl.multiple_of(c * chunk, chunk)
            f = o_ref[pl.ds(r0, chunk), :]
            o_ref[pl.ds(r0, chunk), :] = h_ref[pl.ds(r0, chunk), :] + _rmsnorm_rows(f, gpost_ref[...])
            return carry
        lax.fori_loop(0, n_chunks, body, 0)


def _ffn(h2d, gpre, w_up, conv_w, conv_b, w_down, gpost, layer, *, tm, tf, seq):
    t, d = h2d.shape
    d_ff = w_down.shape[1]
    assert t % tm == 0 and seq % tm == 0 and d_ff % tf == 0 and tm % CONV_HALO == 0
    nj = d_ff // tf
    halo_blocks = tm // CONV_HALO
    return pl.pallas_call(
        functools.partial(_ffn_kernel, tiles_per_seq=seq // tm, chunk=min(tm, 128)),
        grid=(t // tm, nj),
        in_specs=[
            pl.BlockSpec((tm, d), lambda i, j: (i, 0)),
            pl.BlockSpec((CONV_HALO, d), lambda i, j: (jnp.maximum(i * halo_blocks - 1, 0), 0)),
            _layer_rows(layer, d),
            pl.BlockSpec((None, d, tf), lambda i, j: (layer, 0, j)),
            pl.BlockSpec((None, d, tf), lambda i, j: (layer, 0, nj + j)),
            pl.BlockSpec((None, CONV_WIDTH, 2 * d_ff), lambda i, j: (layer, 0, 0)),
            pl.BlockSpec((None, 1, 2 * d_ff), lambda i, j: (layer, 0, 0)),
            pl.BlockSpec((None, tf, d), lambda i, j: (layer, j, 0)),
            _layer_rows(layer, d),
        ],
        out_specs=pl.BlockSpec((tm, d), lambda i, j: (i, 0)),
        out_shape=jax.ShapeDtypeStruct((t, d), _F32),
        scratch_shapes=[pltpu.VMEM((CONV_HALO + tm, d), _BF16)],
        compiler_params=_params(("parallel", "arbitrary")),
        name="conv_ffn",
    )(h2d, h2d, gpre, w_up, w_up, conv_w, conv_b, w_down, gpost)


def _ple_kernel(h_ref, p_ref, wg_ref, wi_ref, o_ref):
    h = h_ref[...]
    gate = jax.nn.sigmoid(jnp.dot(h.astype(_BF16), wg_ref[...], preferred_element_type=_F32))
    e = jnp.dot(p_ref[...].astype(_BF16), wi_ref[...], preferred_element_type=_F32)
    o_ref[...] = h + gate * e


def _ple(h2d, p3d, w_gate, w_in, layer, *, tm):
    t, d = h2d.shape
    pd = p3d.shape[2]
    assert t % tm == 0
    resident = lambda w: pl.BlockSpec((None,) + w.shape[1:], lambda i: (layer, 0, 0), pipeline_mode=pl.Buffered(1))
    return pl.pallas_call(
        _ple_kernel,
        grid=(t // tm,),
        in_specs=[
            pl.BlockSpec((tm, d), lambda i: (i, 0)),
            pl.BlockSpec((None, tm, pd), lambda i: (layer, i, 0)),
            resident(w_gate),
            resident(w_in),
        ],
        out_specs=pl.BlockSpec((tm, d), lambda i: (i, 0)),
        out_shape=jax.ShapeDtypeStruct((t, d), _F32),
        compiler_params=_params(("parallel",)),
        name="layer_embedding",
    )(h2d, p3d, w_gate, w_in)


def _tiles(t, seq, d, d_in, d_ff):
    def largest(n, cands):
        return next(c for c in cands if n % c == 0)
    return dict(
        proj_tm=largest(t, (1024, 512, 256, 128)),
        proj_tn=largest(math.gcd(d_in, QKV_WIDTH), (2048, 1024, 512, 256, 128)),
        stick_tq=largest(seq, (256, 128)),
        merge_tm=largest(t, (512, 256, 128)),
        ffn_tm=largest(seq, (1024, 512, 256, 128)),
        ffn_tf=largest(d_ff, (512, 256, 128)),
        ple_tm=largest(t, (1024, 512, 256, 128)),
    )


def kernel(x, p, g_mix_pre, w_in, w_branch_a, w_branch_b, w_out, g_mix_post, g_ffn_pre, w_up, conv_w,
           conv_b, w_down, g_ffn_post, w_ple_in, w_ple_gate):
    b, s, d = x.shape
    depth = w_in.shape[0]
    d_in = w_in.shape[2]
    d_ff = w_down.shape[1]
    t = b * s
    assert d_in == QKV_WIDTH + 2 * d
    tiles = _tiles(t, s, d, d_in, d_ff)

    q_scale = 1.0 / math.sqrt(HEAD_DIM)
    col = jnp.arange(d_in)
    is_q = (col < WIDTH_A) | ((col >= 3 * WIDTH_A) & (col < 3 * WIDTH_A + WIDTH_B))
    colscale = jnp.where(is_q, q_scale, 1.0).astype(_F32)[None, :]

    bf16 = lambda w: w.astype(_BF16)
    rows = lambda v: v[:, None, :]
    w_in, w_branch_a, w_branch_b, w_out = bf16(w_in), bf16(w_branch_a), bf16(w_branch_b), bf16(w_out)
    w_up, w_down, w_ple_in, w_ple_gate = bf16(w_up), bf16(w_down), bf16(w_ple_in), bf16(w_ple_gate)
    p3d = p.reshape(depth, t, p.shape[-1])

    h = x.reshape(t, d)
    for i in range(depth):
        proj = _in_proj(h, rows(g_mix_pre), colscale, w_in, i, tm=tiles["proj_tm"], tn=tiles["proj_tn"])
        proj3 = proj.reshape(b, s, d_in)
        ya = _dilated_attention(proj3).reshape(t, A_OUT)
        yb = _stick_attention(proj3, tq=tiles["stick_tq"], heads=N_HEADS_B).reshape(t, WIDTH_B)
        h = _merge(ya, yb, proj, h, w_branch_a, w_branch_b, w_out, rows(g_mix_post), i, tm=tiles["merge_tm"])
        h = _ffn(h, rows(g_ffn_pre), w_up, conv_w, rows(conv_b), w_down, rows(g_ffn_post), i,
                 tm=tiles["ffn_tm"], tf=tiles["ffn_tf"], seq=s)
        h = _ple(h, p3d, w_ple_gate, w_ple_in, i, tm=tiles["ple_tm"])
    return h.reshape(b, s, d)
```

```python
import functools
import math

import jax
import jax.numpy as jnp
from jax import lax
from jax.experimental import pallas as pl
from jax.experimental.pallas import tpu as pltpu

HEAD_DIM = 128
DIL_GROUPS = ((128, 1), (512, 4), (2048, 16))
NEAR_GROUPS = DIL_GROUPS[:-1]
FAR_RESIDUES_PER_STEP = 8
HEADS_PER_GROUP = 4
N_HEADS_A = HEADS_PER_GROUP * len(DIL_GROUPS)
N_HEADS_B = 4
WIDTH_A = N_HEADS_A * HEAD_DIM
WIDTH_B = N_HEADS_B * HEAD_DIM
A_OUT = HEADS_PER_GROUP * HEAD_DIM
QKV_WIDTH = 3 * WIDTH_A + 3 * WIDTH_B
CONV_WIDTH = 3
RMS_EPS = 1e-6
ALIBI_MAX = 8.0
MASKED = -1e30
EXP_UNDERFLOW = -110.0
QBLK = 128
SPAN = 2048
DIL_BLOCKS_PER_TRIP = 16
CONV_HALO = 16
VMEM_LIMIT = 56 * 1024 * 1024
IN_PROJ_VMEM_LIMIT = 60 * 1024 * 1024

_F32 = jnp.float32
_BF16 = jnp.bfloat16


def _rmsnorm_rows(x, g):
    ms = jnp.mean(x * x, axis=-1, keepdims=True)
    return x * lax.rsqrt(ms + RMS_EPS) * g


def _params(sem, vmem_limit=VMEM_LIMIT):
    return pltpu.CompilerParams(dimension_semantics=sem, vmem_limit_bytes=vmem_limit)


def _layer_rows(layer, width):
    return pl.BlockSpec((None, 1, width), lambda *_: (layer, 0, 0))


def _in_proj_kernel(x_ref, g_ref, cs_ref, w_ref, o_ref, far_ref, u_ref, *, gate_tile0, chunk, far_cols):
    j = pl.program_id(1)
    tm = x_ref.shape[0]
    tn = w_ref.shape[1]

    @pl.when(j == 0)
    def _():
        def body(c, carry):
            r0 = pl.multiple_of(c * chunk, chunk)
            x = x_ref[pl.ds(r0, chunk), :]
            u_ref[pl.ds(r0, chunk), :] = _rmsnorm_rows(x, g_ref[...]).astype(u_ref.dtype)
            return carry
        lax.fori_loop(0, tm // chunk, body, 0)

    acc = jnp.dot(u_ref[...], w_ref[...], preferred_element_type=_F32)

    o_ref[...] = jnp.where(j >= gate_tile0, jax.nn.sigmoid(acc), acc * cs_ref[...]).astype(o_ref.dtype)

    for col in far_cols:
        @pl.when(j == col // tn)
        def _(col=col):
            cols = slice(col % tn, col % tn + A_OUT)
            far_ref[...] = acc[:, cols] * cs_ref[:, cols]


def _in_proj(h2d, g, colscale, w, layer, *, tm, tn):
    t, d = h2d.shape
    d_in = w.shape[2]
    assert t % tm == 0 and d_in % tn == 0 and QKV_WIDTH % tn == 0 and tn % A_OUT == 0
    far_cols = tuple(part * WIDTH_A + (len(DIL_GROUPS) - 1) * A_OUT for part in range(3))
    far_tiles = [c // tn for c in far_cols]
    assert len(set(far_tiles)) == 3
    far_part = lambda j: (j >= far_tiles[1]).astype(jnp.int32) + (j >= far_tiles[2]).astype(jnp.int32)
    return pl.pallas_call(
        functools.partial(_in_proj_kernel, gate_tile0=QKV_WIDTH // tn, chunk=min(tm, 128), far_cols=far_cols),
        grid=(t // tm, d_in // tn),
        in_specs=[
            pl.BlockSpec((tm, d), lambda i, j: (i, 0)),
            _layer_rows(layer, d),
            pl.BlockSpec((1, tn), lambda i, j: (0, j)),
            pl.BlockSpec((None, d, tn), lambda i, j: (layer, 0, j)),
        ],
        out_specs=[pl.BlockSpec((tm, tn), lambda i, j: (i, j)),
                   pl.BlockSpec((tm, A_OUT), lambda i, j: (i, far_part(j)))],
        out_shape=[jax.ShapeDtypeStruct((t, d_in), _BF16), jax.ShapeDtypeStruct((t, 3 * A_OUT), _F32)],
        scratch_shapes=[pltpu.VMEM((tm, d), _BF16)],
        compiler_params=_params(("parallel", "arbitrary"), IN_PROJ_VMEM_LIMIT),
        name="in_proj",
    )(h2d, g, colscale, w)


def _dilated_kernel(q0, q1, k0, k1, v0, v1, b0, b1, far_out_ref, far_lse_ref, o_ref, kb0, kb1, vb0, vb1):
    i = pl.program_id(2)
    q_refs, bias_refs = (q0, q1), (b0, b1)
    k_bufs, v_bufs = (kb0, kb1), (vb0, vb1)

    for buf, new in zip(k_bufs + v_bufs, (k0, k1, v0, v1)):
        @pl.when(i == 0)
        def _(buf=buf):
            buf[0:SPAN, :] = jnp.zeros((SPAN, HEAD_DIM), buf.dtype)

        @pl.when(i > 0)
        def _(buf=buf):
            buf[0:SPAN, :] = buf[SPAN:2 * SPAN, :]

        buf[SPAN:2 * SPAN, :] = new[0]

    first_valid_row = jnp.where(i == 0, SPAN, 0)

    def masked_scores(o):
        scores, starts = [], []
        for g, (window, _) in enumerate(NEAR_GROUPS):
            n = window + QBLK
            start = pl.multiple_of(o + SPAN - window, QBLK)
            q = q_refs[g][0, pl.ds(o, QBLK), :]
            kk = k_bufs[g][pl.ds(start, n), :]
            sc = lax.dot_general(q, kk, (((1,), (1,)), ((), ())), preferred_element_type=_F32)
            sc = sc + bias_refs[g][0]
            kj = lax.broadcasted_iota(jnp.int32, (QBLK, n), 1)
            scores.append(jnp.where(kj >= first_valid_row - start, sc, MASKED))
            starts.append(start)
        return scores, starts

    def joint_softmax_weights(scores, o):
        far_lse = far_lse_ref[0, pl.ds(o, QBLK), 0:1]
        m = functools.reduce(jnp.maximum, [jnp.max(sc, axis=-1, keepdims=True) for sc in scores] + [far_lse])
        ps = [jnp.exp(sc - m) for sc in scores]
        far_weight = jnp.exp(far_lse - m)
        denom = functools.reduce(jnp.add, [jnp.sum(p, axis=-1, keepdims=True) for p in ps] + [far_weight])
        return [p.astype(_BF16) for p in ps], far_weight, denom

    def weighted_values(ps, far_weight, starts, o):
        acc = far_weight * far_out_ref[0, pl.ds(o, QBLK), :]
        for g, (window, _) in enumerate(NEAR_GROUPS):
            vv = v_bufs[g][pl.ds(starts[g], window + QBLK), :]
            acc = acc + jnp.dot(ps[g], vv, preferred_element_type=_F32)
        return acc

    def body(s, carry):
        offs = [pl.multiple_of((s * DIL_BLOCKS_PER_TRIP + t) * QBLK, QBLK) for t in range(DIL_BLOCKS_PER_TRIP)]
        staged = [masked_scores(o) for o in offs]
        weights = [joint_softmax_weights(scores, o) for o, (scores, _) in zip(offs, staged)]
        for o, (_, starts), (ps, far_weight, denom) in zip(offs, staged, weights):
            y = weighted_values(ps, far_weight, starts, o) / denom
            o_ref[0, pl.ds(o, QBLK), :] = y.astype(o_ref.dtype)
        return carry

    lax.fori_loop(0, SPAN // (QBLK * DIL_BLOCKS_PER_TRIP), body, 0)


def _alibi_slopes(group):
    slopes = jnp.exp2(-ALIBI_MAX * jnp.arange(1, N_HEADS_A + 1, dtype=_F32) / N_HEADS_A)
    return slopes[group * HEADS_PER_GROUP:(group + 1) * HEADS_PER_GROUP]


def _dilated_bias_tables():
    tables = []
    for g, (window, dilation) in enumerate(NEAR_GROUPS):
        qi = jnp.arange(QBLK)[:, None]
        kj = jnp.arange(window + QBLK)[None, :]
        dist = qi + window - kj
        valid = (dist >= 0) & (dist <= window) & (dist % dilation == 0)
        bias = -_alibi_slopes(g)[:, None, None] * dist.astype(_F32)[None]
        tables.append(jnp.where(valid[None], bias, MASKED))
    return tables


def _far_bias_table():
    window, dilation = DIL_GROUPS[-1]
    qi = jnp.arange(QBLK)[:, None]
    kj = jnp.arange(2 * QBLK)[None, :]
    steps = QBLK + qi - kj
    valid = (steps >= 0) & (steps <= window // dilation)
    bias = -_alibi_slopes(len(DIL_GROUPS) - 1)[:, None, None] * (steps * dilation).astype(_F32)[None]
    return jnp.where(valid[None], bias, MASKED)


def _far_kernel(q_ref, kp_ref, kc_ref, vp_ref, vc_ref, bias_ref, out_ref, lse_ref):
    n = pl.program_id(1)
    kj = lax.broadcasted_iota(jnp.int32, (QBLK, 2 * QBLK), 1)
    in_sequence = kj >= jnp.where(n == 0, QBLK, 0)
    residue_major = lambda ref: pltpu.einshape("mrd->rmd", ref[0]).astype(_BF16)
    q_all = residue_major(q_ref)
    k_all = jnp.concatenate([residue_major(kp_ref), residue_major(kc_ref)], axis=1)
    v_all = jnp.concatenate([residue_major(vp_ref), residue_major(vc_ref)], axis=1)
    outs, lses = [], []
    for r in range(FAR_RESIDUES_PER_STEP):
        lanes = [slice(h * HEAD_DIM, (h + 1) * HEAD_DIM) for h in range(HEADS_PER_GROUP)]
        scores = []
        for h in range(HEADS_PER_GROUP):
            sc = lax.dot_general(q_all[r][:, lanes[h]], k_all[r][:, lanes[h]], (((1,), (1,)), ((), ())),
                                 preferred_element_type=_F32) + bias_ref[h]
            scores.append(jnp.where(in_sequence, sc, MASKED))
        out_r, lse_r = [], []
        for h in range(HEADS_PER_GROUP):
            m = jnp.max(scores[h], axis=-1, keepdims=True)
            p = jnp.exp(scores[h] - m)
            denom = jnp.sum(p, axis=-1, keepdims=True)
            acc = jnp.dot(p.astype(_BF16), v_all[r][:, lanes[h]], preferred_element_type=_F32)
            out_r.append(acc / denom)
            lse_r.append(jnp.broadcast_to(m + jnp.log(denom), (QBLK, HEAD_DIM)))
        outs.append(jnp.concatenate(out_r, axis=1))
        lses.append(jnp.concatenate(lse_r, axis=1))
    out_ref[0] = pltpu.einshape("rmd->mrd", jnp.stack(outs))
    lse_ref[0] = pltpu.einshape("rmd->mrd", jnp.stack(lses))


def _far_attention(far_qkv, bias):
    b, s, _ = far_qkv.shape
    dilation = DIL_GROUPS[-1][1]
    positions = s // dilation
    assert s % (dilation * QBLK) == 0 and dilation % FAR_RESIDUES_PER_STEP == 0
    view = far_qkv.reshape(b, positions, dilation, 3 * A_OUT)
    blk = (1, QBLK, FAR_RESIDUES_PER_STEP, A_OUT)
    cur = lambda part: pl.BlockSpec(blk, lambda bi, n, a: (bi, n, a, part))
    prev = lambda part: pl.BlockSpec(blk, lambda bi, n, a: (bi, jnp.maximum(n - 1, 0), a, part))
    out_spec = pl.BlockSpec(blk, lambda bi, n, a: (bi, n, a, 0))
    shape = jax.ShapeDtypeStruct((b, positions, dilation, A_OUT), _F32)
    out, lse = pl.pallas_call(
        _far_kernel,
        grid=(b, positions // QBLK, dilation // FAR_RESIDUES_PER_STEP),
        in_specs=[cur(0), prev(1), cur(1), prev(2), cur(2),
                  pl.BlockSpec(bias.shape, lambda bi, n, a: (0, 0, 0))],
        out_specs=[out_spec, out_spec],
        out_shape=[shape, shape],
        compiler_params=_params(("parallel", "parallel", "parallel")),
        name="far_dilated_attention",
    )(view, view, view, view, view, bias)
    return out.reshape(b, s, A_OUT), lse.reshape(b, s, A_OUT)


def _dilated_attention(proj3, far_qkv):
    b, s, _ = proj3.shape
    assert s % SPAN == 0
    n_heads = N_HEADS_A
    tables = _dilated_bias_tables()
    far_out, far_lse = _far_attention(far_qkv, _far_bias_table())

    def col_spec(base):
        return [pl.BlockSpec((1, SPAN, HEAD_DIM), (lambda bi, hh, i, c=base + g * HEADS_PER_GROUP: (bi, i, c + hh)))
                for g in range(len(NEAR_GROUPS))]

    bias_specs = [pl.BlockSpec((1, QBLK, w + QBLK), lambda bi, hh, i: (hh, 0, 0)) for w, _ in NEAR_GROUPS]
    far_spec = pl.BlockSpec((1, SPAN, HEAD_DIM), lambda bi, hh, i: (bi, i, hh))
    return pl.pallas_call(
        _dilated_kernel,
        grid=(b, HEADS_PER_GROUP, s // SPAN),
        in_specs=col_spec(0) + col_spec(n_heads) + col_spec(2 * n_heads) + bias_specs + [far_spec, far_spec],
        out_specs=pl.BlockSpec((1, SPAN, HEAD_DIM), lambda bi, hh, i: (bi, i, hh)),
        out_shape=jax.ShapeDtypeStruct((b, s, A_OUT), _BF16),
        scratch_shapes=[pltpu.VMEM((2 * SPAN, HEAD_DIM), _BF16) for _ in range(2 * len(NEAR_GROUPS))],
        compiler_params=_params(("arbitrary", "arbitrary", "arbitrary")),
        name="dilated_attention",
    )(*([proj3] * (3 * len(NEAR_GROUPS))), *tables, far_out, far_lse)


def _stick_kernel(q_ref, k_ref, v_ref, tri_ref, o_ref, *, tq, heads):
    i = pl.program_id(2)
    row = lax.broadcasted_iota(jnp.int32, (tq, tq), 0)
    col = lax.broadcasted_iota(jnp.int32, (tq, tq), 1)
    before = col < row

    def chunk(c, afters, accs, diagonal):
        k0 = pl.multiple_of(c * tq, tq)
        tri = tri_ref[...]
        lanes = [slice(h * HEAD_DIM, (h + 1) * HEAD_DIM) for h in range(heads)]
        zs = [lax.dot_general(q_ref[0, :, lanes[h]], k_ref[0, pl.ds(k0, tq), lanes[h]],
                              (((1,), (1,)), ((), ())), preferred_element_type=_F32) for h in range(heads)]
        csums = []
        for h in range(heads):
            z = zs[h]
            ln = -(jnp.maximum(z, 0.0) + jnp.log(1.0 + jnp.exp(-jnp.abs(z))))
            if diagonal:
                ln = jnp.where(before, ln, 0.0)
            hi = ln.astype(_BF16)
            lo = (ln - hi.astype(_F32)).astype(_BF16)
            csums.append(jnp.dot(hi, tri, preferred_element_type=_F32)
                         + jnp.dot(lo, tri, preferred_element_type=_F32))
        new_afters, new_accs = [], []
        for h in range(heads):
            w = jnp.exp(zs[h] + csums[h] + afters[h])
            if diagonal:
                w = jnp.where(before, w, 0.0)
            new_accs.append(accs[h] + jnp.dot(w.astype(_BF16), v_ref[0, pl.ds(k0, tq), lanes[h]],
                                              preferred_element_type=_F32))
            new_afters.append(afters[h] + csums[h][:, 0:1])
        return tuple(new_afters), tuple(new_accs)

    afters, accs = chunk(i, (jnp.zeros((tq, 1), _F32),) * heads, (jnp.zeros((tq, HEAD_DIM), _F32),) * heads, True)

    def keep_going(carry):
        c, afters, _ = carry
        return jnp.logical_and(c >= 0, jnp.max(functools.reduce(jnp.maximum, afters)) > EXP_UNDERFLOW)

    def body(carry):
        c, afters, accs = carry
        afters, accs = chunk(c, afters, accs, False)
        return c - 1, afters, accs

    _, _, accs = lax.while_loop(keep_going, body, (i - 1, afters, accs))
    for h in range(heads):
        o_ref[0, :, h * HEAD_DIM:(h + 1) * HEAD_DIM] = accs[h].astype(o_ref.dtype)


def _stick_attention(proj3, *, tq, heads):
    b, s, _ = proj3.shape
    width = heads * HEAD_DIM
    assert s % tq == 0 and N_HEADS_B % heads == 0 and (3 * WIDTH_A) % width == 0
    groups = N_HEADS_B // heads
    base = 3 * WIDTH_A // width
    idx = jnp.arange(tq)
    tri = (idx[:, None] >= idx[None, :]).astype(_BF16)
    whole_seq = lambda part: pl.BlockSpec((1, s, width), lambda bi, hg, i: (bi, 0, base + part * groups + hg),
                                          pipeline_mode=pl.Buffered(1))
    return pl.pallas_call(
        functools.partial(_stick_kernel, tq=tq, heads=heads),
        grid=(b, groups, s // tq),
        in_specs=[
            pl.BlockSpec((1, tq, width), lambda bi, hg, i: (bi, i, base + hg)),
            whole_seq(1),
            whole_seq(2),
            pl.BlockSpec((tq, tq), lambda bi, hg, i: (0, 0)),
        ],
        out_specs=pl.BlockSpec((1, tq, width), lambda bi, hg, i: (bi, i, hg)),
        out_shape=jax.ShapeDtypeStruct((b, s, WIDTH_B), _BF16),
        compiler_params=_params(("parallel", "parallel", "arbitrary")),
        name="stick_breaking_attention",
    )(proj3, proj3, proj3, tri)


def _merge_kernel(ya_ref, yb_ref, ga_ref, gb_ref, h_ref, wa_ref, wb_ref, wo_ref, g_ref, o_ref):
    a = jnp.dot(ya_ref[...], wa_ref[...], preferred_element_type=_F32)
    bb = jnp.dot(yb_ref[...], wb_ref[...], preferred_element_type=_F32)
    m = (ga_ref[...].astype(_F32) * a + gb_ref[...].astype(_F32) * bb).astype(_BF16)
    mix = jnp.dot(m, wo_ref[...], preferred_element_type=_F32)
    o_ref[...] = h_ref[...] + _rmsnorm_rows(mix, g_ref[...])


def _merge(ya, yb, proj, h2d, wa, wb, wo, g, layer, *, tm):
    t, d = h2d.shape
    assert t % tm == 0 and QKV_WIDTH % d == 0
    gate_blk = QKV_WIDTH // d
    resident = lambda w: pl.BlockSpec((None,) + w.shape[1:], lambda i: (layer, 0, 0), pipeline_mode=pl.Buffered(1))
    return pl.pallas_call(
        _merge_kernel,
        grid=(t // tm,),
        in_specs=[
            pl.BlockSpec((tm, A_OUT), lambda i: (i, 0)),
            pl.BlockSpec((tm, WIDTH_B), lambda i: (i, 0)),
            pl.BlockSpec((tm, d), lambda i: (i, gate_blk)),
            pl.BlockSpec((tm, d), lambda i: (i, gate_blk + 1)),
            pl.BlockSpec((tm, d), lambda i: (i, 0)),
            resident(wa), resident(wb), resident(wo), _layer_rows(layer, d),
        ],
        out_specs=pl.BlockSpec((tm, d), lambda i: (i, 0)),
        out_shape=jax.ShapeDtypeStruct((t, d), _F32),
        compiler_params=_params(("parallel",)),
        name="gated_merge",
    )(ya, yb, proj, proj, h2d, wa, wb, wo, g)


def _ffn_kernel(h_ref, hp_ref, gpre_ref, wg_ref, wv_ref, cwg_ref, cwv_ref, cbg_ref, cbv_ref, wd_ref,
                gpost_ref, o_ref, u_ref, *, tiles_per_seq, chunk):
    i = pl.program_id(0)
    j = pl.program_id(1)
    tm = h_ref.shape[0]
    n_chunks = tm // chunk

    @pl.when(j == 0)
    def _():
        halo = _rmsnorm_rows(hp_ref[...], gpre_ref[...])
        at_sequence_start = (i % tiles_per_seq) == 0
        u_ref[0:CONV_HALO, :] = jnp.where(at_sequence_start, 0.0, halo).astype(u_ref.dtype)
        o_ref[...] = jnp.zeros(o_ref.shape, o_ref.dtype)

        def body(c, carry):
            r0 = pl.multiple_of(c * chunk, chunk)
            x = h_ref[pl.ds(r0, chunk), :]
            u_ref[pl.ds(CONV_HALO + r0, chunk), :] = _rmsnorm_rows(x, gpre_ref[...]).astype(u_ref.dtype)
            return carry
        lax.fori_loop(0, n_chunks, body, 0)

    u = u_ref[...]

    def conv_branch(w_ref, cw_ref, cb_ref):
        x = jnp.dot(u, w_ref[...], preferred_element_type=_F32)
        cw = cw_ref[...]
        y = cb_ref[...] + cw[0:1, :] * pltpu.roll(x, 2, 0)
        y = y + cw[1:2, :] * pltpu.roll(x, 1, 0)
        y = y + cw[2:3, :] * x
        return y[CONV_HALO:, :]

    gate = conv_branch(wg_ref, cwg_ref, cbg_ref)
    value = conv_branch(wv_ref, cwv_ref, cbv_ref)
    act = (jax.nn.gelu(gate, approximate=True) * value).astype(_BF16)
    o_ref[...] += jnp.dot(act, wd_ref[...], preferred_element_type=_F32)

    @pl.when(j == pl.num_programs(1) - 1)
    def _():
        def body(c, carry):
            r0 = pl.multiple_of(c * chunk, chunk)
            f = o_ref[pl.ds(r0, chunk), :]
            o_ref[pl.ds(r0, chunk), :] = h_ref[pl.ds(r0, chunk), :] + _rmsnorm_rows(f, gpost_ref[...])
            return carry
        lax.fori_loop(0, n_chunks, body, 0)


def _ffn(h2d, gpre, w_up, conv_w, conv_b, w_down, gpost, layer, *, tm, tf, seq):
    t, d = h2d.shape
    d_ff = w_down.shape[1]
    assert t % tm == 0 and seq % tm == 0 and d_ff % tf == 0 and tm % CONV_HALO == 0
    nj = d_ff // tf
    halo_blocks = tm // CONV_HALO
    return pl.pallas_call(
        functools.partial(_ffn_kernel, tiles_per_seq=seq // tm, chunk=min(tm, 128)),
        grid=(t // tm, nj),
        in_specs=[
            pl.BlockSpec((tm, d), lambda i, j: (i, 0)),
            pl.BlockSpec((CONV_HALO, d), lambda i, j: (jnp.maximum(i * halo_blocks - 1, 0), 0)),
            _layer_rows(layer, d),
            pl.BlockSpec((None, d, tf), lambda i, j: (layer, 0, j)),
            pl.BlockSpec((None, d, tf), lambda i, j: (layer, 0, nj + j)),
            pl.BlockSpec((None, CONV_WIDTH, tf), lambda i, j: (layer, 0, j)),
            pl.BlockSpec((None, CONV_WIDTH, tf), lambda i, j: (layer, 0, nj + j)),
            pl.BlockSpec((None, 1, tf), lambda i, j: (layer, 0, j)),
            pl.BlockSpec((None, 1, tf), lambda i, j: (layer, 0, nj + j)),
            pl.BlockSpec((None, tf, d), lambda i, j: (layer, j, 0)),
            _layer_rows(layer, d),
        ],
        out_specs=pl.BlockSpec((tm, d), lambda i, j: (i, 0)),
        out_shape=jax.ShapeDtypeStruct((t, d), _F32),
        scratch_shapes=[pltpu.VMEM((CONV_HALO + tm, d), _BF16)],
        compiler_params=_params(("parallel", "arbitrary")),
        name="conv_ffn",
    )(h2d, h2d, gpre, w_up, w_up, conv_w, conv_w, conv_b, conv_b, w_down, gpost)


def _ple_kernel(h_ref, p_ref, wg_ref, wi_ref, o_ref):
    h = h_ref[...]
    gate = jax.nn.sigmoid(jnp.dot(h.astype(_BF16), wg_ref[...], preferred_element_type=_F32))
    e = jnp.dot(p_ref[...].astype(_BF16), wi_ref[...], preferred_element_type=_F32)
    o_ref[...] = h + gate * e


def _ple(h2d, p3d, w_gate, w_in, layer, *, tm):
    t, d = h2d.shape
    pd = p3d.shape[2]
    assert t % tm == 0
    resident = lambda w: pl.BlockSpec((None,) + w.shape[1:], lambda i: (layer, 0, 0), pipeline_mode=pl.Buffered(1))
    return pl.pallas_call(
        _ple_kernel,
        grid=(t // tm,),
        in_specs=[
            pl.BlockSpec((tm, d), lambda i: (i, 0)),
            pl.BlockSpec((None, tm, pd), lambda i: (layer, i, 0)),
            resident(w_gate),
            resident(w_in),
        ],
        out_specs=pl.BlockSpec((tm, d), lambda i: (i, 0)),
        out_shape=jax.ShapeDtypeStruct((t, d), _F32),
        compiler_params=_params(("parallel",)),
        name="layer_embedding",
    )(h2d, p3d, w_gate, w_in)


def _tiles(t, seq, d, d_in, d_ff):
    def largest(n, cands):
        return next(c for c in cands if n % c == 0)
    return dict(
        proj_tm=largest(t, (1024, 512, 256, 128)),
        proj_tn=largest(math.gcd(d_in, QKV_WIDTH), (2048, 1024, 512, 256, 128)),
        stick_tq=largest(seq, (256, 128)),
        merge_tm=largest(t, (512, 256, 128)),
        ffn_tm=largest(seq, (1024, 512, 256, 128)),
        ffn_tf=largest(d_ff, (512, 256, 128)),
        ple_tm=largest(t, (1024, 512, 256, 128)),
    )


def kernel(x, p, g_mix_pre, w_in, w_branch_a, w_branch_b, w_out, g_mix_post, g_ffn_pre, w_up, conv_w,
           conv_b, w_down, g_ffn_post, w_ple_in, w_ple_gate):
    b, s, d = x.shape
    depth = w_in.shape[0]
    d_in = w_in.shape[2]
    d_ff = w_down.shape[1]
    t = b * s
    assert d_in == QKV_WIDTH + 2 * d
    tiles = _tiles(t, s, d, d_in, d_ff)

    q_scale = 1.0 / math.sqrt(HEAD_DIM)
    col = jnp.arange(d_in)
    is_q = (col < WIDTH_A) | ((col >= 3 * WIDTH_A) & (col < 3 * WIDTH_A + WIDTH_B))
    colscale = jnp.where(is_q, q_scale, 1.0).astype(_F32)[None, :]

    bf16 = lambda w: w.astype(_BF16)
    rows = lambda v: v[:, None, :]
    w_in, w_branch_a, w_branch_b, w_out = bf16(w_in), bf16(w_branch_a), bf16(w_branch_b), bf16(w_out)
    w_up, w_down, w_ple_in, w_ple_gate = bf16(w_up), bf16(w_down), bf16(w_ple_in), bf16(w_ple_gate)
    p3d = p.reshape(depth, t, p.shape[-1])

    h = x.reshape(t, d)
    for i in range(depth):
        proj, far_qkv = _in_proj(h, rows(g_mix_pre), colscale, w_in, i, tm=tiles["proj_tm"], tn=tiles["proj_tn"])
        proj3 = proj.reshape(b, s, d_in)
        ya = _dilated_attention(proj3, far_qkv.reshape(b, s, 3 * A_OUT)).reshape(t, A_OUT)
        yb = _stick_attention(proj3, tq=tiles["stick_tq"], heads=N_HEADS_B).reshape(t, WIDTH_B)
        h = _merge(ya, yb, proj, h, w_branch_a, w_branch_b, w_out, rows(g_mix_post), i, tm=tiles["merge_tm"])
        h = _ffn(h, rows(g_ffn_pre), w_up, conv_w, rows(conv_b), w_down, rows(g_ffn_post), i,
                 tm=tiles["ffn_tm"], tf=tiles["ffn_tf"], seq=s)
        h = _ple(h, p3d, w_ple_gate, w_ple_in, i, tm=tiles["ple_tm"])
    return h.reshape(b, s, d)
```

```python
import functools
import math

import jax
import jax.numpy as jnp
from jax import lax
from jax.experimental import pallas as pl
from jax.experimental.pallas import tpu as pltpu

HEAD_DIM = 128
DIL_GROUPS = ((128, 1), (512, 4), (2048, 16))
NEAR_GROUPS = DIL_GROUPS[:-1]
FAR_RESIDUES_PER_STEP = 8
HEADS_PER_GROUP = 4
N_HEADS_A = HEADS_PER_GROUP * len(DIL_GROUPS)
N_HEADS_B = 4
WIDTH_A = N_HEADS_A * HEAD_DIM
WIDTH_B = N_HEADS_B * HEAD_DIM
A_OUT = HEADS_PER_GROUP * HEAD_DIM
QKV_WIDTH = 3 * WIDTH_A + 3 * WIDTH_B
CONV_WIDTH = 3
RMS_EPS = 1e-6
ALIBI_MAX = 8.0
MASKED = -1e30
EXP_UNDERFLOW = -110.0
QBLK = 128
SPAN = 2048
CONV_HALO = 16
VMEM_LIMIT = 56 * 1024 * 1024
IN_PROJ_VMEM_LIMIT = 60 * 1024 * 1024

_F32 = jnp.float32
_BF16 = jnp.bfloat16


def _rmsnorm_rows(x, g):
    ms = jnp.mean(x * x, axis=-1, keepdims=True)
    return x * lax.rsqrt(ms + RMS_EPS) * g


def _params(sem, vmem_limit=VMEM_LIMIT):
    return pltpu.CompilerParams(dimension_semantics=sem, vmem_limit_bytes=vmem_limit)


def _layer_rows(layer, width):
    return pl.BlockSpec((None, 1, width), lambda *_: (layer, 0, 0))


def _in_proj_kernel(x_ref, g_ref, cs_ref, w_ref, o_ref, far_ref, u_ref, *, gate_tile0, chunk, far_cols):
    j = pl.program_id(1)
    tm = x_ref.shape[0]
    tn = w_ref.shape[1]

    @pl.when(j == 0)
    def _():
        def body(c, carry):
            r0 = pl.multiple_of(c * chunk, chunk)
            x = x_ref[pl.ds(r0, chunk), :]
            u_ref[pl.ds(r0, chunk), :] = _rmsnorm_rows(x, g_ref[...]).astype(u_ref.dtype)
            return carry
        lax.fori_loop(0, tm // chunk, body, 0)

    acc = jnp.dot(u_ref[...], w_ref[...], preferred_element_type=_F32)

    o_ref[...] = jnp.where(j >= gate_tile0, jax.nn.sigmoid(acc), acc * cs_ref[...]).astype(o_ref.dtype)

    for col in far_cols:
        @pl.when(j == col // tn)
        def _(col=col):
            cols = slice(col % tn, col % tn + A_OUT)
            far_ref[...] = acc[:, cols] * cs_ref[:, cols]


def _in_proj(h2d, g, colscale, w, layer, *, tm, tn):
    t, d = h2d.shape
    d_in = w.shape[2]
    assert t % tm == 0 and d_in % tn == 0 and QKV_WIDTH % tn == 0 and tn % A_OUT == 0
    far_cols = tuple(part * WIDTH_A + (len(DIL_GROUPS) - 1) * A_OUT for part in range(3))
    far_tiles = [c // tn for c in far_cols]
    assert len(set(far_tiles)) == 3
    far_part = lambda j: (j >= far_tiles[1]).astype(jnp.int32) + (j >= far_tiles[2]).astype(jnp.int32)
    return pl.pallas_call(
        functools.partial(_in_proj_kernel, gate_tile0=QKV_WIDTH // tn, chunk=min(tm, 128), far_cols=far_cols),
        grid=(t // tm, d_in // tn),
        in_specs=[
            pl.BlockSpec((tm, d), lambda i, j: (i, 0)),
            _layer_rows(layer, d),
            pl.BlockSpec((1, tn), lambda i, j: (0, j)),
            pl.BlockSpec((None, d, tn), lambda i, j: (layer, 0, j)),
        ],
        out_specs=[pl.BlockSpec((tm, tn), lambda i, j: (i, j)),
                   pl.BlockSpec((tm, A_OUT), lambda i, j: (i, far_part(j)))],
        out_shape=[jax.ShapeDtypeStruct((t, d_in), _BF16), jax.ShapeDtypeStruct((t, 3 * A_OUT), _F32)],
        scratch_shapes=[pltpu.VMEM((tm, d), _BF16)],
        compiler_params=_params(("parallel", "arbitrary"), IN_PROJ_VMEM_LIMIT),
        name="in_proj",
    )(h2d, g, colscale, w)


def _dilated_kernel(q0, q1, kp0, kp1, kc0, kc1, vp0, vp1, vc0, vc1, b0, b1, far_out_ref, far_lse_ref, o_ref):
    i = pl.program_id(2)
    q_refs, bias_refs = (q0, q1), (b0, b1)
    k_refs, v_refs = ((kp0, kc0), (kp1, kc1)), ((vp0, vc0), (vp1, vc1))

    def window_rows(prev_cur, o, window):
        prev_ref, cur_ref = prev_cur
        n_prev = max(window - o, 0)
        cur = cur_ref[0, max(o - window, 0):o + QBLK, :]
        if n_prev == 0:
            return cur, 0
        return jnp.concatenate([prev_ref[0, SPAN - n_prev:SPAN, :], cur], axis=0), n_prev

    def masked_scores(o):
        scores = []
        for g, (window, _) in enumerate(NEAR_GROUPS):
            kk, n_prev = window_rows(k_refs[g], o, window)
            sc = lax.dot_general(q_refs[g][0, o:o + QBLK, :], kk, (((1,), (1,)), ((), ())),
                                 preferred_element_type=_F32)
            sc = sc + bias_refs[g][0]
            if n_prev:
                kj = lax.broadcasted_iota(jnp.int32, sc.shape, 1)
                sc = jnp.where(jnp.logical_or(kj >= n_prev, i > 0), sc, MASKED)
            scores.append(sc)
        return scores

    def joint_softmax_weights(scores, o):
        far_lse = far_lse_ref[0, o:o + QBLK, 0:1]
        m = functools.reduce(jnp.maximum, [jnp.max(sc, axis=-1, keepdims=True) for sc in scores] + [far_lse])
        ps = [jnp.exp(sc - m) for sc in scores]
        far_weight = jnp.exp(far_lse - m)
        denom = functools.reduce(jnp.add, [jnp.sum(p, axis=-1, keepdims=True) for p in ps] + [far_weight])
        return [p.astype(_BF16) for p in ps], far_weight, denom

    def weighted_values(ps, far_weight, o):
        acc = far_weight * far_out_ref[0, o:o + QBLK, :]
        for g, (window, _) in enumerate(NEAR_GROUPS):
            acc = acc + jnp.dot(ps[g], window_rows(v_refs[g], o, window)[0], preferred_element_type=_F32)
        return acc

    offs = range(0, SPAN, QBLK)
    staged = [masked_scores(o) for o in offs]
    weights = [joint_softmax_weights(scores, o) for o, scores in zip(offs, staged)]
    for o, (ps, far_weight, denom) in zip(offs, weights):
        o_ref[0, o:o + QBLK, :] = (weighted_values(ps, far_weight, o) / denom).astype(o_ref.dtype)


def _alibi_slopes(group):
    slopes = jnp.exp2(-ALIBI_MAX * jnp.arange(1, N_HEADS_A + 1, dtype=_F32) / N_HEADS_A)
    return slopes[group * HEADS_PER_GROUP:(group + 1) * HEADS_PER_GROUP]


def _dilated_bias_tables():
    tables = []
    for g, (window, dilation) in enumerate(NEAR_GROUPS):
        qi = jnp.arange(QBLK)[:, None]
        kj = jnp.arange(window + QBLK)[None, :]
        dist = qi + window - kj
        valid = (dist >= 0) & (dist <= window) & (dist % dilation == 0)
        bias = -_alibi_slopes(g)[:, None, None] * dist.astype(_F32)[None]
        tables.append(jnp.where(valid[None], bias, MASKED))
    return tables


def _far_bias_table():
    window, dilation = DIL_GROUPS[-1]
    qi = jnp.arange(QBLK)[:, None]
    kj = jnp.arange(2 * QBLK)[None, :]
    steps = QBLK + qi - kj
    valid = (steps >= 0) & (steps <= window // dilation)
    bias = -_alibi_slopes(len(DIL_GROUPS) - 1)[:, None, None] * (steps * dilation).astype(_F32)[None]
    return jnp.where(valid[None], bias, MASKED)


def _far_kernel(q_ref, kp_ref, kc_ref, vp_ref, vc_ref, bias_ref, out_ref, lse_ref):
    n = pl.program_id(1)
    kj = lax.broadcasted_iota(jnp.int32, (QBLK, 2 * QBLK), 1)
    in_sequence = kj >= jnp.where(n == 0, QBLK, 0)
    residue_major = lambda ref: pltpu.einshape("mrd->rmd", ref[0]).astype(_BF16)
    q_all = residue_major(q_ref)
    k_all = jnp.concatenate([residue_major(kp_ref), residue_major(kc_ref)], axis=1)
    v_all = jnp.concatenate([residue_major(vp_ref), residue_major(vc_ref)], axis=1)
    outs, lses = [], []
    for r in range(FAR_RESIDUES_PER_STEP):
        lanes = [slice(h * HEAD_DIM, (h + 1) * HEAD_DIM) for h in range(HEADS_PER_GROUP)]
        scores = []
        for h in range(HEADS_PER_GROUP):
            sc = lax.dot_general(q_all[r][:, lanes[h]], k_all[r][:, lanes[h]], (((1,), (1,)), ((), ())),
                                 preferred_element_type=_F32) + bias_ref[h]
            scores.append(jnp.where(in_sequence, sc, MASKED))
        out_r, lse_r = [], []
        for h in range(HEADS_PER_GROUP):
            m = jnp.max(scores[h], axis=-1, keepdims=True)
            p = jnp.exp(scores[h] - m)
            denom = jnp.sum(p, axis=-1, keepdims=True)
            acc = jnp.dot(p.astype(_BF16), v_all[r][:, lanes[h]], preferred_element_type=_F32)
            out_r.append(acc / denom)
            lse_r.append(jnp.broadcast_to(m + jnp.log(denom), (QBLK, HEAD_DIM)))
        outs.append(jnp.concatenate(out_r, axis=1))
        lses.append(jnp.concatenate(lse_r, axis=1))
    out_ref[0] = pltpu.einshape("rmd->mrd", jnp.stack(outs))
    lse_ref[0] = pltpu.einshape("rmd->mrd", jnp.stack(lses))


def _far_attention(far_qkv, bias):
    b, s, _ = far_qkv.shape
    dilation = DIL_GROUPS[-1][1]
    positions = s // dilation
    assert s % (dilation * QBLK) == 0 and dilation % FAR_RESIDUES_PER_STEP == 0
    view = far_qkv.reshape(b, positions, dilation, 3 * A_OUT)
    blk = (1, QBLK, FAR_RESIDUES_PER_STEP, A_OUT)
    cur = lambda part: pl.BlockSpec(blk, lambda bi, n, a: (bi, n, a, part))
    prev = lambda part: pl.BlockSpec(blk, lambda bi, n, a: (bi, jnp.maximum(n - 1, 0), a, part))
    out_spec = pl.BlockSpec(blk, lambda bi, n, a: (bi, n, a, 0))
    shape = jax.ShapeDtypeStruct((b, positions, dilation, A_OUT), _F32)
    out, lse = pl.pallas_call(
        _far_kernel,
        grid=(b, positions // QBLK, dilation // FAR_RESIDUES_PER_STEP),
        in_specs=[cur(0), prev(1), cur(1), prev(2), cur(2),
                  pl.BlockSpec(bias.shape, lambda bi, n, a: (0, 0, 0))],
        out_specs=[out_spec, out_spec],
        out_shape=[shape, shape],
        compiler_params=_params(("parallel", "parallel", "parallel")),
        name="far_dilated_attention",
    )(view, view, view, view, view, bias)
    return out.reshape(b, s, A_OUT), lse.reshape(b, s, A_OUT)


def _dilated_attention(proj3, far_qkv):
    b, s, _ = proj3.shape
    assert s % SPAN == 0
    n_heads = N_HEADS_A
    tables = _dilated_bias_tables()
    far_out, far_lse = _far_attention(far_qkv, _far_bias_table())

    def col_spec(base, lag):
        return [pl.BlockSpec((1, SPAN, HEAD_DIM), (lambda bi, hh, i, c=base + g * HEADS_PER_GROUP:
                                                   (bi, jnp.maximum(i - lag, 0), c + hh)))
                for g in range(len(NEAR_GROUPS))]

    bias_specs = [pl.BlockSpec((1, QBLK, w + QBLK), lambda bi, hh, i: (hh, 0, 0)) for w, _ in NEAR_GROUPS]
    far_spec = pl.BlockSpec((1, SPAN, HEAD_DIM), lambda bi, hh, i: (bi, i, hh))
    qkv_specs = (col_spec(0, 0) + col_spec(n_heads, 1) + col_spec(n_heads, 0)
                 + col_spec(2 * n_heads, 1) + col_spec(2 * n_heads, 0))
    return pl.pallas_call(
        _dilated_kernel,
        grid=(b, HEADS_PER_GROUP, s // SPAN),
        in_specs=qkv_specs + bias_specs + [far_spec, far_spec],
        out_specs=pl.BlockSpec((1, SPAN, HEAD_DIM), lambda bi, hh, i: (bi, i, hh)),
        out_shape=jax.ShapeDtypeStruct((b, s, A_OUT), _BF16),
        compiler_params=_params(("parallel", "parallel", "parallel")),
        name="dilated_attention",
    )(*([proj3] * (5 * len(NEAR_GROUPS))), *tables, far_out, far_lse)


def _stick_kernel(q_ref, k_ref, v_ref, tri_ref, o_ref, *, tq, heads):
    i = pl.program_id(2)
    row = lax.broadcasted_iota(jnp.int32, (tq, tq), 0)
    col = lax.broadcasted_iota(jnp.int32, (tq, tq), 1)
    before = col < row

    def chunk(c, afters, accs, diagonal):
        k0 = pl.multiple_of(c * tq, tq)
        tri = tri_ref[...]
        lanes = [slice(h * HEAD_DIM, (h + 1) * HEAD_DIM) for h in range(heads)]
        zs = [lax.dot_general(q_ref[0, :, lanes[h]], k_ref[0, pl.ds(k0, tq), lanes[h]],
                              (((1,), (1,)), ((), ())), preferred_element_type=_F32) for h in range(heads)]
        csums = []
        for h in range(heads):
            z = zs[h]
            ln = -(jnp.maximum(z, 0.0) + jnp.log(1.0 + jnp.exp(-jnp.abs(z))))
            if diagonal:
                ln = jnp.where(before, ln, 0.0)
            hi = ln.astype(_BF16)
            lo = (ln - hi.astype(_F32)).astype(_BF16)
            csums.append(jnp.dot(hi, tri, preferred_element_type=_F32)
                         + jnp.dot(lo, tri, preferred_element_type=_F32))
        new_afters, new_accs = [], []
        for h in range(heads):
            w = jnp.exp(zs[h] + csums[h] + afters[h])
            if diagonal:
                w = jnp.where(before, w, 0.0)
            new_accs.append(accs[h] + jnp.dot(w.astype(_BF16), v_ref[0, pl.ds(k0, tq), lanes[h]],
                                              preferred_element_type=_F32))
            new_afters.append(afters[h] + csums[h][:, 0:1])
        return tuple(new_afters), tuple(new_accs)

    afters, accs = chunk(i, (jnp.zeros((tq, 1), _F32),) * heads, (jnp.zeros((tq, HEAD_DIM), _F32),) * heads, True)

    def keep_going(carry):
        c, afters, _ = carry
        return jnp.logical_and(c >= 0, jnp.max(functools.reduce(jnp.maximum, afters)) > EXP_UNDERFLOW)

    def body(carry):
        c, afters, accs = carry
        afters, accs = chunk(c, afters, accs, False)
        return c - 1, afters, accs

    _, _, accs = lax.while_loop(keep_going, body, (i - 1, afters, accs))
    for h in range(heads):
        o_ref[0, :, h * HEAD_DIM:(h + 1) * HEAD_DIM] = accs[h].astype(o_ref.dtype)


def _stick_attention(proj3, *, tq, heads):
    b, s, _ = proj3.shape
    width = heads * HEAD_DIM
    assert s % tq == 0 and N_HEADS_B % heads == 0 and (3 * WIDTH_A) % width == 0
    groups = N_HEADS_B // heads
    base = 3 * WIDTH_A // width
    idx = jnp.arange(tq)
    tri = (idx[:, None] >= idx[None, :]).astype(_BF16)
    whole_seq = lambda part: pl.BlockSpec((1, s, width), lambda bi, hg, i: (bi, 0, base + part * groups + hg),
                                          pipeline_mode=pl.Buffered(1))
    return pl.pallas_call(
        functools.partial(_stick_kernel, tq=tq, heads=heads),
        grid=(b, groups, s // tq),
        in_specs=[
            pl.BlockSpec((1, tq, width), lambda bi, hg, i: (bi, i, base + hg)),
            whole_seq(1),
            whole_seq(2),
            pl.BlockSpec((tq, tq), lambda bi, hg, i: (0, 0)),
        ],
        out_specs=pl.BlockSpec((1, tq, width), lambda bi, hg, i: (bi, i, hg)),
        out_shape=jax.ShapeDtypeStruct((b, s, WIDTH_B), _BF16),
        compiler_params=_params(("parallel", "parallel", "arbitrary")),
        name="stick_breaking_attention",
    )(proj3, proj3, proj3, tri)


def _merge_kernel(ya_ref, yb_ref, ga_ref, gb_ref, h_ref, wa_ref, wb_ref, wo_ref, g_ref, o_ref):
    a = jnp.dot(ya_ref[...], wa_ref[...], preferred_element_type=_F32)
    bb = jnp.dot(yb_ref[...], wb_ref[...], preferred_element_type=_F32)
    m = (ga_ref[...].astype(_F32) * a + gb_ref[...].astype(_F32) * bb).astype(_BF16)
    mix = jnp.dot(m, wo_ref[...], preferred_element_type=_F32)
    o_ref[...] = h_ref[...] + _rmsnorm_rows(mix, g_ref[...])


def _merge(ya, yb, proj, h2d, wa, wb, wo, g, layer, *, tm):
    t, d = h2d.shape
    assert t % tm == 0 and QKV_WIDTH % d == 0
    gate_blk = QKV_WIDTH // d
    resident = lambda w: pl.BlockSpec((None,) + w.shape[1:], lambda i: (layer, 0, 0), pipeline_mode=pl.Buffered(1))
    return pl.pallas_call(
        _merge_kernel,
        grid=(t // tm,),
        in_specs=[
            pl.BlockSpec((tm, A_OUT), lambda i: (i, 0)),
            pl.BlockSpec((tm, WIDTH_B), lambda i: (i, 0)),
            pl.BlockSpec((tm, d), lambda i: (i, gate_blk)),
            pl.BlockSpec((tm, d), lambda i: (i, gate_blk + 1)),
            pl.BlockSpec((tm, d), lambda i: (i, 0)),
            resident(wa), resident(wb), resident(wo), _layer_rows(layer, d),
        ],
        out_specs=pl.BlockSpec((tm, d), lambda i: (i, 0)),
        out_shape=jax.ShapeDtypeStruct((t, d), _F32),
        compiler_params=_params(("parallel",)),
        name="gated_merge",
    )(ya, yb, proj, proj, h2d, wa, wb, wo, g)


def _ffn_kernel(h_ref, hp_ref, gpre_ref, wg_ref, wv_ref, cwg_ref, cwv_ref, cbg_ref, cbv_ref, wd_ref,
                gpost_ref, o_ref, u_ref, *, tiles_per_seq, chunk):
    i = pl.program_id(0)
    j = pl.program_id(1)
    tm = h_ref.shape[0]
    n_chunks = tm // chunk

    @pl.when(j == 0)
    def _():
        halo = _rmsnorm_rows(hp_ref[...], gpre_ref[...])
        at_sequence_start = (i % tiles_per_seq) == 0
        u_ref[0:CONV_HALO, :] = jnp.where(at_sequence_start, 0.0, halo).astype(u_ref.dtype)
        o_ref[...] = jnp.zeros(o_ref.shape, o_ref.dtype)

        def body(c, carry):
            r0 = pl.multiple_of(c * chunk, chunk)
            x = h_ref[pl.ds(r0, chunk), :]
            u_ref[pl.ds(CONV_HALO + r0, chunk), :] = _rmsnorm_rows(x, gpre_ref[...]).astype(u_ref.dtype)
            return carry
        lax.fori_loop(0, n_chunks, body, 0)

    u = u_ref[...]

    def conv_branch(w_ref, cw_ref, cb_ref):
        x = jnp.dot(u, w_ref[...], preferred_element_type=_F32)
        cw = cw_ref[...]
        y = cb_ref[...] + cw[0:1, :] * pltpu.roll(x, 2, 0)
        y = y + cw[1:2, :] * pltpu.roll(x, 1, 0)
        y = y + cw[2:3, :] * x
        return y[CONV_HALO:, :]

    gate = conv_branch(wg_ref, cwg_ref, cbg_ref)
    value = conv_branch(wv_ref, cwv_ref, cbv_ref)
    act = (jax.nn.gelu(gate, approximate=True) * value).astype(_BF16)
    o_ref[...] += jnp.dot(act, wd_ref[...], preferred_element_type=_F32)

    @pl.when(j == pl.num_programs(1) - 1)
    def _():
        def body(c, carry):
            r0 = pl.multiple_of(c * chunk, chunk)
            f = o_ref[pl.ds(r0, chunk), :]
            o_ref[pl.ds(r0, chunk), :] = h_ref[pl.ds(r0, chunk), :] + _rmsnorm_rows(f, gpost_ref[...])
            return carry
        lax.fori_loop(0, n_chunks, body, 0)


def _ffn(h2d, gpre, w_up, conv_w, conv_b, w_down, gpost, layer, *, tm, tf, seq):
    t, d = h2d.shape
    d_ff = w_down.shape[1]
    assert t % tm == 0 and seq % tm == 0 and d_ff % tf == 0 and tm % CONV_HALO == 0
    nj = d_ff // tf
    halo_blocks = tm // CONV_HALO
    return pl.pallas_call(
        functools.partial(_ffn_kernel, tiles_per_seq=seq // tm, chunk=min(tm, 128)),
        grid=(t // tm, nj),
        in_specs=[
            pl.BlockSpec((tm, d), lambda i, j: (i, 0)),
            pl.BlockSpec((CONV_HALO, d), lambda i, j: (jnp.maximum(i * halo_blocks - 1, 0), 0)),
            _layer_rows(layer, d),
            pl.BlockSpec((None, d, tf), lambda i, j: (layer, 0, j)),
            pl.BlockSpec((None, d, tf), lambda i, j: (layer, 0, nj + j)),
            pl.BlockSpec((None, CONV_WIDTH, tf), lambda i, j: (layer, 0, j)),
            pl.BlockSpec((None, CONV_WIDTH, tf), lambda i, j: (layer, 0, nj + j)),
            pl.BlockSpec((None, 1, tf), lambda i, j: (layer, 0, j)),
            pl.BlockSpec((None, 1, tf), lambda i, j: (layer, 0, nj + j)),
            pl.BlockSpec((None, tf, d), lambda i, j: (layer, j, 0)),
            _layer_rows(layer, d),
        ],
        out_specs=pl.BlockSpec((tm, d), lambda i, j: (i, 0)),
        out_shape=jax.ShapeDtypeStruct((t, d), _F32),
        scratch_shapes=[pltpu.VMEM((CONV_HALO + tm, d), _BF16)],
        compiler_params=_params(("parallel", "arbitrary")),
        name="conv_ffn",
    )(h2d, h2d, gpre, w_up, w_up, conv_w, conv_w, conv_b, conv_b, w_down, gpost)


def _ple_kernel(h_ref, p_ref, wg_ref, wi_ref, o_ref):
    h = h_ref[...]
    gate = jax.nn.sigmoid(jnp.dot(h.astype(_BF16), wg_ref[...], preferred_element_type=_F32))
    e = jnp.dot(p_ref[...].astype(_BF16), wi_ref[...], preferred_element_type=_F32)
    o_ref[...] = h + gate * e


def _ple(h2d, p3d, w_gate, w_in, layer, *, tm):
    t, d = h2d.shape
    pd = p3d.shape[2]
    assert t % tm == 0
    resident = lambda w: pl.BlockSpec((None,) + w.shape[1:], lambda i: (layer, 0, 0), pipeline_mode=pl.Buffered(1))
    return pl.pallas_call(
        _ple_kernel,
        grid=(t // tm,),
        in_specs=[
            pl.BlockSpec((tm, d), lambda i: (i, 0)),
            pl.BlockSpec((None, tm, pd), lambda i: (layer, i, 0)),
            resident(w_gate),
            resident(w_in),
        ],
        out_specs=pl.BlockSpec((tm, d), lambda i: (i, 0)),
        out_shape=jax.ShapeDtypeStruct((t, d), _F32),
        compiler_params=_params(("parallel",)),
        name="layer_embedding",
    )(h2d, p3d, w_gate, w_in)


def _tiles(t, seq, d, d_in, d_ff):
    def largest(n, cands):
        return next(c for c in cands if n % c == 0)
    return dict(
        proj_tm=largest(t, (1024, 512, 256, 128)),
        proj_tn=largest(math.gcd(d_in, QKV_WIDTH), (2048, 1024, 512, 256, 128)),
        stick_tq=largest(seq, (256, 128)),
        merge_tm=largest(t, (512, 256, 128)),
        ffn_tm=largest(seq, (1024, 512, 256, 128)),
        ffn_tf=largest(d_ff, (512, 256, 128)),
        ple_tm=largest(t, (1024, 512, 256, 128)),
    )


def kernel(x, p, g_mix_pre, w_in, w_branch_a, w_branch_b, w_out, g_mix_post, g_ffn_pre, w_up, conv_w,
           conv_b, w_down, g_ffn_post, w_ple_in, w_ple_gate):
    b, s, d = x.shape
    depth = w_in.shape[0]
    d_in = w_in.shape[2]
    d_ff = w_down.shape[1]
    t = b * s
    assert d_in == QKV_WIDTH + 2 * d
    tiles = _tiles(t, s, d, d_in, d_ff)

    q_scale = 1.0 / math.sqrt(HEAD_DIM)
    col = jnp.arange(d_in)
    is_q = (col < WIDTH_A) | ((col >= 3 * WIDTH_A) & (col < 3 * WIDTH_A + WIDTH_B))
    colscale = jnp.where(is_q, q_scale, 1.0).astype(_F32)[None, :]

    bf16 = lambda w: w.astype(_BF16)
    rows = lambda v: v[:, None, :]
    w_in, w_branch_a, w_branch_b, w_out = bf16(w_in), bf16(w_branch_a), bf16(w_branch_b), bf16(w_out)
    w_up, w_down, w_ple_in, w_ple_gate = bf16(w_up), bf16(w_down), bf16(w_ple_in), bf16(w_ple_gate)
    p3d = p.reshape(depth, t, p.shape[-1])

    h = x.reshape(t, d)
    for i in range(depth):
        proj, far_qkv = _in_proj(h, rows(g_mix_pre), colscale, w_in, i, tm=tiles["proj_tm"], tn=tiles["proj_tn"])
        proj3 = proj.reshape(b, s, d_in)
        ya = _dilated_attention(proj3, far_qkv.reshape(b, s, 3 * A_OUT)).reshape(t, A_OUT)
        yb = _stick_attention(proj3, tq=tiles["stick_tq"], heads=N_HEADS_B).reshape(t, WIDTH_B)
        h = _merge(ya, yb, proj, h, w_branch_a, w_branch_b, w_out, rows(g_mix_post), i, tm=tiles["merge_tm"])
        h = _ffn(h, rows(g_ffn_pre), w_up, conv_w, rows(conv_b), w_down, rows(g_ffn_post), i,
                 tm=tiles["ffn_tm"], tf=tiles["ffn_tf"], seq=s)
        h = _ple(h, p3d, w_ple_gate, w_ple_in, i, tm=tiles["ple_tm"])
    return h.reshape(b, s, d)
```

```python
import functools
import math

import jax
import jax.numpy as jnp
from jax import lax
from jax.experimental import pallas as pl
from jax.experimental.pallas import tpu as pltpu

HEAD_DIM = 128
DIL_GROUPS = ((128, 1), (512, 4), (2048, 16))
NEAR_GROUPS = DIL_GROUPS[:-1]
FAR_RESIDUES_PER_STEP = 8
HEADS_PER_GROUP = 4
N_HEADS_A = HEADS_PER_GROUP * len(DIL_GROUPS)
N_HEADS_B = 4
WIDTH_A = N_HEADS_A * HEAD_DIM
WIDTH_B = N_HEADS_B * HEAD_DIM
A_OUT = HEADS_PER_GROUP * HEAD_DIM
QKV_WIDTH = 3 * WIDTH_A + 3 * WIDTH_B
CONV_WIDTH = 3
RMS_EPS = 1e-6
ALIBI_MAX = 8.0
MASKED = -1e30
EXP_UNDERFLOW = -110.0
QBLK = 128
SPAN = 2048
CONV_HALO = 16
VMEM_LIMIT = 56 * 1024 * 1024
IN_PROJ_VMEM_LIMIT = 60 * 1024 * 1024

_F32 = jnp.float32
_BF16 = jnp.bfloat16


def _rmsnorm_rows(x, g):
    ms = jnp.mean(x * x, axis=-1, keepdims=True)
    return x * lax.rsqrt(ms + RMS_EPS) * g


def _params(sem, vmem_limit=VMEM_LIMIT):
    return pltpu.CompilerParams(dimension_semantics=sem, vmem_limit_bytes=vmem_limit)


def _layer_rows(layer, width):
    return pl.BlockSpec((None, 1, width), lambda *_: (layer, 0, 0))


def _in_proj_kernel(x_ref, g_ref, cs_ref, w_ref, o_ref, far_ref, u_ref, *, gate_tile0, chunk, far_cols):
    j = pl.program_id(1)
    tm = x_ref.shape[0]
    tn = w_ref.shape[1]

    @pl.when(j == 0)
    def _():
        def body(c, carry):
            r0 = pl.multiple_of(c * chunk, chunk)
            x = x_ref[pl.ds(r0, chunk), :]
            u_ref[pl.ds(r0, chunk), :] = _rmsnorm_rows(x, g_ref[...]).astype(u_ref.dtype)
            return carry
        lax.fori_loop(0, tm // chunk, body, 0)

    acc = jnp.dot(u_ref[...], w_ref[...], preferred_element_type=_F32)

    o_ref[...] = jnp.where(j >= gate_tile0, jax.nn.sigmoid(acc), acc * cs_ref[...]).astype(o_ref.dtype)

    for col in far_cols:
        @pl.when(j == col // tn)
        def _(col=col):
            cols = slice(col % tn, col % tn + A_OUT)
            far_ref[...] = acc[:, cols] * cs_ref[:, cols]


def _in_proj(h2d, g, colscale, w, layer, *, tm, tn):
    t, d = h2d.shape
    d_in = w.shape[2]
    assert t % tm == 0 and d_in % tn == 0 and QKV_WIDTH % tn == 0 and tn % A_OUT == 0
    far_cols = tuple(part * WIDTH_A + (len(DIL_GROUPS) - 1) * A_OUT for part in range(3))
    far_tiles = [c // tn for c in far_cols]
    assert len(set(far_tiles)) == 3
    far_part = lambda j: (j >= far_tiles[1]).astype(jnp.int32) + (j >= far_tiles[2]).astype(jnp.int32)
    return pl.pallas_call(
        functools.partial(_in_proj_kernel, gate_tile0=QKV_WIDTH // tn, chunk=min(tm, 128), far_cols=far_cols),
        grid=(t // tm, d_in // tn),
        in_specs=[
            pl.BlockSpec((tm, d), lambda i, j: (i, 0)),
            _layer_rows(layer, d),
            pl.BlockSpec((1, tn), lambda i, j: (0, j)),
            pl.BlockSpec((None, d, tn), lambda i, j: (layer, 0, j)),
        ],
        out_specs=[pl.BlockSpec((tm, tn), lambda i, j: (i, j)),
                   pl.BlockSpec((tm, A_OUT), lambda i, j: (i, far_part(j)))],
        out_shape=[jax.ShapeDtypeStruct((t, d_in), _BF16), jax.ShapeDtypeStruct((t, 3 * A_OUT), _F32)],
        scratch_shapes=[pltpu.VMEM((tm, d), _BF16)],
        compiler_params=_params(("parallel", "arbitrary"), IN_PROJ_VMEM_LIMIT),
        name="in_proj",
    )(h2d, g, colscale, w)


def _near_attention_stages(i, q0, q1, kp0, kp1, kc0, kc1, vp0, vp1, vc0, vc1, b0, b1, far_out_ref, far_lse_ref):
    q_refs, bias_refs = (q0, q1), (b0, b1)
    k_refs, v_refs = ((kp0, kc0), (kp1, kc1)), ((vp0, vc0), (vp1, vc1))

    def window_rows(prev_cur, o, window):
        prev_ref, cur_ref = prev_cur
        n_prev = max(window - o, 0)
        cur = cur_ref[0, max(o - window, 0):o + QBLK, :]
        if n_prev == 0:
            return cur, 0
        return jnp.concatenate([prev_ref[0, SPAN - n_prev:SPAN, :], cur], axis=0), n_prev

    def masked_scores(o):
        scores = []
        for g, (window, _) in enumerate(NEAR_GROUPS):
            kk, n_prev = window_rows(k_refs[g], o, window)
            sc = lax.dot_general(q_refs[g][0, o:o + QBLK, :], kk, (((1,), (1,)), ((), ())),
                                 preferred_element_type=_F32)
            sc = sc + bias_refs[g][0]
            if n_prev:
                kj = lax.broadcasted_iota(jnp.int32, sc.shape, 1)
                sc = jnp.where(jnp.logical_or(kj >= n_prev, i > 0), sc, MASKED)
            scores.append(sc)
        return scores

    def joint_softmax_weights(scores, o):
        far_lse = far_lse_ref[0, o:o + QBLK, 0:1]
        m = functools.reduce(jnp.maximum, [jnp.max(sc, axis=-1, keepdims=True) for sc in scores] + [far_lse])
        ps = [jnp.exp(sc - m) for sc in scores]
        far_weight = jnp.exp(far_lse - m)
        denom = functools.reduce(jnp.add, [jnp.sum(p, axis=-1, keepdims=True) for p in ps] + [far_weight])
        return [p.astype(_BF16) for p in ps], far_weight, denom

    def output_rows(weights, o):
        ps, far_weight, denom = weights
        acc = far_weight * far_out_ref[0, o:o + QBLK, :]
        for g, (window, _) in enumerate(NEAR_GROUPS):
            acc = acc + jnp.dot(ps[g], window_rows(v_refs[g], o, window)[0], preferred_element_type=_F32)
        return (acc / denom).astype(_BF16)

    return masked_scores, joint_softmax_weights, output_rows


def _alibi_slopes(group):
    slopes = jnp.exp2(-ALIBI_MAX * jnp.arange(1, N_HEADS_A + 1, dtype=_F32) / N_HEADS_A)
    return slopes[group * HEADS_PER_GROUP:(group + 1) * HEADS_PER_GROUP]


def _dilated_bias_tables():
    tables = []
    for g, (window, dilation) in enumerate(NEAR_GROUPS):
        qi = jnp.arange(QBLK)[:, None]
        kj = jnp.arange(window + QBLK)[None, :]
        dist = qi + window - kj
        valid = (dist >= 0) & (dist <= window) & (dist % dilation == 0)
        bias = -_alibi_slopes(g)[:, None, None] * dist.astype(_F32)[None]
        tables.append(jnp.where(valid[None], bias, MASKED))
    return tables


def _far_bias_table():
    window, dilation = DIL_GROUPS[-1]
    qi = jnp.arange(QBLK)[:, None]
    kj = jnp.arange(2 * QBLK)[None, :]
    steps = QBLK + qi - kj
    valid = (steps >= 0) & (steps <= window // dilation)
    bias = -_alibi_slopes(len(DIL_GROUPS) - 1)[:, None, None] * (steps * dilation).astype(_F32)[None]
    return jnp.where(valid[None], bias, MASKED)


def _far_kernel(q_ref, kp_ref, kc_ref, vp_ref, vc_ref, bias_ref, out_ref, lse_ref):
    n = pl.program_id(1)
    kj = lax.broadcasted_iota(jnp.int32, (QBLK, 2 * QBLK), 1)
    in_sequence = kj >= jnp.where(n == 0, QBLK, 0)
    residue_major = lambda ref: pltpu.einshape("mrd->rmd", ref[0]).astype(_BF16)
    q_all = residue_major(q_ref)
    k_all = jnp.concatenate([residue_major(kp_ref), residue_major(kc_ref)], axis=1)
    v_all = jnp.concatenate([residue_major(vp_ref), residue_major(vc_ref)], axis=1)
    outs, lses = [], []
    for r in range(FAR_RESIDUES_PER_STEP):
        lanes = [slice(h * HEAD_DIM, (h + 1) * HEAD_DIM) for h in range(HEADS_PER_GROUP)]
        scores = []
        for h in range(HEADS_PER_GROUP):
            sc = lax.dot_general(q_all[r][:, lanes[h]], k_all[r][:, lanes[h]], (((1,), (1,)), ((), ())),
                                 preferred_element_type=_F32) + bias_ref[h]
            scores.append(jnp.where(in_sequence, sc, MASKED))
        out_r, lse_r = [], []
        for h in range(HEADS_PER_GROUP):
            m = jnp.max(scores[h], axis=-1, keepdims=True)
            p = jnp.exp(scores[h] - m)
            denom = jnp.sum(p, axis=-1, keepdims=True)
            acc = jnp.dot(p.astype(_BF16), v_all[r][:, lanes[h]], preferred_element_type=_F32)
            out_r.append(acc / denom)
            lse_r.append(jnp.broadcast_to(m + jnp.log(denom), (QBLK, HEAD_DIM)))
        outs.append(jnp.concatenate(out_r, axis=1))
        lses.append(jnp.concatenate(lse_r, axis=1))
    out_ref[0] = pltpu.einshape("rmd->mrd", jnp.stack(outs))
    lse_ref[0] = pltpu.einshape("rmd->mrd", jnp.stack(lses))


def _far_attention(far_qkv, bias):
    b, s, _ = far_qkv.shape
    dilation = DIL_GROUPS[-1][1]
    positions = s // dilation
    assert s % (dilation * QBLK) == 0 and dilation % FAR_RESIDUES_PER_STEP == 0
    view = far_qkv.reshape(b, positions, dilation, 3 * A_OUT)
    blk = (1, QBLK, FAR_RESIDUES_PER_STEP, A_OUT)
    cur = lambda part: pl.BlockSpec(blk, lambda bi, n, a: (bi, n, a, part))
    prev = lambda part: pl.BlockSpec(blk, lambda bi, n, a: (bi, jnp.maximum(n - 1, 0), a, part))
    out_spec = pl.BlockSpec(blk, lambda bi, n, a: (bi, n, a, 0))
    shape = jax.ShapeDtypeStruct((b, positions, dilation, A_OUT), _F32)
    out, lse = pl.pallas_call(
        _far_kernel,
        grid=(b, positions // QBLK, dilation // FAR_RESIDUES_PER_STEP),
        in_specs=[cur(0), prev(1), cur(1), prev(2), cur(2),
                  pl.BlockSpec(bias.shape, lambda bi, n, a: (0, 0, 0))],
        out_specs=[out_spec, out_spec],
        out_shape=[shape, shape],
        compiler_params=_params(("parallel", "parallel", "parallel")),
        name="far_dilated_attention",
    )(view, view, view, view, view, bias)
    return out.reshape(b, s, A_OUT), lse.reshape(b, s, A_OUT)


def _stick_kernel(q_ref, k_ref, v_ref, tri_ref, o_ref, *, tq, heads):
    i = pl.program_id(2)
    row = lax.broadcasted_iota(jnp.int32, (tq, tq), 0)
    col = lax.broadcasted_iota(jnp.int32, (tq, tq), 1)
    before = col < row

    def chunk(c, afters, accs, diagonal):
        k0 = pl.multiple_of(c * tq, tq)
        tri = tri_ref[...]
        lanes = [slice(h * HEAD_DIM, (h + 1) * HEAD_DIM) for h in range(heads)]
        zs = [lax.dot_general(q_ref[0, :, lanes[h]], k_ref[0, pl.ds(k0, tq), lanes[h]],
                              (((1,), (1,)), ((), ())), preferred_element_type=_F32) for h in range(heads)]
        csums = []
        for h in range(heads):
            z = zs[h]
            ln = -(jnp.maximum(z, 0.0) + jnp.log(1.0 + jnp.exp(-jnp.abs(z))))
            if diagonal:
                ln = jnp.where(before, ln, 0.0)
            hi = ln.astype(_BF16)
            lo = (ln - hi.astype(_F32)).astype(_BF16)
            csums.append(jnp.dot(hi, tri, preferred_element_type=_F32)
                         + jnp.dot(lo, tri, preferred_element_type=_F32))
        new_afters, new_accs = [], []
        for h in range(heads):
            w = jnp.exp(zs[h] + csums[h] + afters[h])
            if diagonal:
                w = jnp.where(before, w, 0.0)
            new_accs.append(accs[h] + jnp.dot(w.astype(_BF16), v_ref[0, pl.ds(k0, tq), lanes[h]],
                                              preferred_element_type=_F32))
            new_afters.append(afters[h] + csums[h][:, 0:1])
        return tuple(new_afters), tuple(new_accs)

    afters, accs = chunk(i, (jnp.zeros((tq, 1), _F32),) * heads, (jnp.zeros((tq, HEAD_DIM), _F32),) * heads, True)

    def keep_going(carry):
        c, afters, _ = carry
        return jnp.logical_and(c >= 0, jnp.max(functools.reduce(jnp.maximum, afters)) > EXP_UNDERFLOW)

    def body(carry):
        c, afters, accs = carry
        afters, accs = chunk(c, afters, accs, False)
        return c - 1, afters, accs

    _, _, accs = lax.while_loop(keep_going, body, (i - 1, afters, accs))
    for h in range(heads):
        o_ref[0, :, h * HEAD_DIM:(h + 1) * HEAD_DIM] = accs[h].astype(o_ref.dtype)


def _stick_attention(proj3, *, tq, heads):
    b, s, _ = proj3.shape
    width = heads * HEAD_DIM
    assert s % tq == 0 and N_HEADS_B % heads == 0 and (3 * WIDTH_A) % width == 0
    groups = N_HEADS_B // heads
    base = 3 * WIDTH_A // width
    idx = jnp.arange(tq)
    tri = (idx[:, None] >= idx[None, :]).astype(_BF16)
    whole_seq = lambda part: pl.BlockSpec((1, s, width), lambda bi, hg, i: (bi, 0, base + part * groups + hg),
                                          pipeline_mode=pl.Buffered(1))
    return pl.pallas_call(
        functools.partial(_stick_kernel, tq=tq, heads=heads),
        grid=(b, groups, s // tq),
        in_specs=[
            pl.BlockSpec((1, tq, width), lambda bi, hg, i: (bi, i, base + hg)),
            whole_seq(1),
            whole_seq(2),
            pl.BlockSpec((tq, tq), lambda bi, hg, i: (0, 0)),
        ],
        out_specs=pl.BlockSpec((1, tq, width), lambda bi, hg, i: (bi, i, hg)),
        out_shape=jax.ShapeDtypeStruct((b, s, WIDTH_B), _BF16),
        compiler_params=_params(("parallel", "parallel", "arbitrary")),
        name="stick_breaking_attention",
    )(proj3, proj3, proj3, tri)


def _attend_merge_kernel(q0, q1, kp0, kp1, kc0, kc1, vp0, vp1, vc0, vc1, b0, b1, far_out_ref, far_lse_ref,
                         yb_ref, ga_ref, gb_ref, h_ref, wa_ref, wb_ref, wo_ref, g_ref, o_ref, ya_ref,
                         *, n_spans, spans_per_seq):
    g = pl.program_id(0)
    st = pl.program_id(1)
    head_slot = st // 2
    span_in_seq = jnp.minimum(g, n_spans - 1) % spans_per_seq
    slot = g % 2
    tm = o_ref.shape[0]

    @pl.when(jnp.logical_and(g == 0, st == 0))
    def _():
        ya_ref[...] = jnp.zeros(ya_ref.shape, ya_ref.dtype)

    scores_of, weights_of, rows_of = _near_attention_stages(
        span_in_seq, q0, q1, kp0, kp1, kc0, kc1, vp0, vp1, vc0, vc1, b0, b1, far_out_ref, far_lse_ref)

    def step(half):
        offs = range(half * (SPAN // 2), (half + 1) * (SPAN // 2), QBLK)
        merge_rows = pl.ds(pl.multiple_of(st * tm, tm), tm)
        scores = [scores_of(o) for o in offs]
        ya = jnp.concatenate([ya_ref[1 - slot, h, merge_rows, :] for h in range(HEADS_PER_GROUP)], axis=1)
        a = jnp.dot(ya, wa_ref[...], preferred_element_type=_F32)
        bb = jnp.dot(yb_ref[...], wb_ref[...], preferred_element_type=_F32)
        weights = [weights_of(sc, o) for o, sc in zip(offs, scores)]
        for o, w in zip(offs, weights):
            ya_ref[slot, head_slot, o:o + QBLK, :] = rows_of(w, o)
        m = (ga_ref[...].astype(_F32) * a + gb_ref[...].astype(_F32) * bb).astype(_BF16)
        mix = jnp.dot(m, wo_ref[...], preferred_element_type=_F32)
        o_ref[...] = h_ref[...] + _rmsnorm_rows(mix, g_ref[...])

    for half in range(2):
        pl.when(st % 2 == half)(functools.partial(step, half))


def _attend_merge(proj3, far_out, far_lse, yb, h2d, wa, wb, wo, g, layer):
    b, s, d_in = proj3.shape
    t, d = h2d.shape
    steps = 2 * HEADS_PER_GROUP
    tm = SPAN // steps
    assert s % SPAN == 0 and QKV_WIDTH % d == 0
    spans_per_seq = s // SPAN
    n_spans = b * spans_per_seq
    gate_blk = QKV_WIDTH // d
    proj2d = proj3.reshape(t, d_in)
    n_heads = N_HEADS_A

    def attend_block(col_of_step, lag):
        def index(gi, st):
            sp = jnp.minimum(gi, n_spans - 1)
            return (sp // spans_per_seq, jnp.maximum(sp % spans_per_seq - lag, 0), col_of_step(st))
        return pl.BlockSpec((1, SPAN, HEAD_DIM), index)

    def col_specs(base, lag):
        return [attend_block(lambda st, c=base + grp * HEADS_PER_GROUP: c + st // 2, lag)
                for grp in range(len(NEAR_GROUPS))]

    merge_tile = lambda width, col: pl.BlockSpec(
        (tm, width), lambda gi, st: (jnp.where(gi == 0, 0, (gi - 1) * steps + st), col))
    resident = lambda w: pl.BlockSpec((None,) + w.shape[1:], lambda gi, st: (layer, 0, 0),
                                      pipeline_mode=pl.Buffered(1))
    bias_specs = [pl.BlockSpec((1, QBLK, w + QBLK), lambda gi, st: (st // 2, 0, 0)) for w, _ in NEAR_GROUPS]
    far_spec = attend_block(lambda st: st // 2, 0)
    qkv_specs = (col_specs(0, 0) + col_specs(n_heads, 1) + col_specs(n_heads, 0)
                 + col_specs(2 * n_heads, 1) + col_specs(2 * n_heads, 0))
    return pl.pallas_call(
        functools.partial(_attend_merge_kernel, n_spans=n_spans, spans_per_seq=spans_per_seq),
        grid=(n_spans + 1, steps),
        in_specs=qkv_specs + bias_specs + [far_spec, far_spec] + [
            merge_tile(WIDTH_B, 0), merge_tile(d, gate_blk), merge_tile(d, gate_blk + 1), merge_tile(d, 0),
            resident(wa), resident(wb), resident(wo), _layer_rows(layer, d),
        ],
        out_specs=merge_tile(d, 0),
        out_shape=jax.ShapeDtypeStruct((t, d), _F32),
        scratch_shapes=[pltpu.VMEM((2, HEADS_PER_GROUP, SPAN, HEAD_DIM), _BF16)],
        compiler_params=_params(("arbitrary", "arbitrary")),
        name="attend_merge",
    )(*([proj3] * (5 * len(NEAR_GROUPS))), *_dilated_bias_tables(), far_out, far_lse,
      yb, proj2d, proj2d, h2d, wa, wb, wo, g)


def _ffn_kernel(h_ref, hp_ref, gpre_ref, wg_ref, wv_ref, cwg_ref, cwv_ref, cbg_ref, cbv_ref, wd_ref,
                gpost_ref, o_ref, u_ref, *, tiles_per_seq, chunk):
    i = pl.program_id(0)
    j = pl.program_id(1)
    tm = h_ref.shape[0]
    n_chunks = tm // chunk

    @pl.when(j == 0)
    def _():
        halo = _rmsnorm_rows(hp_ref[...], gpre_ref[...])
        at_sequence_start = (i % tiles_per_seq) == 0
        u_ref[0:CONV_HALO, :] = jnp.where(at_sequence_start, 0.0, halo).astype(u_ref.dtype)
        o_ref[...] = jnp.zeros(o_ref.shape, o_ref.dtype)

        def body(c, carry):
            r0 = pl.multiple_of(c * chunk, chunk)
            x = h_ref[pl.ds(r0, chunk), :]
            u_ref[pl.ds(CONV_HALO + r0, chunk), :] = _rmsnorm_rows(x, gpre_ref[...]).astype(u_ref.dtype)
            return carry
        lax.fori_loop(0, n_chunks, body, 0)

    u = u_ref[...]

    def conv_branch(w_ref, cw_ref, cb_ref):
        x = jnp.dot(u, w_ref[...], preferred_element_type=_F32)
        cw = cw_ref[...]
        y = cb_ref[...] + cw[0:1, :] * pltpu.roll(x, 2, 0)
        y = y + cw[1:2, :] * pltpu.roll(x, 1, 0)
        y = y + cw[2:3, :] * x
        return y[CONV_HALO:, :]

    gate = conv_branch(wg_ref, cwg_ref, cbg_ref)
    value = conv_branch(wv_ref, cwv_ref, cbv_ref)
    act = (jax.nn.gelu(gate, approximate=True) * value).astype(_BF16)
    o_ref[...] += jnp.dot(act, wd_ref[...], preferred_element_type=_F32)

    @pl.when(j == pl.num_programs(1) - 1)
    def _():
        def body(c, carry):
            r0 = pl.multiple_of(c * chunk, chunk)
            f = o_ref[pl.ds(r0, chunk), :]
            o_ref[pl.ds(r0, chunk), :] = h_ref[pl.ds(r0, chunk), :] + _rmsnorm_rows(f, gpost_ref[...])
            return carry
        lax.fori_loop(0, n_chunks, body, 0)


def _ffn(h2d, gpre, w_up, conv_w, conv_b, w_down, gpost, layer, *, tm, tf, seq):
    t, d = h2d.shape
    d_ff = w_down.shape[1]
    assert t % tm == 0 and seq % tm == 0 and d_ff % tf == 0 and tm % CONV_HALO == 0
    nj = d_ff // tf
    halo_blocks = tm // CONV_HALO
    return pl.pallas_call(
        functools.partial(_ffn_kernel, tiles_per_seq=seq // tm, chunk=min(tm, 128)),
        grid=(t // tm, nj),
        in_specs=[
            pl.BlockSpec((tm, d), lambda i, j: (i, 0)),
            pl.BlockSpec((CONV_HALO, d), lambda i, j: (jnp.maximum(i * halo_blocks - 1, 0), 0)),
            _layer_rows(layer, d),
            pl.BlockSpec((None, d, tf), lambda i, j: (layer, 0, j)),
            pl.BlockSpec((None, d, tf), lambda i, j: (layer, 0, nj + j)),
            pl.BlockSpec((None, CONV_WIDTH, tf), lambda i, j: (layer, 0, j)),
            pl.BlockSpec((None, CONV_WIDTH, tf), lambda i, j: (layer, 0, nj + j)),
            pl.BlockSpec((None, 1, tf), lambda i, j: (layer, 0, j)),
            pl.BlockSpec((None, 1, tf), lambda i, j: (layer, 0, nj + j)),
            pl.BlockSpec((None, tf, d), lambda i, j: (layer, j, 0)),
            _layer_rows(layer, d),
        ],
        out_specs=pl.BlockSpec((tm, d), lambda i, j: (i, 0)),
        out_shape=jax.ShapeDtypeStruct((t, d), _F32),
        scratch_shapes=[pltpu.VMEM((CONV_HALO + tm, d), _BF16)],
        compiler_params=_params(("parallel", "arbitrary")),
        name="conv_ffn",
    )(h2d, h2d, gpre, w_up, w_up, conv_w, conv_w, conv_b, conv_b, w_down, gpost)


def _ple_kernel(h_ref, p_ref, wg_ref, wi_ref, o_ref):
    h = h_ref[...]
    gate = jax.nn.sigmoid(jnp.dot(h.astype(_BF16), wg_ref[...], preferred_element_type=_F32))
    e = jnp.dot(p_ref[...].astype(_BF16), wi_ref[...], preferred_element_type=_F32)
    o_ref[...] = h + gate * e


def _ple(h2d, p3d, w_gate, w_in, layer, *, tm):
    t, d = h2d.shape
    pd = p3d.shape[2]
    assert t % tm == 0
    resident = lambda w: pl.BlockSpec((None,) + w.shape[1:], lambda i: (layer, 0, 0), pipeline_mode=pl.Buffered(1))
    return pl.pallas_call(
        _ple_kernel,
        grid=(t // tm,),
        in_specs=[
            pl.BlockSpec((tm, d), lambda i: (i, 0)),
            pl.BlockSpec((None, tm, pd), lambda i: (layer, i, 0)),
            resident(w_gate),
            resident(w_in),
        ],
        out_specs=pl.BlockSpec((tm, d), lambda i: (i, 0)),
        out_shape=jax.ShapeDtypeStruct((t, d), _F32),
        compiler_params=_params(("parallel",)),
        name="layer_embedding",
    )(h2d, p3d, w_gate, w_in)


def _tiles(t, seq, d, d_in, d_ff):
    def largest(n, cands):
        return next(c for c in cands if n % c == 0)
    return dict(
        proj_tm=largest(t, (1024, 512, 256, 128)),
        proj_tn=largest(math.gcd(d_in, QKV_WIDTH), (2048, 1024, 512, 256, 128)),
        stick_tq=largest(seq, (256, 128)),
        merge_tm=largest(t, (512, 256, 128)),
        ffn_tm=largest(seq, (1024, 512, 256, 128)),
        ffn_tf=largest(d_ff, (512, 256, 128)),
        ple_tm=largest(t, (1024, 512, 256, 128)),
    )


def kernel(x, p, g_mix_pre, w_in, w_branch_a, w_branch_b, w_out, g_mix_post, g_ffn_pre, w_up, conv_w,
           conv_b, w_down, g_ffn_post, w_ple_in, w_ple_gate):
    b, s, d = x.shape
    depth = w_in.shape[0]
    d_in = w_in.shape[2]
    d_ff = w_down.shape[1]
    t = b * s
    assert d_in == QKV_WIDTH + 2 * d
    tiles = _tiles(t, s, d, d_in, d_ff)

    q_scale = 1.0 / math.sqrt(HEAD_DIM)
    col = jnp.arange(d_in)
    is_q = (col < WIDTH_A) | ((col >= 3 * WIDTH_A) & (col < 3 * WIDTH_A + WIDTH_B))
    colscale = jnp.where(is_q, q_scale, 1.0).astype(_F32)[None, :]

    bf16 = lambda w: w.astype(_BF16)
    rows = lambda v: v[:, None, :]
    w_in, w_branch_a, w_branch_b, w_out = bf16(w_in), bf16(w_branch_a), bf16(w_branch_b), bf16(w_out)
    w_up, w_down, w_ple_in, w_ple_gate = bf16(w_up), bf16(w_down), bf16(w_ple_in), bf16(w_ple_gate)
    p3d = p.reshape(depth, t, p.shape[-1])

    h = x.reshape(t, d)
    for i in range(depth):
        proj, far_qkv = _in_proj(h, rows(g_mix_pre), colscale, w_in, i, tm=tiles["proj_tm"], tn=tiles["proj_tn"])
        proj3 = proj.reshape(b, s, d_in)
        far_out, far_lse = _far_attention(far_qkv.reshape(b, s, 3 * A_OUT), _far_bias_table())
        yb = _stick_attention(proj3, tq=tiles["stick_tq"], heads=N_HEADS_B).reshape(t, WIDTH_B)
        h = _attend_merge(proj3, far_out, far_lse, yb, h, w_branch_a, w_branch_b, w_out, rows(g_mix_post), i)
        h = _ffn(h, rows(g_ffn_pre), w_up, conv_w, rows(conv_b), w_down, rows(g_ffn_post), i,
                 tm=tiles["ffn_tm"], tf=tiles["ffn_tf"], seq=s)
        h = _ple(h, p3d, w_ple_gate, w_ple_in, i, tm=tiles["ple_tm"])
    return h.reshape(b, s, d)
```

```python
import functools
import math

import jax
import jax.numpy as jnp
from jax import lax
from jax.experimental import pallas as pl
from jax.experimental.pallas import tpu as pltpu

HEAD_DIM = 128
DIL_GROUPS = ((128, 1), (512, 4), (2048, 16))
NEAR_GROUPS = DIL_GROUPS[:-1]
FAR_RESIDUES_PER_STEP = 16
HEADS_PER_GROUP = 4
N_HEADS_A = HEADS_PER_GROUP * len(DIL_GROUPS)
N_HEADS_B = 4
WIDTH_A = N_HEADS_A * HEAD_DIM
WIDTH_B = N_HEADS_B * HEAD_DIM
A_OUT = HEADS_PER_GROUP * HEAD_DIM
QKV_WIDTH = 3 * WIDTH_A + 3 * WIDTH_B
CONV_WIDTH = 3
RMS_EPS = 1e-6
ALIBI_MAX = 8.0
MASKED = -1e30
EXP_UNDERFLOW = -110.0
QBLK = 128
SPAN = 2048
CONV_HALO = 16
VMEM_LIMIT = 56 * 1024 * 1024

_F32 = jnp.float32
_BF16 = jnp.bfloat16


def _rmsnorm_rows(x, g):
    ms = jnp.mean(x * x, axis=-1, keepdims=True)
    return x * lax.rsqrt(ms + RMS_EPS) * g


def _params(sem):
    return pltpu.CompilerParams(dimension_semantics=sem, vmem_limit_bytes=VMEM_LIMIT)


def _layer_rows(layer, width):
    return pl.BlockSpec((None, 1, width), lambda *_: (layer, 0, 0))


def _in_proj_kernel(x_ref, g_ref, cs_ref, w_ref, o_ref, u_ref, *, gate_tile0, chunk):
    j = pl.program_id(1)
    tm = x_ref.shape[0]

    @pl.when(j == 0)
    def _():
        def body(c, carry):
            r0 = pl.multiple_of(c * chunk, chunk)
            x = x_ref[pl.ds(r0, chunk), :]
            u_ref[pl.ds(r0, chunk), :] = _rmsnorm_rows(x, g_ref[...]).astype(u_ref.dtype)
            return carry
        lax.fori_loop(0, tm // chunk, body, 0)

    acc = jnp.dot(u_ref[...], w_ref[...], preferred_element_type=_F32)

    o_ref[...] = jnp.where(j >= gate_tile0, jax.nn.sigmoid(acc), acc * cs_ref[...]).astype(o_ref.dtype)


def _in_proj(h2d, g, colscale, w, layer, *, tm, tn):
    t, d = h2d.shape
    d_in = w.shape[2]
    assert t % tm == 0 and d_in % tn == 0 and QKV_WIDTH % tn == 0
    return pl.pallas_call(
        functools.partial(_in_proj_kernel, gate_tile0=QKV_WIDTH // tn, chunk=min(tm, 128)),
        grid=(t // tm, d_in // tn),
        in_specs=[
            pl.BlockSpec((tm, d), lambda i, j: (i, 0)),
            _layer_rows(layer, d),
            pl.BlockSpec((1, tn), lambda i, j: (0, j)),
            pl.BlockSpec((None, d, tn), lambda i, j: (layer, 0, j)),
        ],
        out_specs=pl.BlockSpec((tm, tn), lambda i, j: (i, j)),
        out_shape=jax.ShapeDtypeStruct((t, d_in), _BF16),
        scratch_shapes=[pltpu.VMEM((tm, d), _BF16)],
        compiler_params=_params(("parallel", "arbitrary")),
        name="in_proj",
    )(h2d, g, colscale, w)


def _dilated_kernel(q0, q1, kp0, kp1, kc0, kc1, vp0, vp1, vc0, vc1, b0, b1, far_out_ref, far_lse_ref, o_ref):
    i = pl.program_id(2)
    q_refs, bias_refs = (q0, q1), (b0, b1)
    k_refs, v_refs = ((kp0, kc0), (kp1, kc1)), ((vp0, vc0), (vp1, vc1))

    def window_rows(prev_cur, o, window):
        prev_ref, cur_ref = prev_cur
        n_prev = max(window - o, 0)
        cur = cur_ref[0, max(o - window, 0):o + QBLK, :]
        if n_prev == 0:
            return cur, 0
        return jnp.concatenate([prev_ref[0, SPAN - n_prev:SPAN, :], cur], axis=0), n_prev

    def masked_scores(o):
        scores = []
        for g, (window, _) in enumerate(NEAR_GROUPS):
            kk, n_prev = window_rows(k_refs[g], o, window)
            sc = lax.dot_general(q_refs[g][0, o:o + QBLK, :], kk, (((1,), (1,)), ((), ())),
                                 preferred_element_type=_F32)
            sc = sc + bias_refs[g][0]
            if n_prev:
                kj = lax.broadcasted_iota(jnp.int32, sc.shape, 1)
                sc = jnp.where(jnp.logical_or(kj >= n_prev, i > 0), sc, MASKED)
            scores.append(sc)
        return scores

    def joint_softmax_weights(scores, o):
        far_lse = far_lse_ref[0, o:o + QBLK, 0:1]
        m = functools.reduce(jnp.maximum, [jnp.max(sc, axis=-1, keepdims=True) for sc in scores] + [far_lse])
        ps = [jnp.exp(sc - m) for sc in scores]
        far_weight = jnp.exp(far_lse - m)
        denom = functools.reduce(jnp.add, [jnp.sum(p, axis=-1, keepdims=True) for p in ps] + [far_weight])
        return [p.astype(_BF16) for p in ps], far_weight, denom

    def weighted_values(ps, far_weight, o):
        acc = far_weight * far_out_ref[0, o:o + QBLK, :]
        for g, (window, _) in enumerate(NEAR_GROUPS):
            acc = acc + jnp.dot(ps[g], window_rows(v_refs[g], o, window)[0], preferred_element_type=_F32)
        return acc

    offs = range(0, SPAN, QBLK)
    staged = [masked_scores(o) for o in offs]
    weights = [joint_softmax_weights(scores, o) for o, scores in zip(offs, staged)]
    for o, (ps, far_weight, denom) in zip(offs, weights):
        o_ref[0, o:o + QBLK, :] = (weighted_values(ps, far_weight, o) / denom).astype(o_ref.dtype)


def _alibi_slopes(group):
    slopes = jnp.exp2(-ALIBI_MAX * jnp.arange(1, N_HEADS_A + 1, dtype=_F32) / N_HEADS_A)
    return slopes[group * HEADS_PER_GROUP:(group + 1) * HEADS_PER_GROUP]


def _dilated_bias_tables():
    tables = []
    for g, (window, dilation) in enumerate(NEAR_GROUPS):
        qi = jnp.arange(QBLK)[:, None]
        kj = jnp.arange(window + QBLK)[None, :]
        dist = qi + window - kj
        valid = (dist >= 0) & (dist <= window) & (dist % dilation == 0)
        bias = -_alibi_slopes(g)[:, None, None] * dist.astype(_F32)[None]
        tables.append(jnp.where(valid[None], bias, MASKED))
    return tables


def _far_bias_table():
    window, dilation = DIL_GROUPS[-1]
    qi = jnp.arange(QBLK)[:, None]
    kj = jnp.arange(2 * QBLK)[None, :]
    steps = QBLK + qi - kj
    valid = (steps >= 0) & (steps <= window // dilation)
    bias = -_alibi_slopes(len(DIL_GROUPS) - 1)[:, None, None] * (steps * dilation).astype(_F32)[None]
    return jnp.where(valid[None], bias, MASKED)


def _far_kernel(q_ref, kp_ref, kc_ref, vp_ref, vc_ref, bias_ref, out_ref, lse_ref):
    n = pl.program_id(1)
    kj = lax.broadcasted_iota(jnp.int32, (QBLK, 2 * QBLK), 1)
    in_sequence = kj >= jnp.where(n == 0, QBLK, 0)
    residue_major = lambda ref: pltpu.einshape("mrd->rmd", ref[0])
    q_all = residue_major(q_ref)
    k_all = jnp.concatenate([residue_major(kp_ref), residue_major(kc_ref)], axis=1)
    v_all = jnp.concatenate([residue_major(vp_ref), residue_major(vc_ref)], axis=1)
    outs, lses = [], []
    for r in range(FAR_RESIDUES_PER_STEP):
        lanes = [slice(h * HEAD_DIM, (h + 1) * HEAD_DIM) for h in range(HEADS_PER_GROUP)]
        scores = []
        for h in range(HEADS_PER_GROUP):
            sc = lax.dot_general(q_all[r][:, lanes[h]], k_all[r][:, lanes[h]], (((1,), (1,)), ((), ())),
                                 preferred_element_type=_F32) + bias_ref[h]
            scores.append(jnp.where(in_sequence, sc, MASKED))
        out_r, lse_r = [], []
        for h in range(HEADS_PER_GROUP):
            m = jnp.max(scores[h], axis=-1, keepdims=True)
            p = jnp.exp(scores[h] - m)
            denom = jnp.sum(p, axis=-1, keepdims=True)
            acc = jnp.dot(p.astype(_BF16), v_all[r][:, lanes[h]], preferred_element_type=_F32)
            out_r.append(acc / denom)
            lse_r.append(jnp.broadcast_to(m + jnp.log(denom), (QBLK, HEAD_DIM)))
        outs.append(jnp.concatenate(out_r, axis=1))
        lses.append(jnp.concatenate(lse_r, axis=1))
    out_ref[0] = pltpu.einshape("rmd->mrd", jnp.stack(outs))
    lse_ref[0] = pltpu.einshape("rmd->mrd", jnp.stack(lses))


def _far_attention(proj3, bias):
    b, s, d_in = proj3.shape
    dilation = DIL_GROUPS[-1][1]
    positions = s // dilation
    assert s % (dilation * QBLK) == 0 and dilation % FAR_RESIDUES_PER_STEP == 0 and WIDTH_A % A_OUT == 0
    view = proj3.reshape(b, positions, dilation, d_in)
    blk = (1, QBLK, FAR_RESIDUES_PER_STEP, A_OUT)
    col = lambda part: (part * WIDTH_A) // A_OUT + len(DIL_GROUPS) - 1
    cur = lambda part: pl.BlockSpec(blk, lambda bi, n, a: (bi, n, a, col(part)))
    prev = lambda part: pl.BlockSpec(blk, lambda bi, n, a: (bi, jnp.maximum(n - 1, 0), a, col(part)))
    out_spec = pl.BlockSpec(blk, lambda bi, n, a: (bi, n, a, 0))
    shape = jax.ShapeDtypeStruct((b, positions, dilation, A_OUT), _F32)
    out, lse = pl.pallas_call(
        _far_kernel,
        grid=(b, positions // QBLK, dilation // FAR_RESIDUES_PER_STEP),
        in_specs=[cur(0), prev(1), cur(1), prev(2), cur(2),
                  pl.BlockSpec(bias.shape, lambda bi, n, a: (0, 0, 0))],
        out_specs=[out_spec, out_spec],
        out_shape=[shape, shape],
        compiler_params=_params(("parallel", "parallel", "parallel")),
        name="far_dilated_attention",
    )(view, view, view, view, view, bias)
    return out.reshape(b, s, A_OUT), lse.reshape(b, s, A_OUT)


def _dilated_attention(proj3):
    b, s, _ = proj3.shape
    assert s % SPAN == 0
    n_heads = N_HEADS_A
    tables = _dilated_bias_tables()
    far_out, far_lse = _far_attention(proj3, _far_bias_table())

    def col_spec(base, lag):
        return [pl.BlockSpec((1, SPAN, HEAD_DIM), (lambda bi, hh, i, c=base + g * HEADS_PER_GROUP:
                                                   (bi, jnp.maximum(i - lag, 0), c + hh)))
                for g in range(len(NEAR_GROUPS))]

    bias_specs = [pl.BlockSpec((1, QBLK, w + QBLK), lambda bi, hh, i: (hh, 0, 0)) for w, _ in NEAR_GROUPS]
    far_spec = pl.BlockSpec((1, SPAN, HEAD_DIM), lambda bi, hh, i: (bi, i, hh))
    qkv_specs = (col_spec(0, 0) + col_spec(n_heads, 1) + col_spec(n_heads, 0)
                 + col_spec(2 * n_heads, 1) + col_spec(2 * n_heads, 0))
    return pl.pallas_call(
        _dilated_kernel,
        grid=(b, HEADS_PER_GROUP, s // SPAN),
        in_specs=qkv_specs + bias_specs + [far_spec, far_spec],
        out_specs=pl.BlockSpec((1, SPAN, HEAD_DIM), lambda bi, hh, i: (bi, i, hh)),
        out_shape=jax.ShapeDtypeStruct((b, s, A_OUT), _BF16),
        compiler_params=_params(("parallel", "parallel", "parallel")),
        name="dilated_attention",
    )(*([proj3] * (5 * len(NEAR_GROUPS))), *tables, far_out, far_lse)


def _stick_kernel(q_ref, k_ref, v_ref, tri_ref, o_ref, *, tq, heads):
    i = pl.program_id(2)
    row = lax.broadcasted_iota(jnp.int32, (tq, tq), 0)
    col = lax.broadcasted_iota(jnp.int32, (tq, tq), 1)
    before = col < row

    def chunk(c, afters, accs, diagonal):
        k0 = pl.multiple_of(c * tq, tq)
        tri = tri_ref[...]
        lanes = [slice(h * HEAD_DIM, (h + 1) * HEAD_DIM) for h in range(heads)]
        zs = [lax.dot_general(q_ref[0, :, lanes[h]], k_ref[0, pl.ds(k0, tq), lanes[h]],
                              (((1,), (1,)), ((), ())), preferred_element_type=_F32) for h in range(heads)]
        csums = []
        for h in range(heads):
            z = zs[h]
            ln = -(jnp.maximum(z, 0.0) + jnp.log(1.0 + jnp.exp(-jnp.abs(z))))
            if diagonal:
                ln = jnp.where(before, ln, 0.0)
            hi = ln.astype(_BF16)
            lo = (ln - hi.astype(_F32)).astype(_BF16)
            csums.append(jnp.dot(hi, tri, preferred_element_type=_F32)
                         + jnp.dot(lo, tri, preferred_element_type=_F32))
        new_afters, new_accs = [], []
        for h in range(heads):
            w = jnp.exp(zs[h] + csums[h] + afters[h])
            if diagonal:
                w = jnp.where(before, w, 0.0)
            new_accs.append(accs[h] + jnp.dot(w.astype(_BF16), v_ref[0, pl.ds(k0, tq), lanes[h]],
                                              preferred_element_type=_F32))
            new_afters.append(afters[h] + csums[h][:, 0:1])
        return tuple(new_afters), tuple(new_accs)

    afters, accs = chunk(i, (jnp.zeros((tq, 1), _F32),) * heads, (jnp.zeros((tq, HEAD_DIM), _F32),) * heads, True)

    def keep_going(carry):
        c, afters, _ = carry
        return jnp.logical_and(c >= 0, jnp.max(functools.reduce(jnp.maximum, afters)) > EXP_UNDERFLOW)

    def body(carry):
        c, afters, accs = carry
        afters, accs = chunk(c, afters, accs, False)
        return c - 1, afters, accs

    _, _, accs = lax.while_loop(keep_going, body, (i - 1, afters, accs))
    for h in range(heads):
        o_ref[0, :, h * HEAD_DIM:(h + 1) * HEAD_DIM] = accs[h].astype(o_ref.dtype)


def _stick_attention(proj3, *, tq, heads):
    b, s, _ = proj3.shape
    width = heads * HEAD_DIM
    assert s % tq == 0 and N_HEADS_B % heads == 0 and (3 * WIDTH_A) % width == 0
    groups = N_HEADS_B // heads
    base = 3 * WIDTH_A // width
    idx = jnp.arange(tq)
    tri = (idx[:, None] >= idx[None, :]).astype(_BF16)
    whole_seq = lambda part: pl.BlockSpec((1, s, width), lambda bi, hg, i: (bi, 0, base + part * groups + hg),
                                          pipeline_mode=pl.Buffered(1))
    return pl.pallas_call(
        functools.partial(_stick_kernel, tq=tq, heads=heads),
        grid=(b, groups, s // tq),
        in_specs=[
            pl.BlockSpec((1, tq, width), lambda bi, hg, i: (bi, i, base + hg)),
            whole_seq(1),
            whole_seq(2),
            pl.BlockSpec((tq, tq), lambda bi, hg, i: (0, 0)),
        ],
        out_specs=pl.BlockSpec((1, tq, width), lambda bi, hg, i: (bi, i, hg)),
        out_shape=jax.ShapeDtypeStruct((b, s, WIDTH_B), _BF16),
        compiler_params=_params(("parallel", "parallel", "arbitrary")),
        name="stick_breaking_attention",
    )(proj3, proj3, proj3, tri)


def _merge_kernel(ya_ref, yb_ref, ga_ref, gb_ref, h_ref, wa_ref, wb_ref, wo_ref, g_ref, o_ref):
    a = jnp.dot(ya_ref[...], wa_ref[...], preferred_element_type=_F32)
    bb = jnp.dot(yb_ref[...], wb_ref[...], preferred_element_type=_F32)
    m = (ga_ref[...].astype(_F32) * a + gb_ref[...].astype(_F32) * bb).astype(_BF16)
    mix = jnp.dot(m, wo_ref[...], preferred_element_type=_F32)
    o_ref[...] = h_ref[...] + _rmsnorm_rows(mix, g_ref[...])


def _merge(ya, yb, proj, h2d, wa, wb, wo, g, layer, *, tm):
    t, d = h2d.shape
    assert t % tm == 0 and QKV_WIDTH % d == 0
    gate_blk = QKV_WIDTH // d
    resident = lambda w: pl.BlockSpec((None,) + w.shape[1:], lambda i: (layer, 0, 0), pipeline_mode=pl.Buffered(1))
    return pl.pallas_call(
        _merge_kernel,
        grid=(t // tm,),
        in_specs=[
            pl.BlockSpec((tm, A_OUT), lambda i: (i, 0)),
            pl.BlockSpec((tm, WIDTH_B), lambda i: (i, 0)),
            pl.BlockSpec((tm, d), lambda i: (i, gate_blk)),
            pl.BlockSpec((tm, d), lambda i: (i, gate_blk + 1)),
            pl.BlockSpec((tm, d), lambda i: (i, 0)),
            resident(wa), resident(wb), resident(wo), _layer_rows(layer, d),
        ],
        out_specs=pl.BlockSpec((tm, d), lambda i: (i, 0)),
        out_shape=jax.ShapeDtypeStruct((t, d), _F32),
        compiler_params=_params(("parallel",)),
        name="gated_merge",
    )(ya, yb, proj, proj, h2d, wa, wb, wo, g)


def _ffn_kernel(h_ref, hp_ref, gpre_ref, wg_ref, wv_ref, cwg_ref, cwv_ref, cbg_ref, cbv_ref, wd_ref,
                gpost_ref, o_ref, u_ref, *, tiles_per_seq, chunk):
    i = pl.program_id(0)
    j = pl.program_id(1)
    tm = h_ref.shape[0]
    n_chunks = tm // chunk

    @pl.when(j == 0)
    def _():
        halo = _rmsnorm_rows(hp_ref[...], gpre_ref[...])
        at_sequence_start = (i % tiles_per_seq) == 0
        u_ref[0:CONV_HALO, :] = jnp.where(at_sequence_start, 0.0, halo).astype(u_ref.dtype)
        o_ref[...] = jnp.zeros(o_ref.shape, o_ref.dtype)

        def body(c, carry):
            r0 = pl.multiple_of(c * chunk, chunk)
            x = h_ref[pl.ds(r0, chunk), :]
            u_ref[pl.ds(CONV_HALO + r0, chunk), :] = _rmsnorm_rows(x, gpre_ref[...]).astype(u_ref.dtype)
            return carry
        lax.fori_loop(0, n_chunks, body, 0)

    u = u_ref[...]

    def conv_branch(w_ref, cw_ref, cb_ref):
        x = jnp.dot(u, w_ref[...], preferred_element_type=_F32)
        cw = cw_ref[...]
        y = cb_ref[...] + cw[0:1, :] * pltpu.roll(x, 2, 0)
        y = y + cw[1:2, :] * pltpu.roll(x, 1, 0)
        y = y + cw[2:3, :] * x
        return y[CONV_HALO:, :]

    gate = conv_branch(wg_ref, cwg_ref, cbg_ref)
    value = conv_branch(wv_ref, cwv_ref, cbv_ref)
    act = (jax.nn.gelu(gate, approximate=True) * value).astype(_BF16)
    o_ref[...] += jnp.dot(act, wd_ref[...], preferred_element_type=_F32)

    @pl.when(j == pl.num_programs(1) - 1)
    def _():
        def body(c, carry):
            r0 = pl.multiple_of(c * chunk, chunk)
            f = o_ref[pl.ds(r0, chunk), :]
            o_ref[pl.ds(r0, chunk), :] = h_ref[pl.ds(r0, chunk), :] + _rmsnorm_rows(f, gpost_ref[...])
            return carry
        lax.fori_loop(0, n_chunks, body, 0)


def _ffn(h2d, gpre, w_up, conv_w, conv_b, w_down, gpost, layer, *, tm, tf, seq):
    t, d = h2d.shape
    d_ff = w_down.shape[1]
    assert t % tm == 0 and seq % tm == 0 and d_ff % tf == 0 and tm % CONV_HALO == 0
    nj = d_ff // tf
    halo_blocks = tm // CONV_HALO
    return pl.pallas_call(
        functools.partial(_ffn_kernel, tiles_per_seq=seq // tm, chunk=min(tm, 128)),
        grid=(t // tm, nj),
        in_specs=[
            pl.BlockSpec((tm, d), lambda i, j: (i, 0)),
            pl.BlockSpec((CONV_HALO, d), lambda i, j: (jnp.maximum(i * halo_blocks - 1, 0), 0)),
            _layer_rows(layer, d),
            pl.BlockSpec((None, d, tf), lambda i, j: (layer, 0, j)),
            pl.BlockSpec((None, d, tf), lambda i, j: (layer, 0, nj + j)),
            pl.BlockSpec((None, CONV_WIDTH, tf), lambda i, j: (layer, 0, j)),
            pl.BlockSpec((None, CONV_WIDTH, tf), lambda i, j: (layer, 0, nj + j)),
            pl.BlockSpec((None, 1, tf), lambda i, j: (layer, 0, j)),
            pl.BlockSpec((None, 1, tf), lambda i, j: (layer, 0, nj + j)),
            pl.BlockSpec((None, tf, d), lambda i, j: (layer, j, 0)),
            _layer_rows(layer, d),
        ],
        out_specs=pl.BlockSpec((tm, d), lambda i, j: (i, 0)),
        out_shape=jax.ShapeDtypeStruct((t, d), _F32),
        scratch_shapes=[pltpu.VMEM((CONV_HALO + tm, d), _BF16)],
        compiler_params=_params(("parallel", "arbitrary")),
        name="conv_ffn",
    )(h2d, h2d, gpre, w_up, w_up, conv_w, conv_w, conv_b, conv_b, w_down, gpost)


def _ple_kernel(h_ref, p_ref, wg_ref, wi_ref, o_ref):
    h = h_ref[...]
    gate = jax.nn.sigmoid(jnp.dot(h.astype(_BF16), wg_ref[...], preferred_element_type=_F32))
    e = jnp.dot(p_ref[...].astype(_BF16), wi_ref[...], preferred_element_type=_F32)
    o_ref[...] = h + gate * e


def _ple(h2d, p3d, w_gate, w_in, layer, *, tm):
    t, d = h2d.shape
    pd = p3d.shape[2]
    assert t % tm == 0
    resident = lambda w: pl.BlockSpec((None,) + w.shape[1:], lambda i: (layer, 0, 0), pipeline_mode=pl.Buffered(1))
    return pl.pallas_call(
        _ple_kernel,
        grid=(t // tm,),
        in_specs=[
            pl.BlockSpec((tm, d), lambda i: (i, 0)),
            pl.BlockSpec((None, tm, pd), lambda i: (layer, i, 0)),
            resident(w_gate),
            resident(w_in),
        ],
        out_specs=pl.BlockSpec((tm, d), lambda i: (i, 0)),
        out_shape=jax.ShapeDtypeStruct((t, d), _F32),
        compiler_params=_params(("parallel",)),
        name="layer_embedding",
    )(h2d, p3d, w_gate, w_in)


def _tiles(t, seq, d, d_in, d_ff):
    def largest(n, cands):
        return next(c for c in cands if n % c == 0)
    return dict(
        proj_tm=largest(t, (1024, 512, 256, 128)),
        proj_tn=largest(math.gcd(d_in, QKV_WIDTH), (2048, 1024, 512, 256, 128)),
        stick_tq=largest(seq, (256, 128)),
        merge_tm=largest(t, (512, 256, 128)),
        ffn_tm=largest(seq, (1024, 512, 256, 128)),
        ffn_tf=largest(d_ff, (512, 256, 128)),
        ple_tm=largest(t, (1024, 512, 256, 128)),
    )


def kernel(x, p, g_mix_pre, w_in, w_branch_a, w_branch_b, w_out, g_mix_post, g_ffn_pre, w_up, conv_w,
           conv_b, w_down, g_ffn_post, w_ple_in, w_ple_gate):
    b, s, d = x.shape
    depth = w_in.shape[0]
    d_in = w_in.shape[2]
    d_ff = w_down.shape[1]
    t = b * s
    assert d_in == QKV_WIDTH + 2 * d
    tiles = _tiles(t, s, d, d_in, d_ff)

    q_scale = 1.0 / math.sqrt(HEAD_DIM)
    col = jnp.arange(d_in)
    is_q = (col < WIDTH_A) | ((col >= 3 * WIDTH_A) & (col < 3 * WIDTH_A + WIDTH_B))
    colscale = jnp.where(is_q, q_scale, 1.0).astype(_F32)[None, :]

    bf16 = lambda w: w.astype(_BF16)
    rows = lambda v: v[:, None, :]
    w_in, w_branch_a, w_branch_b, w_out = bf16(w_in), bf16(w_branch_a), bf16(w_branch_b), bf16(w_out)
    w_up, w_down, w_ple_in, w_ple_gate = bf16(w_up), bf16(w_down), bf16(w_ple_in), bf16(w_ple_gate)
    p3d = p.reshape(depth, t, p.shape[-1])

    h = x.reshape(t, d)
    for i in range(depth):
        proj = _in_proj(h, rows(g_mix_pre), colscale, w_in, i, tm=tiles["proj_tm"], tn=tiles["proj_tn"])
        proj3 = proj.reshape(b, s, d_in)
        ya = _dilated_attention(proj3).reshape(t, A_OUT)
        yb = _stick_attention(proj3, tq=tiles["stick_tq"], heads=N_HEADS_B).reshape(t, WIDTH_B)
        h = _merge(ya, yb, proj, h, w_branch_a, w_branch_b, w_out, rows(g_mix_post), i, tm=tiles["merge_tm"])
        h = _ffn(h, rows(g_ffn_pre), w_up, conv_w, rows(conv_b), w_down, rows(g_ffn_post), i,
                 tm=tiles["ffn_tm"], tf=tiles["ffn_tf"], seq=s)
        h = _ple(h, p3d, w_ple_gate, w_ple_in, i, tm=tiles["ple_tm"])
    return h.reshape(b, s, d)
```

```python
import functools
import math

import jax
import jax.numpy as jnp
from jax import lax
from jax.experimental import pallas as pl
from jax.experimental.pallas import tpu as pltpu

HEAD_DIM = 128
DIL_GROUPS = ((128, 1), (512, 4), (2048, 16))
NEAR_GROUPS = DIL_GROUPS[:-1]
HEADS_PER_GROUP = 4
N_HEADS_A = HEADS_PER_GROUP * len(DIL_GROUPS)
N_HEADS_B = 4
WIDTH_A = N_HEADS_A * HEAD_DIM
WIDTH_B = N_HEADS_B * HEAD_DIM
A_OUT = HEADS_PER_GROUP * HEAD_DIM
QKV_WIDTH = 3 * WIDTH_A + 3 * WIDTH_B
CONV_WIDTH = 3
RMS_EPS = 1e-6
ALIBI_MAX = 8.0
MASKED = -1e30
EXP_UNDERFLOW = -110.0
QBLK = 128
SPAN = 2048
CONV_HALO = 16
VMEM_LIMIT = 56 * 1024 * 1024

_F32 = jnp.float32
_BF16 = jnp.bfloat16


def _rmsnorm_rows(x, g):
    ms = jnp.mean(x * x, axis=-1, keepdims=True)
    return x * lax.rsqrt(ms + RMS_EPS) * g


def _params(sem):
    return pltpu.CompilerParams(dimension_semantics=sem, vmem_limit_bytes=VMEM_LIMIT)


def _layer_rows(layer, width):
    return pl.BlockSpec((None, 1, width), lambda *_: (layer, 0, 0))


def _in_proj_kernel(x_ref, g_ref, cs_ref, w_ref, o_ref, u_ref, *, gate_tile0, chunk):
    j = pl.program_id(1)
    tm = x_ref.shape[0]

    @pl.when(j == 0)
    def _():
        def body(c, carry):
            r0 = pl.multiple_of(c * chunk, chunk)
            x = x_ref[pl.ds(r0, chunk), :]
            u_ref[pl.ds(r0, chunk), :] = _rmsnorm_rows(x, g_ref[...]).astype(u_ref.dtype)
            return carry
        lax.fori_loop(0, tm // chunk, body, 0)

    acc = jnp.dot(u_ref[...], w_ref[...], preferred_element_type=_F32)

    o_ref[...] = jnp.where(j >= gate_tile0, jax.nn.sigmoid(acc), acc * cs_ref[...]).astype(o_ref.dtype)


def _in_proj(h2d, g, colscale, w, layer, *, tm, tn):
    t, d = h2d.shape
    d_in = w.shape[2]
    assert t % tm == 0 and d_in % tn == 0 and QKV_WIDTH % tn == 0
    return pl.pallas_call(
        functools.partial(_in_proj_kernel, gate_tile0=QKV_WIDTH // tn, chunk=min(tm, 128)),
        grid=(t // tm, d_in // tn),
        in_specs=[
            pl.BlockSpec((tm, d), lambda i, j: (i, 0)),
            _layer_rows(layer, d),
            pl.BlockSpec((1, tn), lambda i, j: (0, j)),
            pl.BlockSpec((None, d, tn), lambda i, j: (layer, 0, j)),
        ],
        out_specs=pl.BlockSpec((tm, tn), lambda i, j: (i, j)),
        out_shape=jax.ShapeDtypeStruct((t, d_in), _BF16),
        scratch_shapes=[pltpu.VMEM((tm, d), _BF16)],
        compiler_params=_params(("parallel", "arbitrary")),
        name="in_proj",
    )(h2d, g, colscale, w)


def _dilated_kernel(q0, q1, kp0, kp1, kc0, kc1, vp0, vp1, vc0, vc1, b0, b1,
                    fq, fkp, fkc, fvp, fvc, fbias, o_ref):
    i = pl.program_id(2)
    q_refs, bias_refs = (q0, q1), (b0, b1)
    k_refs, v_refs = ((kp0, kc0), (kp1, kc1)), ((vp0, vc0), (vp1, vc1))

    def far_group():
        residue_major = lambda ref: pltpu.einshape("mrd->rmd", ref[0])
        token_major = lambda rows: pltpu.einshape("rmd->mrd", jnp.stack(rows)).reshape(SPAN, HEAD_DIM)
        qf = residue_major(fq)
        kf = jnp.concatenate([residue_major(fkp), residue_major(fkc)], axis=1)
        vf = jnp.concatenate([residue_major(fvp), residue_major(fvc)], axis=1)
        kj = lax.broadcasted_iota(jnp.int32, (QBLK, 2 * QBLK), 1)
        in_sequence = kj >= jnp.where(i == 0, QBLK, 0)
        residues = range(qf.shape[0])
        scores = [jnp.where(in_sequence,
                            lax.dot_general(qf[r], kf[r], (((1,), (1,)), ((), ())), preferred_element_type=_F32)
                            + fbias[0], MASKED) for r in residues]
        outs, lses = [], []
        for r in residues:
            m = jnp.max(scores[r], axis=-1, keepdims=True)
            p = jnp.exp(scores[r] - m)
            denom = jnp.sum(p, axis=-1, keepdims=True)
            outs.append(jnp.dot(p.astype(_BF16), vf[r], preferred_element_type=_F32) / denom)
            lses.append(jnp.broadcast_to(m + jnp.log(denom), (QBLK, HEAD_DIM)))
        return token_major(outs), token_major(lses)

    far_out, far_lse_lanes = far_group()

    def window_rows(prev_cur, o, window):
        prev_ref, cur_ref = prev_cur
        n_prev = max(window - o, 0)
        cur = cur_ref[0, max(o - window, 0):o + QBLK, :]
        if n_prev == 0:
            return cur, 0
        return jnp.concatenate([prev_ref[0, SPAN - n_prev:SPAN, :], cur], axis=0), n_prev

    def masked_scores(o):
        scores = []
        for g, (window, _) in enumerate(NEAR_GROUPS):
            kk, n_prev = window_rows(k_refs[g], o, window)
            sc = lax.dot_general(q_refs[g][0, o:o + QBLK, :], kk, (((1,), (1,)), ((), ())),
                                 preferred_element_type=_F32)
            sc = sc + bias_refs[g][0]
            if n_prev:
                kj = lax.broadcasted_iota(jnp.int32, sc.shape, 1)
                sc = jnp.where(jnp.logical_or(kj >= n_prev, i > 0), sc, MASKED)
            scores.append(sc)
        return scores

    def joint_softmax_weights(scores, o):
        far_lse = far_lse_lanes[o:o + QBLK, 0:1]
        m = functools.reduce(jnp.maximum, [jnp.max(sc, axis=-1, keepdims=True) for sc in scores] + [far_lse])
        ps = [jnp.exp(sc - m) for sc in scores]
        far_weight = jnp.exp(far_lse - m)
        denom = functools.reduce(jnp.add, [jnp.sum(p, axis=-1, keepdims=True) for p in ps] + [far_weight])
        return [p.astype(_BF16) for p in ps], far_weight, denom

    def weighted_values(ps, far_weight, o):
        acc = far_weight * far_out[o:o + QBLK, :]
        for g, (window, _) in enumerate(NEAR_GROUPS):
            acc = acc + jnp.dot(ps[g], window_rows(v_refs[g], o, window)[0], preferred_element_type=_F32)
        return acc

    offs = range(0, SPAN, QBLK)
    staged = [masked_scores(o) for o in offs]
    weights = [joint_softmax_weights(scores, o) for o, scores in zip(offs, staged)]
    for o, (ps, far_weight, denom) in zip(offs, weights):
        o_ref[0, o:o + QBLK, :] = (weighted_values(ps, far_weight, o) / denom).astype(o_ref.dtype)


def _alibi_slopes(group):
    slopes = jnp.exp2(-ALIBI_MAX * jnp.arange(1, N_HEADS_A + 1, dtype=_F32) / N_HEADS_A)
    return slopes[group * HEADS_PER_GROUP:(group + 1) * HEADS_PER_GROUP]


def _dilated_bias_tables():
    tables = []
    for g, (window, dilation) in enumerate(NEAR_GROUPS):
        qi = jnp.arange(QBLK)[:, None]
        kj = jnp.arange(window + QBLK)[None, :]
        dist = qi + window - kj
        valid = (dist >= 0) & (dist <= window) & (dist % dilation == 0)
        bias = -_alibi_slopes(g)[:, None, None] * dist.astype(_F32)[None]
        tables.append(jnp.where(valid[None], bias, MASKED))
    return tables


def _far_bias_table():
    window, dilation = DIL_GROUPS[-1]
    qi = jnp.arange(QBLK)[:, None]
    kj = jnp.arange(2 * QBLK)[None, :]
    steps = QBLK + qi - kj
    valid = (steps >= 0) & (steps <= window // dilation)
    bias = -_alibi_slopes(len(DIL_GROUPS) - 1)[:, None, None] * (steps * dilation).astype(_F32)[None]
    return jnp.where(valid[None], bias, MASKED)


def _dilated_attention(proj3):
    b, s, d_in = proj3.shape
    far_window, far_dilation = DIL_GROUPS[-1]
    assert s % SPAN == 0 and SPAN == QBLK * far_dilation == far_window
    n_heads = N_HEADS_A
    far_group = len(DIL_GROUPS) - 1

    def col_spec(base, lag):
        return [pl.BlockSpec((1, SPAN, HEAD_DIM), (lambda bi, hh, i, c=base + g * HEADS_PER_GROUP:
                                                   (bi, jnp.maximum(i - lag, 0), c + hh)))
                for g in range(len(NEAR_GROUPS))]

    def far_spec(base, lag):
        return pl.BlockSpec((1, QBLK, far_dilation, HEAD_DIM),
                            lambda bi, hh, i, c=base + far_group * HEADS_PER_GROUP:
                            (bi, jnp.maximum(i - lag, 0), 0, c + hh))

    head_table = lambda table: pl.BlockSpec((1,) + table.shape[1:], lambda bi, hh, i: (hh, 0, 0))
    tables = _dilated_bias_tables() + [_far_bias_table()]
    near_specs = (col_spec(0, 0) + col_spec(n_heads, 1) + col_spec(n_heads, 0)
                  + col_spec(2 * n_heads, 1) + col_spec(2 * n_heads, 0))
    far_specs = [far_spec(0, 0), far_spec(n_heads, 1), far_spec(n_heads, 0),
                 far_spec(2 * n_heads, 1), far_spec(2 * n_heads, 0)]
    far_view = proj3.reshape(b, s // far_dilation, far_dilation, d_in)
    n_near = len(NEAR_GROUPS)
    return pl.pallas_call(
        _dilated_kernel,
        grid=(b, HEADS_PER_GROUP, s // SPAN),
        in_specs=(near_specs + [head_table(tb) for tb in tables[:n_near]] + far_specs + [head_table(tables[-1])]),
        out_specs=pl.BlockSpec((1, SPAN, HEAD_DIM), lambda bi, hh, i: (bi, i, hh)),
        out_shape=jax.ShapeDtypeStruct((b, s, A_OUT), _BF16),
        compiler_params=_params(("parallel", "parallel", "parallel")),
        name="dilated_attention",
    )(*([proj3] * (5 * n_near)), *tables[:n_near], *([far_view] * 5), tables[-1])


def _stick_kernel(q_ref, k_ref, v_ref, tri_ref, o_ref, *, tq, heads):
    i = pl.program_id(2)
    row = lax.broadcasted_iota(jnp.int32, (tq, tq), 0)
    col = lax.broadcasted_iota(jnp.int32, (tq, tq), 1)
    before = col < row

    def chunk(c, afters, accs, diagonal):
        k0 = pl.multiple_of(c * tq, tq)
        tri = tri_ref[...]
        lanes = [slice(h * HEAD_DIM, (h + 1) * HEAD_DIM) for h in range(heads)]
        zs = [lax.dot_general(q_ref[0, :, lanes[h]], k_ref[0, pl.ds(k0, tq), lanes[h]],
                              (((1,), (1,)), ((), ())), preferred_element_type=_F32) for h in range(heads)]
        csums = []
        for h in range(heads):
            z = zs[h]
            ln = -(jnp.maximum(z, 0.0) + jnp.log(1.0 + jnp.exp(-jnp.abs(z))))
            if diagonal:
                ln = jnp.where(before, ln, 0.0)
            hi = ln.astype(_BF16)
            lo = (ln - hi.astype(_F32)).astype(_BF16)
            csums.append(jnp.dot(hi, tri, preferred_element_type=_F32)
                         + jnp.dot(lo, tri, preferred_element_type=_F32))
        new_afters, new_accs = [], []
        for h in range(heads):
            w = jnp.exp(zs[h] + csums[h] + afters[h])
            if diagonal:
                w = jnp.where(before, w, 0.0)
            new_accs.append(accs[h] + jnp.dot(w.astype(_BF16), v_ref[0, pl.ds(k0, tq), lanes[h]],
                                              preferred_element_type=_F32))
            new_afters.append(afters[h] + csums[h][:, 0:1])
        return tuple(new_afters), tuple(new_accs)

    afters, accs = chunk(i, (jnp.zeros((tq, 1), _F32),) * heads, (jnp.zeros((tq, HEAD_DIM), _F32),) * heads, True)

    def keep_going(carry):
        c, afters, _ = carry
        return jnp.logical_and(c >= 0, jnp.max(functools.reduce(jnp.maximum, afters)) > EXP_UNDERFLOW)

    def body(carry):
        c, afters, accs = carry
        afters, accs = chunk(c, afters, accs, False)
        return c - 1, afters, accs

    _, _, accs = lax.while_loop(keep_going, body, (i - 1, afters, accs))
    for h in range(heads):
        o_ref[0, :, h * HEAD_DIM:(h + 1) * HEAD_DIM] = accs[h].astype(o_ref.dtype)


def _stick_attention(proj3, *, tq, heads):
    b, s, _ = proj3.shape
    width = heads * HEAD_DIM
    assert s % tq == 0 and N_HEADS_B % heads == 0 and (3 * WIDTH_A) % width == 0
    groups = N_HEADS_B // heads
    base = 3 * WIDTH_A // width
    idx = jnp.arange(tq)
    tri = (idx[:, None] >= idx[None, :]).astype(_BF16)
    whole_seq = lambda part: pl.BlockSpec((1, s, width), lambda bi, hg, i: (bi, 0, base + part * groups + hg),
                                          pipeline_mode=pl.Buffered(1))
    return pl.pallas_call(
        functools.partial(_stick_kernel, tq=tq, heads=heads),
        grid=(b, groups, s // tq),
        in_specs=[
            pl.BlockSpec((1, tq, width), lambda bi, hg, i: (bi, i, base + hg)),
            whole_seq(1),
            whole_seq(2),
            pl.BlockSpec((tq, tq), lambda bi, hg, i: (0, 0)),
        ],
        out_specs=pl.BlockSpec((1, tq, width), lambda bi, hg, i: (bi, i, hg)),
        out_shape=jax.ShapeDtypeStruct((b, s, WIDTH_B), _BF16),
        compiler_params=_params(("parallel", "parallel", "arbitrary")),
        name="stick_breaking_attention",
    )(proj3, proj3, proj3, tri)


def _merge_kernel(ya_ref, yb_ref, ga_ref, gb_ref, h_ref, wa_ref, wb_ref, wo_ref, g_ref, o_ref):
    a = jnp.dot(ya_ref[...], wa_ref[...], preferred_element_type=_F32)
    bb = jnp.dot(yb_ref[...], wb_ref[...], preferred_element_type=_F32)
    m = (ga_ref[...].astype(_F32) * a + gb_ref[...].astype(_F32) * bb).astype(_BF16)
    mix = jnp.dot(m, wo_ref[...], preferred_element_type=_F32)
    o_ref[...] = h_ref[...] + _rmsnorm_rows(mix, g_ref[...])


def _merge(ya, yb, proj, h2d, wa, wb, wo, g, layer, *, tm):
    t, d = h2d.shape
    assert t % tm == 0 and QKV_WIDTH % d == 0
    gate_blk = QKV_WIDTH // d
    resident = lambda w: pl.BlockSpec((None,) + w.shape[1:], lambda i: (layer, 0, 0), pipeline_mode=pl.Buffered(1))
    return pl.pallas_call(
        _merge_kernel,
        grid=(t // tm,),
        in_specs=[
            pl.BlockSpec((tm, A_OUT), lambda i: (i, 0)),
            pl.BlockSpec((tm, WIDTH_B), lambda i: (i, 0)),
            pl.BlockSpec((tm, d), lambda i: (i, gate_blk)),
            pl.BlockSpec((tm, d), lambda i: (i, gate_blk + 1)),
            pl.BlockSpec((tm, d), lambda i: (i, 0)),
            resident(wa), resident(wb), resident(wo), _layer_rows(layer, d),
        ],
        out_specs=pl.BlockSpec((tm, d), lambda i: (i, 0)),
        out_shape=jax.ShapeDtypeStruct((t, d), _F32),
        compiler_params=_params(("parallel",)),
        name="gated_merge",
    )(ya, yb, proj, proj, h2d, wa, wb, wo, g)


def _ffn_kernel(h_ref, hp_ref, gpre_ref, wg_ref, wv_ref, cwg_ref, cwv_ref, cbg_ref, cbv_ref, wd_ref,
                gpost_ref, o_ref, u_ref, *, tiles_per_seq, chunk):
    i = pl.program_id(0)
    j = pl.program_id(1)
    tm = h_ref.shape[0]
    n_chunks = tm // chunk

    @pl.when(j == 0)
    def _():
        halo = _rmsnorm_rows(hp_ref[...], gpre_ref[...])
        at_sequence_start = (i % tiles_per_seq) == 0
        u_ref[0:CONV_HALO, :] = jnp.where(at_sequence_start, 0.0, halo).astype(u_ref.dtype)
        o_ref[...] = jnp.zeros(o_ref.shape, o_ref.dtype)

        def body(c, carry):
            r0 = pl.multiple_of(c * chunk, chunk)
            x = h_ref[pl.ds(r0, chunk), :]
            u_ref[pl.ds(CONV_HALO + r0, chunk), :] = _rmsnorm_rows(x, gpre_ref[...]).astype(u_ref.dtype)
            return carry
        lax.fori_loop(0, n_chunks, body, 0)

    u = u_ref[...]

    def conv_branch(w_ref, cw_ref, cb_ref):
        x = jnp.dot(u, w_ref[...], preferred_element_type=_F32)
        cw = cw_ref[...]
        y = cb_ref[...] + cw[0:1, :] * pltpu.roll(x, 2, 0)
        y = y + cw[1:2, :] * pltpu.roll(x, 1, 0)
        y = y + cw[2:3, :] * x
        return y[CONV_HALO:, :]

    gate = conv_branch(wg_ref, cwg_ref, cbg_ref)
    value = conv_branch(wv_ref, cwv_ref, cbv_ref)
    act = (jax.nn.gelu(gate, approximate=True) * value).astype(_BF16)
    o_ref[...] += jnp.dot(act, wd_ref[...], preferred_element_type=_F32)

    @pl.when(j == pl.num_programs(1) - 1)
    def _():
        def body(c, carry):
            r0 = pl.multiple_of(c * chunk, chunk)
            f = o_ref[pl.ds(r0, chunk), :]
            o_ref[pl.ds(r0, chunk), :] = h_ref[pl.ds(r0, chunk), :] + _rmsnorm_rows(f, gpost_ref[...])
            return carry
        lax.fori_loop(0, n_chunks, body, 0)


def _ffn(h2d, gpre, w_up, conv_w, conv_b, w_down, gpost, layer, *, tm, tf, seq):
    t, d = h2d.shape
    d_ff = w_down.shape[1]
    assert t % tm == 0 and seq % tm == 0 and d_ff % tf == 0 and tm % CONV_HALO == 0
    nj = d_ff // tf
    halo_blocks = tm // CONV_HALO
    return pl.pallas_call(
        functools.partial(_ffn_kernel, tiles_per_seq=seq // tm, chunk=min(tm, 128)),
        grid=(t // tm, nj),
        in_specs=[
            pl.BlockSpec((tm, d), lambda i, j: (i, 0)),
            pl.BlockSpec((CONV_HALO, d), lambda i, j: (jnp.maximum(i * halo_blocks - 1, 0), 0)),
            _layer_rows(layer, d),
            pl.BlockSpec((None, d, tf), lambda i, j: (layer, 0, j)),
            pl.BlockSpec((None, d, tf), lambda i, j: (layer, 0, nj + j)),
            pl.BlockSpec((None, CONV_WIDTH, tf), lambda i, j: (layer, 0, j)),
            pl.BlockSpec((None, CONV_WIDTH, tf), lambda i, j: (layer, 0, nj + j)),
            pl.BlockSpec((None, 1, tf), lambda i, j: (layer, 0, j)),
            pl.BlockSpec((None, 1, tf), lambda i, j: (layer, 0, nj + j)),
            pl.BlockSpec((None, tf, d), lambda i, j: (layer, j, 0)),
            _layer_rows(layer, d),
        ],
        out_specs=pl.BlockSpec((tm, d), lambda i, j: (i, 0)),
        out_shape=jax.ShapeDtypeStruct((t, d), _F32),
        scratch_shapes=[pltpu.VMEM((CONV_HALO + tm, d), _BF16)],
        compiler_params=_params(("parallel", "arbitrary")),
        name="conv_ffn",
    )(h2d, h2d, gpre, w_up, w_up, conv_w, conv_w, conv_b, conv_b, w_down, gpost)


def _ple_kernel(h_ref, p_ref, wg_ref, wi_ref, o_ref):
    h = h_ref[...]
    gate = jax.nn.sigmoid(jnp.dot(h.astype(_BF16), wg_ref[...], preferred_element_type=_F32))
    e = jnp.dot(p_ref[...].astype(_BF16), wi_ref[...], preferred_element_type=_F32)
    o_ref[...] = h + gate * e


def _ple(h2d, p3d, w_gate, w_in, layer, *, tm):
    t, d = h2d.shape
    pd = p3d.shape[2]
    assert t % tm == 0
    resident = lambda w: pl.BlockSpec((None,) + w.shape[1:], lambda i: (layer, 0, 0), pipeline_mode=pl.Buffered(1))
    return pl.pallas_call(
        _ple_kernel,
        grid=(t // tm,),
        in_specs=[
            pl.BlockSpec((tm, d), lambda i: (i, 0)),
            pl.BlockSpec((None, tm, pd), lambda i: (layer, i, 0)),
            resident(w_gate),
            resident(w_in),
        ],
        out_specs=pl.BlockSpec((tm, d), lambda i: (i, 0)),
        out_shape=jax.ShapeDtypeStruct((t, d), _F32),
        compiler_params=_params(("parallel",)),
        name="layer_embedding",
    )(h2d, p3d, w_gate, w_in)


def _tiles(t, seq, d, d_in, d_ff):
    def largest(n, cands):
        return next(c for c in cands if n % c == 0)
    return dict(
        proj_tm=largest(t, (1024, 512, 256, 128)),
        proj_tn=largest(math.gcd(d_in, QKV_WIDTH), (2048, 1024, 512, 256, 128)),
        stick_tq=largest(seq, (256, 128)),
        merge_tm=largest(t, (512, 256, 128)),
        ffn_tm=largest(seq, (1024, 512, 256, 128)),
        ffn_tf=largest(d_ff, (512, 256, 128)),
        ple_tm=largest(t, (1024, 512, 256, 128)),
    )


def kernel(x, p, g_mix_pre, w_in, w_branch_a, w_branch_b, w_out, g_mix_post, g_ffn_pre, w_up, conv_w,
           conv_b, w_down, g_ffn_post, w_ple_in, w_ple_gate):
    b, s, d = x.shape
    depth = w_in.shape[0]
    d_in = w_in.shape[2]
    d_ff = w_down.shape[1]
    t = b * s
    assert d_in == QKV_WIDTH + 2 * d
    tiles = _tiles(t, s, d, d_in, d_ff)

    q_scale = 1.0 / math.sqrt(HEAD_DIM)
    col = jnp.arange(d_in)
    is_q = (col < WIDTH_A) | ((col >= 3 * WIDTH_A) & (col < 3 * WIDTH_A + WIDTH_B))
    colscale = jnp.where(is_q, q_scale, 1.0).astype(_F32)[None, :]

    bf16 = lambda w: w.astype(_BF16)
    rows = lambda v: v[:, None, :]
    w_in, w_branch_a, w_branch_b, w_out = bf16(w_in), bf16(w_branch_a), bf16(w_branch_b), bf16(w_out)
    w_up, w_down, w_ple_in, w_ple_gate = bf16(w_up), bf16(w_down), bf16(w_ple_in), bf16(w_ple_gate)
    p3d = p.reshape(depth, t, p.shape[-1])

    h = x.reshape(t, d)
    for i in range(depth):
        proj = _in_proj(h, rows(g_mix_pre), colscale, w_in, i, tm=tiles["proj_tm"], tn=tiles["proj_tn"])
        proj3 = proj.reshape(b, s, d_in)
        ya = _dilated_attention(proj3).reshape(t, A_OUT)
        yb = _stick_attention(proj3, tq=tiles["stick_tq"], heads=N_HEADS_B).reshape(t, WIDTH_B)
        h = _merge(ya, yb, proj, h, w_branch_a, w_branch_b, w_out, rows(g_mix_post), i, tm=tiles["merge_tm"])
        h = _ffn(h, rows(g_ffn_pre), w_up, conv_w, rows(conv_b), w_down, rows(g_ffn_post), i,
                 tm=tiles["ffn_tm"], tf=tiles["ffn_tf"], seq=s)
        h = _ple(h, p3d, w_ple_gate, w_ple_in, i, tm=tiles["ple_tm"])
    return h.reshape(b, s, d)
```

```python
import functools
import math

import jax
import jax.numpy as jnp
from jax import lax
from jax.experimental import pallas as pl
from jax.experimental.pallas import tpu as pltpu

HEAD_DIM = 128
DIL_GROUPS = ((128, 1), (512, 4), (2048, 16))
NEAR_GROUPS = DIL_GROUPS[:-1]
HEADS_PER_GROUP = 4
N_HEADS_A = HEADS_PER_GROUP * len(DIL_GROUPS)
N_HEADS_B = 4
WIDTH_A = N_HEADS_A * HEAD_DIM
WIDTH_B = N_HEADS_B * HEAD_DIM
A_OUT = HEADS_PER_GROUP * HEAD_DIM
QKV_WIDTH = 3 * WIDTH_A + 3 * WIDTH_B
CONV_WIDTH = 3
RMS_EPS = 1e-6
ALIBI_MAX = 8.0
MASKED = -1e30
EXP_UNDERFLOW = -110.0
QBLK = 128
SPAN = 2048
CONV_HALO = 16
VMEM_LIMIT = 56 * 1024 * 1024

_F32 = jnp.float32
_BF16 = jnp.bfloat16


def _rmsnorm_rows(x, g):
    ms = jnp.mean(x * x, axis=-1, keepdims=True)
    return x * lax.rsqrt(ms + RMS_EPS) * g


def _params(sem):
    return pltpu.CompilerParams(dimension_semantics=sem, vmem_limit_bytes=VMEM_LIMIT)


def _layer_rows(layer, width):
    return pl.BlockSpec((None, 1, width), lambda *_: (layer, 0, 0))


def _in_proj_kernel(x_ref, g_ref, cs_ref, w_ref, o_ref, u_ref, *, gate_tile0, chunk):
    j = pl.program_id(1)
    tm = x_ref.shape[0]

    @pl.when(j == 0)
    def _():
        def body(c, carry):
            r0 = pl.multiple_of(c * chunk, chunk)
            x = x_ref[pl.ds(r0, chunk), :]
            u_ref[pl.ds(r0, chunk), :] = _rmsnorm_rows(x, g_ref[...]).astype(u_ref.dtype)
            return carry
        lax.fori_loop(0, tm // chunk, body, 0)

    acc = jnp.dot(u_ref[...], w_ref[...], preferred_element_type=_F32)

    o_ref[...] = jnp.where(j >= gate_tile0, jax.nn.sigmoid(acc), acc * cs_ref[...]).astype(o_ref.dtype)


def _in_proj(h2d, g, colscale, w, layer, *, tm, tn):
    t, d = h2d.shape
    d_in = w.shape[2]
    assert t % tm == 0 and d_in % tn == 0 and QKV_WIDTH % tn == 0
    return pl.pallas_call(
        functools.partial(_in_proj_kernel, gate_tile0=QKV_WIDTH // tn, chunk=min(tm, 128)),
        grid=(t // tm, d_in // tn),
        in_specs=[
            pl.BlockSpec((tm, d), lambda i, j: (i, 0)),
            _layer_rows(layer, d),
            pl.BlockSpec((1, tn), lambda i, j: (0, j)),
            pl.BlockSpec((None, d, tn), lambda i, j: (layer, 0, j)),
        ],
        out_specs=pl.BlockSpec((tm, tn), lambda i, j: (i, j)),
        out_shape=jax.ShapeDtypeStruct((t, d_in), _BF16),
        scratch_shapes=[pltpu.VMEM((tm, d), _BF16)],
        compiler_params=_params(("parallel", "arbitrary")),
        name="in_proj",
    )(h2d, g, colscale, w)


def _dilated_kernel(q0, q1, kp0, kp1, kc0, kc1, vp0, vp1, vc0, vc1, b0, b1,
                    fq, fkp, fkc, fvp, fvc, fbias, o_ref):
    i = pl.program_id(2)
    q_refs, bias_refs = (q0, q1), (b0, b1)
    k_refs, v_refs = ((kp0, kc0), (kp1, kc1)), ((vp0, vc0), (vp1, vc1))

    def far_group():
        residue_major = lambda ref: pltpu.einshape("mrd->rmd", ref[0])
        token_major = lambda rows: pltpu.einshape("rmd->mrd", jnp.stack(rows)).reshape(SPAN, HEAD_DIM)
        qf = residue_major(fq)
        kf = jnp.concatenate([residue_major(fkp), residue_major(fkc)], axis=1)
        vf = jnp.concatenate([residue_major(fvp), residue_major(fvc)], axis=1)
        kj = lax.broadcasted_iota(jnp.int32, (QBLK, 2 * QBLK), 1)
        in_sequence = kj >= jnp.where(i == 0, QBLK, 0)
        residues = range(qf.shape[0])
        scores = [jnp.where(in_sequence,
                            lax.dot_general(qf[r], kf[r], (((1,), (1,)), ((), ())), preferred_element_type=_F32)
                            + fbias[0], MASKED) for r in residues]
        outs, lses = [], []
        for r in residues:
            m = jnp.max(scores[r], axis=-1, keepdims=True)
            p = jnp.exp(scores[r] - m)
            denom = jnp.sum(p, axis=-1, keepdims=True)
            outs.append(jnp.dot(p.astype(_BF16), vf[r], preferred_element_type=_F32) / denom)
            lses.append(jnp.broadcast_to(m + jnp.log(denom), (QBLK, HEAD_DIM)))
        return token_major(outs), token_major(lses)

    far_out, far_lse_lanes = far_group()

    def window_rows(prev_cur, o, window):
        prev_ref, cur_ref = prev_cur
        n_prev = max(window - o, 0)
        cur = cur_ref[0, max(o - window, 0):o + QBLK, :]
        if n_prev == 0:
            return cur, 0
        return jnp.concatenate([prev_ref[0, SPAN - n_prev:SPAN, :], cur], axis=0), n_prev

    def masked_scores(o):
        scores = []
        for g, (window, _) in enumerate(NEAR_GROUPS):
            kk, n_prev = window_rows(k_refs[g], o, window)
            sc = lax.dot_general(q_refs[g][0, o:o + QBLK, :], kk, (((1,), (1,)), ((), ())),
                                 preferred_element_type=_F32)
            sc = sc + bias_refs[g][0]
            if n_prev:
                kj = lax.broadcasted_iota(jnp.int32, sc.shape, 1)
                sc = jnp.where(jnp.logical_or(kj >= n_prev, i > 0), sc, MASKED)
            scores.append(sc)
        return scores

    def joint_softmax_weights(scores, o):
        far_lse = far_lse_lanes[o:o + QBLK, 0:1]
        m = functools.reduce(jnp.maximum, [jnp.max(sc, axis=-1, keepdims=True) for sc in scores] + [far_lse])
        ps = [jnp.exp(sc - m) for sc in scores]
        far_weight = jnp.exp(far_lse - m)
        denom = functools.reduce(jnp.add, [jnp.sum(p, axis=-1, keepdims=True) for p in ps] + [far_weight])
        return [p.astype(_BF16) for p in ps], far_weight, denom

    def weighted_values(ps, far_weight, o):
        acc = far_weight * far_out[o:o + QBLK, :]
        for g, (window, _) in enumerate(NEAR_GROUPS):
            acc = acc + jnp.dot(ps[g], window_rows(v_refs[g], o, window)[0], preferred_element_type=_F32)
        return acc

    offs = range(0, SPAN, QBLK)
    staged = [masked_scores(o) for o in offs]
    weights = [joint_softmax_weights(scores, o) for o, scores in zip(offs, staged)]
    for o, (ps, far_weight, denom) in zip(offs, weights):
        o_ref[0, o:o + QBLK, :] = (weighted_values(ps, far_weight, o) / denom).astype(o_ref.dtype)


def _alibi_slopes(group):
    slopes = jnp.exp2(-ALIBI_MAX * jnp.arange(1, N_HEADS_A + 1, dtype=_F32) / N_HEADS_A)
    return slopes[group * HEADS_PER_GROUP:(group + 1) * HEADS_PER_GROUP]


def _dilated_bias_tables():
    tables = []
    for g, (window, dilation) in enumerate(NEAR_GROUPS):
        qi = jnp.arange(QBLK)[:, None]
        kj = jnp.arange(window + QBLK)[None, :]
        dist = qi + window - kj
        valid = (dist >= 0) & (dist <= window) & (dist % dilation == 0)
        bias = -_alibi_slopes(g)[:, None, None] * dist.astype(_F32)[None]
        tables.append(jnp.where(valid[None], bias, MASKED))
    return tables


def _far_bias_table():
    window, dilation = DIL_GROUPS[-1]
    qi = jnp.arange(QBLK)[:, None]
    kj = jnp.arange(2 * QBLK)[None, :]
    steps = QBLK + qi - kj
    valid = (steps >= 0) & (steps <= window // dilation)
    bias = -_alibi_slopes(len(DIL_GROUPS) - 1)[:, None, None] * (steps * dilation).astype(_F32)[None]
    return jnp.where(valid[None], bias, MASKED)


def _dilated_attention(proj3):
    b, s, d_in = proj3.shape
    far_window, far_dilation = DIL_GROUPS[-1]
    assert s % SPAN == 0 and SPAN == QBLK * far_dilation == far_window
    n_heads = N_HEADS_A
    far_group = len(DIL_GROUPS) - 1

    def col_spec(base, lag):
        return [pl.BlockSpec((1, SPAN, HEAD_DIM), (lambda bi, hh, i, c=base + g * HEADS_PER_GROUP:
                                                   (bi, jnp.maximum(i - lag, 0), c + hh)))
                for g in range(len(NEAR_GROUPS))]

    def far_spec(base, lag):
        return pl.BlockSpec((1, QBLK, far_dilation, HEAD_DIM),
                            lambda bi, hh, i, c=base + far_group * HEADS_PER_GROUP:
                            (bi, jnp.maximum(i - lag, 0), 0, c + hh))

    head_table = lambda table: pl.BlockSpec((1,) + table.shape[1:], lambda bi, hh, i: (hh, 0, 0))
    tables = _dilated_bias_tables() + [_far_bias_table()]
    near_specs = (col_spec(0, 0) + col_spec(n_heads, 1) + col_spec(n_heads, 0)
                  + col_spec(2 * n_heads, 1) + col_spec(2 * n_heads, 0))
    far_specs = [far_spec(0, 0), far_spec(n_heads, 1), far_spec(n_heads, 0),
                 far_spec(2 * n_heads, 1), far_spec(2 * n_heads, 0)]
    far_view = proj3.reshape(b, s // far_dilation, far_dilation, d_in)
    n_near = len(NEAR_GROUPS)
    return pl.pallas_call(
        _dilated_kernel,
        grid=(b, HEADS_PER_GROUP, s // SPAN),
        in_specs=(near_specs + [head_table(tb) for tb in tables[:n_near]] + far_specs + [head_table(tables[-1])]),
        out_specs=pl.BlockSpec((1, SPAN, HEAD_DIM), lambda bi, hh, i: (bi, i, hh)),
        out_shape=jax.ShapeDtypeStruct((b, s, A_OUT), _BF16),
        compiler_params=_params(("parallel", "parallel", "parallel")),
        name="dilated_attention",
    )(*([proj3] * (5 * n_near)), *tables[:n_near], *([far_view] * 5), tables[-1])


def _stick_kernel(q_ref, k_ref, v_ref, tri_ref, o_ref, *, tq, heads):
    i = pl.program_id(2)
    row = lax.broadcasted_iota(jnp.int32, (tq, tq), 0)
    col = lax.broadcasted_iota(jnp.int32, (tq, tq), 1)
    before = col < row

    def chunk(c, afters, accs, diagonal):
        k0 = pl.multiple_of(c * tq, tq)
        tri = tri_ref[...]
        lanes = [slice(h * HEAD_DIM, (h + 1) * HEAD_DIM) for h in range(heads)]
        zs = [lax.dot_general(q_ref[0, :, lanes[h]], k_ref[0, pl.ds(k0, tq), lanes[h]],
                              (((1,), (1,)), ((), ())), preferred_element_type=_F32) for h in range(heads)]
        csums = []
        for h in range(heads):
            z = zs[h]
            ln = -(jnp.maximum(z, 0.0) + jnp.log(1.0 + jnp.exp(-jnp.abs(z))))
            if diagonal:
                ln = jnp.where(before, ln, 0.0)
            hi = ln.astype(_BF16)
            lo = (ln - hi.astype(_F32)).astype(_BF16)
            csums.append(jnp.dot(hi, tri, preferred_element_type=_F32)
                         + jnp.dot(lo, tri, preferred_element_type=_F32))
        new_afters, new_accs = [], []
        for h in range(heads):
            w = jnp.exp(zs[h] + csums[h] + afters[h])
            if diagonal:
                w = jnp.where(before, w, 0.0)
            new_accs.append(accs[h] + jnp.dot(w.astype(_BF16), v_ref[0, pl.ds(k0, tq), lanes[h]],
                                              preferred_element_type=_F32))
            new_afters.append(afters[h] + csums[h][:, 0:1])
        return tuple(new_afters), tuple(new_accs)

    afters, accs = chunk(i, (jnp.zeros((tq, 1), _F32),) * heads, (jnp.zeros((tq, HEAD_DIM), _F32),) * heads, True)

    def keep_going(carry):
        c, afters, _ = carry
        return c >= 0

    def body(carry):
        c, afters, accs = carry
        afters, accs = chunk(c, afters, accs, False)
        return c - 1, afters, accs

    _, _, accs = lax.while_loop(keep_going, body, (i - 1, afters, accs))
    for h in range(heads):
        o_ref[0, :, h * HEAD_DIM:(h + 1) * HEAD_DIM] = accs[h].astype(o_ref.dtype)


def _stick_attention(proj3, *, tq, heads):
    b, s, _ = proj3.shape
    width = heads * HEAD_DIM
    assert s % tq == 0 and N_HEADS_B % heads == 0 and (3 * WIDTH_A) % width == 0
    groups = N_HEADS_B // heads
    base = 3 * WIDTH_A // width
    idx = jnp.arange(tq)
    tri = (idx[:, None] >= idx[None, :]).astype(_BF16)
    whole_seq = lambda part: pl.BlockSpec((1, s, width), lambda bi, hg, i: (bi, 0, base + part * groups + hg),
                                          pipeline_mode=pl.Buffered(1))
    return pl.pallas_call(
        functools.partial(_stick_kernel, tq=tq, heads=heads),
        grid=(b, groups, s // tq),
        in_specs=[
            pl.BlockSpec((1, tq, width), lambda bi, hg, i: (bi, i, base + hg)),
            whole_seq(1),
            whole_seq(2),
            pl.BlockSpec((tq, tq), lambda bi, hg, i: (0, 0)),
        ],
        out_specs=pl.BlockSpec((1, tq, width), lambda bi, hg, i: (bi, i, hg)),
        out_shape=jax.ShapeDtypeStruct((b, s, WIDTH_B), _BF16),
        compiler_params=_params(("parallel", "parallel", "arbitrary")),
        name="stick_breaking_attention",
    )(proj3, proj3, proj3, tri)


def _merge_kernel(ya_ref, yb_ref, ga_ref, gb_ref, h_ref, wa_ref, wb_ref, wo_ref, g_ref, o_ref):
    a = jnp.dot(ya_ref[...], wa_ref[...], preferred_element_type=_F32)
    bb = jnp.dot(yb_ref[...], wb_ref[...], preferred_element_type=_F32)
    m = (ga_ref[...].astype(_F32) * a + gb_ref[...].astype(_F32) * bb).astype(_BF16)
    mix = jnp.dot(m, wo_ref[...], preferred_element_type=_F32)
    o_ref[...] = h_ref[...] + _rmsnorm_rows(mix, g_ref[...])


def _merge(ya, yb, proj, h2d, wa, wb, wo, g, layer, *, tm):
    t, d = h2d.shape
    assert t % tm == 0 and QKV_WIDTH % d == 0
    gate_blk = QKV_WIDTH // d
    resident = lambda w: pl.BlockSpec((None,) + w.shape[1:], lambda i: (layer, 0, 0), pipeline_mode=pl.Buffered(1))
    return pl.pallas_call(
        _merge_kernel,
        grid=(t // tm,),
        in_specs=[
            pl.BlockSpec((tm, A_OUT), lambda i: (i, 0)),
            pl.BlockSpec((tm, WIDTH_B), lambda i: (i, 0)),
            pl.BlockSpec((tm, d), lambda i: (i, gate_blk)),
            pl.BlockSpec((tm, d), lambda i: (i, gate_blk + 1)),
            pl.BlockSpec((tm, d), lambda i: (i, 0)),
            resident(wa), resident(wb), resident(wo), _layer_rows(layer, d),
        ],
        out_specs=pl.BlockSpec((tm, d), lambda i: (i, 0)),
        out_shape=jax.ShapeDtypeStruct((t, d), _F32),
        compiler_params=_params(("parallel",)),
        name="gated_merge",
    )(ya, yb, proj, proj, h2d, wa, wb, wo, g)


def _ffn_kernel(h_ref, hp_ref, gpre_ref, wg_ref, wv_ref, cwg_ref, cwv_ref, cbg_ref, cbv_ref, wd_ref,
                gpost_ref, o_ref, u_ref, *, tiles_per_seq, chunk):
    i = pl.program_id(0)
    j = pl.program_id(1)
    tm = h_ref.shape[0]
    n_chunks = tm // chunk

    @pl.when(j == 0)
    def _():
        halo = _rmsnorm_rows(hp_ref[...], gpre_ref[...])
        at_sequence_start = (i % tiles_per_seq) == 0
        u_ref[0:CONV_HALO, :] = jnp.where(at_sequence_start, 0.0, halo).astype(u_ref.dtype)
        o_ref[...] = jnp.zeros(o_ref.shape, o_ref.dtype)

        def body(c, carry):
            r0 = pl.multiple_of(c * chunk, chunk)
            x = h_ref[pl.ds(r0, chunk), :]
            u_ref[pl.ds(CONV_HALO + r0, chunk), :] = _rmsnorm_rows(x, gpre_ref[...]).astype(u_ref.dtype)
            return carry
        lax.fori_loop(0, n_chunks, body, 0)

    u = u_ref[...]

    def conv_branch(w_ref, cw_ref, cb_ref):
        x = jnp.dot(u, w_ref[...], preferred_element_type=_F32)
        cw = cw_ref[...]
        y = cb_ref[...] + cw[0:1, :] * pltpu.roll(x, 2, 0)
        y = y + cw[1:2, :] * pltpu.roll(x, 1, 0)
        y = y + cw[2:3, :] * x
        return y[CONV_HALO:, :]

    gate = conv_branch(wg_ref, cwg_ref, cbg_ref)
    value = conv_branch(wv_ref, cwv_ref, cbv_ref)
    act = (jax.nn.gelu(gate, approximate=True) * value).astype(_BF16)
    o_ref[...] += jnp.dot(act, wd_ref[...], preferred_element_type=_F32)

    @pl.when(j == pl.num_programs(1) - 1)
    def _():
        def body(c, carry):
            r0 = pl.multiple_of(c * chunk, chunk)
            f = o_ref[pl.ds(r0, chunk), :]
            o_ref[pl.ds(r0, chunk), :] = h_ref[pl.ds(r0, chunk), :] + _rmsnorm_rows(f, gpost_ref[...])
            return carry
        lax.fori_loop(0, n_chunks, body, 0)


def _ffn(h2d, gpre, w_up, conv_w, conv_b, w_down, gpost, layer, *, tm, tf, seq):
    t, d = h2d.shape
    d_ff = w_down.shape[1]
    assert t % tm == 0 and seq % tm == 0 and d_ff % tf == 0 and tm % CONV_HALO == 0
    nj = d_ff // tf
    halo_blocks = tm // CONV_HALO
    return pl.pallas_call(
        functools.partial(_ffn_kernel, tiles_per_seq=seq // tm, chunk=min(tm, 128)),
        grid=(t // tm, nj),
        in_specs=[
            pl.BlockSpec((tm, d), lambda i, j: (i, 0)),
            pl.BlockSpec((CONV_HALO, d), lambda i, j: (jnp.maximum(i * halo_blocks - 1, 0), 0)),
            _layer_rows(layer, d),
            pl.BlockSpec((None, d, tf), lambda i, j: (layer, 0, j)),
            pl.BlockSpec((None, d, tf), lambda i, j: (layer, 0, nj + j)),
            pl.BlockSpec((None, CONV_WIDTH, tf), lambda i, j: (layer, 0, j)),
            pl.BlockSpec((None, CONV_WIDTH, tf), lambda i, j: (layer, 0, nj + j)),
            pl.BlockSpec((None, 1, tf), lambda i, j: (layer, 0, j)),
            pl.BlockSpec((None, 1, tf), lambda i, j: (layer, 0, nj + j)),
            pl.BlockSpec((None, tf, d), lambda i, j: (layer, j, 0)),
            _layer_rows(layer, d),
        ],
        out_specs=pl.BlockSpec((tm, d), lambda i, j: (i, 0)),
        out_shape=jax.ShapeDtypeStruct((t, d), _F32),
        scratch_shapes=[pltpu.VMEM((CONV_HALO + tm, d), _BF16)],
        compiler_params=_params(("parallel", "arbitrary")),
        name="conv_ffn",
    )(h2d, h2d, gpre, w_up, w_up, conv_w, conv_w, conv_b, conv_b, w_down, gpost)


def _ple_kernel(h_ref, p_ref, wg_ref, wi_ref, o_ref):
    h = h_ref[...]
    gate = jax.nn.sigmoid(jnp.dot(h.astype(_BF16), wg_ref[...], preferred_element_type=_F32))
    e = jnp.dot(p_ref[...].astype(_BF16), wi_ref[...], preferred_element_type=_F32)
    o_ref[...] = h + gate * e


def _ple(h2d, p3d, w_gate, w_in, layer, *, tm):
    t, d = h2d.shape
    pd = p3d.shape[2]
    assert t % tm == 0
    resident = lambda w: pl.BlockSpec((None,) + w.shape[1:], lambda i: (layer, 0, 0), pipeline_mode=pl.Buffered(1))
    return pl.pallas_call(
        _ple_kernel,
        grid=(t // tm,),
        in_specs=[
            pl.BlockSpec((tm, d), lambda i: (i, 0)),
            pl.BlockSpec((None, tm, pd), lambda i: (layer, i, 0)),
            resident(w_gate),
            resident(w_in),
        ],
        out_specs=pl.BlockSpec((tm, d), lambda i: (i, 0)),
        out_shape=jax.ShapeDtypeStruct((t, d), _F32),
        compiler_params=_params(("parallel",)),
        name="layer_embedding",
    )(h2d, p3d, w_gate, w_in)


def _tiles(t, seq, d, d_in, d_ff):
    def largest(n, cands):
        return next(c for c in cands if n % c == 0)
    return dict(
        proj_tm=largest(t, (1024, 512, 256, 128)),
        proj_tn=largest(math.gcd(d_in, QKV_WIDTH), (2048, 1024, 512, 256, 128)),
        stick_tq=largest(seq, (256, 128)),
        merge_tm=largest(t, (512, 256, 128)),
        ffn_tm=largest(seq, (1024, 512, 256, 128)),
        ffn_tf=largest(d_ff, (512, 256, 128)),
        ple_tm=largest(t, (1024, 512, 256, 128)),
    )


def kernel(x, p, g_mix_pre, w_in, w_branch_a, w_branch_b, w_out, g_mix_post, g_ffn_pre, w_up, conv_w,
           conv_b, w_down, g_ffn_post, w_ple_in, w_ple_gate):
    b, s, d = x.shape
    depth = w_in.shape[0]
    d_in = w_in.shape[2]
    d_ff = w_down.shape[1]
    t = b * s
    assert d_in == QKV_WIDTH + 2 * d
    tiles = _tiles(t, s, d, d_in, d_ff)

    q_scale = 1.0 / math.sqrt(HEAD_DIM)
    col = jnp.arange(d_in)
    is_q = (col < WIDTH_A) | ((col >= 3 * WIDTH_A) & (col < 3 * WIDTH_A + WIDTH_B))
    colscale = jnp.where(is_q, q_scale, 1.0).astype(_F32)[None, :]

    bf16 = lambda w: w.astype(_BF16)
    rows = lambda v: v[:, None, :]
    w_in, w_branch_a, w_branch_b, w_out = bf16(w_in), bf16(w_branch_a), bf16(w_branch_b), bf16(w_out)
    w_up, w_down, w_ple_in, w_ple_gate = bf16(w_up), bf16(w_down), bf16(w_ple_in), bf16(w_ple_gate)
    p3d = p.reshape(depth, t, p.shape[-1])

    h = x.reshape(t, d)
    for i in range(depth):
        proj = _in_proj(h, rows(g_mix_pre), colscale, w_in, i, tm=tiles["proj_tm"], tn=tiles["proj_tn"])
        proj3 = proj.reshape(b, s, d_in)
        ya = _dilated_attention(proj3).reshape(t, A_OUT)
        yb = _stick_attention(proj3, tq=tiles["stick_tq"], heads=N_HEADS_B).reshape(t, WIDTH_B)
        h = _merge(ya, yb, proj, h, w_branch_a, w_branch_b, w_out, rows(g_mix_post), i, tm=tiles["merge_tm"])
        h = _ffn(h, rows(g_ffn_pre), w_up, conv_w, rows(conv_b), w_down, rows(g_ffn_post), i,
                 tm=tiles["ffn_tm"], tf=tiles["ffn_tf"], seq=s)
        h = _ple(h, p3d, w_ple_gate, w_ple_in, i, tm=tiles["ple_tm"])
    return h.reshape(b, s, d)
```

```python
import functools
import math

import jax
import jax.numpy as jnp
from jax import lax
from jax.experimental import pallas as pl
from jax.experimental.pallas import tpu as pltpu

HEAD_DIM = 128
DIL_GROUPS = ((128, 1), (512, 4), (2048, 16))
NEAR_GROUPS = DIL_GROUPS[:-1]
HEADS_PER_GROUP = 4
N_HEADS_A = HEADS_PER_GROUP * len(DIL_GROUPS)
N_HEADS_B = 4
WIDTH_A = N_HEADS_A * HEAD_DIM
WIDTH_B = N_HEADS_B * HEAD_DIM
A_OUT = HEADS_PER_GROUP * HEAD_DIM
QKV_WIDTH = 3 * WIDTH_A + 3 * WIDTH_B
CONV_WIDTH = 3
RMS_EPS = 1e-6
ALIBI_MAX = 8.0
MASKED = -1e30
EXP_UNDERFLOW = -110.0
QBLK = 128
SPAN = 2048
CONV_HALO = 16
VMEM_LIMIT = 56 * 1024 * 1024

_F32 = jnp.float32
_BF16 = jnp.bfloat16


def _rmsnorm_rows(x, g):
    ms = jnp.mean(x * x, axis=-1, keepdims=True)
    return x * lax.rsqrt(ms + RMS_EPS) * g


def _params(sem):
    return pltpu.CompilerParams(dimension_semantics=sem, vmem_limit_bytes=VMEM_LIMIT)


def _layer_rows(layer, width):
    return pl.BlockSpec((None, 1, width), lambda *_: (layer, 0, 0))


def _in_proj_kernel(x_ref, g_ref, cs_ref, w_ref, o_ref, u_ref, *, gate_tile0, chunk):
    j = pl.program_id(1)
    tm = x_ref.shape[0]

    @pl.when(j == 0)
    def _():
        def body(c, carry):
            r0 = pl.multiple_of(c * chunk, chunk)
            x = x_ref[pl.ds(r0, chunk), :]
            u_ref[pl.ds(r0, chunk), :] = _rmsnorm_rows(x, g_ref[...]).astype(u_ref.dtype)
            return carry
        lax.fori_loop(0, tm // chunk, body, 0)

    acc = jnp.dot(u_ref[...], w_ref[...], preferred_element_type=_F32)

    o_ref[...] = jnp.where(j >= gate_tile0, jax.nn.sigmoid(acc), acc * cs_ref[...]).astype(o_ref.dtype)


def _in_proj(h2d, g, colscale, w, layer, *, tm, tn):
    t, d = h2d.shape
    d_in = w.shape[2]
    assert t % tm == 0 and d_in % tn == 0 and QKV_WIDTH % tn == 0
    return pl.pallas_call(
        functools.partial(_in_proj_kernel, gate_tile0=QKV_WIDTH // tn, chunk=min(tm, 128)),
        grid=(t // tm, d_in // tn),
        in_specs=[
            pl.BlockSpec((tm, d), lambda i, j: (i, 0)),
            _layer_rows(layer, d),
            pl.BlockSpec((1, tn), lambda i, j: (0, j)),
            pl.BlockSpec((None, d, tn), lambda i, j: (layer, 0, j)),
        ],
        out_specs=pl.BlockSpec((tm, tn), lambda i, j: (i, j)),
        out_shape=jax.ShapeDtypeStruct((t, d_in), _BF16),
        scratch_shapes=[pltpu.VMEM((tm, d), _BF16)],
        compiler_params=_params(("parallel", "arbitrary")),
        name="in_proj",
    )(h2d, g, colscale, w)


def _dilated_kernel(q0, q1, kp0, kp1, kc0, kc1, vp0, vp1, vc0, vc1, b0, b1,
                    fq, fkp, fkc, fvp, fvc, fbias, o_ref):
    i = pl.program_id(2)
    q_refs, bias_refs = (q0, q1), (b0, b1)
    k_refs, v_refs = ((kp0, kc0), (kp1, kc1)), ((vp0, vc0), (vp1, vc1))

    def far_group():
        residue_major = lambda ref: pltpu.einshape("mrd->rmd", ref[0])
        token_major = lambda rows: pltpu.einshape("rmd->mrd", jnp.stack(rows)).reshape(SPAN, HEAD_DIM)
        qf = residue_major(fq)
        kf = jnp.concatenate([residue_major(fkp), residue_major(fkc)], axis=1)
        vf = jnp.concatenate([residue_major(fvp), residue_major(fvc)], axis=1)
        kj = lax.broadcasted_iota(jnp.int32, (QBLK, 2 * QBLK), 1)
        in_sequence = kj >= jnp.where(i == 0, QBLK, 0)
        residues = range(qf.shape[0])
        scores = [jnp.where(in_sequence,
                            lax.dot_general(qf[r], kf[r], (((1,), (1,)), ((), ())), preferred_element_type=_F32)
                            + fbias[0], MASKED) for r in residues]
        outs, lses = [], []
        for r in residues:
            m = jnp.max(scores[r], axis=-1, keepdims=True)
            p = jnp.exp(scores[r] - m)
            denom = jnp.sum(p, axis=-1, keepdims=True)
            outs.append(jnp.dot(p.astype(_BF16), vf[r], preferred_element_type=_F32) / denom)
            lses.append(jnp.broadcast_to(m + jnp.log(denom), (QBLK, HEAD_DIM)))
        return token_major(outs), token_major(lses)

    far_out, far_lse_lanes = far_group()

    def window_rows(prev_cur, o, window):
        prev_ref, cur_ref = prev_cur
        n_prev = max(window - o, 0)
        cur = cur_ref[0, max(o - window, 0):o + QBLK, :]
        if n_prev == 0:
            return cur, 0
        return jnp.concatenate([prev_ref[0, SPAN - n_prev:SPAN, :], cur], axis=0), n_prev

    def masked_scores(o):
        scores = []
        for g, (window, _) in enumerate(NEAR_GROUPS):
            kk, n_prev = window_rows(k_refs[g], o, window)
            sc = lax.dot_general(q_refs[g][0, o:o + QBLK, :], kk, (((1,), (1,)), ((), ())),
                                 preferred_element_type=_F32)
            sc = sc + bias_refs[g][0]
            if n_prev:
                kj = lax.broadcasted_iota(jnp.int32, sc.shape, 1)
                sc = jnp.where(jnp.logical_or(kj >= n_prev, i > 0), sc, MASKED)
            scores.append(sc)
        return scores

    def joint_softmax_weights(scores, o):
        far_lse = far_lse_lanes[o:o + QBLK, 0:1]
        m = functools.reduce(jnp.maximum, [jnp.max(sc, axis=-1, keepdims=True) for sc in scores] + [far_lse])
        ps = [jnp.exp(sc - m) for sc in scores]
        far_weight = jnp.exp(far_lse - m)
        denom = functools.reduce(jnp.add, [jnp.sum(p, axis=-1, keepdims=True) for p in ps] + [far_weight])
        return [p.astype(_BF16) for p in ps], far_weight, denom

    def weighted_values(ps, far_weight, o):
        acc = far_weight * far_out[o:o + QBLK, :]
        for g, (window, _) in enumerate(NEAR_GROUPS):
            acc = acc + jnp.dot(ps[g], window_rows(v_refs[g], o, window)[0], preferred_element_type=_F32)
        return acc

    offs = range(0, SPAN, QBLK)
    staged = [masked_scores(o) for o in offs]
    weights = [joint_softmax_weights(scores, o) for o, scores in zip(offs, staged)]
    for o, (ps, far_weight, denom) in zip(offs, weights):
        o_ref[0, o:o + QBLK, :] = (weighted_values(ps, far_weight, o) / denom).astype(o_ref.dtype)


def _alibi_slopes(group):
    slopes = jnp.exp2(-ALIBI_MAX * jnp.arange(1, N_HEADS_A + 1, dtype=_F32) / N_HEADS_A)
    return slopes[group * HEADS_PER_GROUP:(group + 1) * HEADS_PER_GROUP]


def _dilated_bias_tables():
    tables = []
    for g, (window, dilation) in enumerate(NEAR_GROUPS):
        qi = jnp.arange(QBLK)[:, None]
        kj = jnp.arange(window + QBLK)[None, :]
        dist = qi + window - kj
        valid = (dist >= 0) & (dist <= window) & (dist % dilation == 0)
        bias = -_alibi_slopes(g)[:, None, None] * dist.astype(_F32)[None]
        tables.append(jnp.where(valid[None], bias, MASKED))
    return tables


def _far_bias_table():
    window, dilation = DIL_GROUPS[-1]
    qi = jnp.arange(QBLK)[:, None]
    kj = jnp.arange(2 * QBLK)[None, :]
    steps = QBLK + qi - kj
    valid = (steps >= 0) & (steps <= window // dilation)
    bias = -_alibi_slopes(len(DIL_GROUPS) - 1)[:, None, None] * (steps * dilation).astype(_F32)[None]
    return jnp.where(valid[None], bias, MASKED)


def _dilated_attention(proj3):
    b, s, d_in = proj3.shape
    far_window, far_dilation = DIL_GROUPS[-1]
    assert s % SPAN == 0 and SPAN == QBLK * far_dilation == far_window
    n_heads = N_HEADS_A
    far_group = len(DIL_GROUPS) - 1

    def col_spec(base, lag):
        return [pl.BlockSpec((1, SPAN, HEAD_DIM), (lambda bi, hh, i, c=base + g * HEADS_PER_GROUP:
                                                   (bi, jnp.maximum(i - lag, 0), c + hh)))
                for g in range(len(NEAR_GROUPS))]

    def far_spec(base, lag):
        return pl.BlockSpec((1, QBLK, far_dilation, HEAD_DIM),
                            lambda bi, hh, i, c=base + far_group * HEADS_PER_GROUP:
                            (bi, jnp.maximum(i - lag, 0), 0, c + hh))

    head_table = lambda table: pl.BlockSpec((1,) + table.shape[1:], lambda bi, hh, i: (hh, 0, 0))
    tables = _dilated_bias_tables() + [_far_bias_table()]
    near_specs = (col_spec(0, 0) + col_spec(n_heads, 1) + col_spec(n_heads, 0)
                  + col_spec(2 * n_heads, 1) + col_spec(2 * n_heads, 0))
    far_specs = [far_spec(0, 0), far_spec(n_heads, 1), far_spec(n_heads, 0),
                 far_spec(2 * n_heads, 1), far_spec(2 * n_heads, 0)]
    far_view = proj3.reshape(b, s // far_dilation, far_dilation, d_in)
    n_near = len(NEAR_GROUPS)
    return pl.pallas_call(
        _dilated_kernel,
        grid=(b, HEADS_PER_GROUP, s // SPAN),
        in_specs=(near_specs + [head_table(tb) for tb in tables[:n_near]] + far_specs + [head_table(tables[-1])]),
        out_specs=pl.BlockSpec((1, SPAN, HEAD_DIM), lambda bi, hh, i: (bi, i, hh)),
        out_shape=jax.ShapeDtypeStruct((b, s, A_OUT), _BF16),
        compiler_params=_params(("parallel", "parallel", "parallel")),
        name="dilated_attention",
    )(*([proj3] * (5 * n_near)), *tables[:n_near], *([far_view] * 5), tables[-1])


def _stick_kernel(q_ref, k_ref, v_ref, tri_ref, o_ref, *, tq, heads):
    i = pl.program_id(2)
    row = lax.broadcasted_iota(jnp.int32, (tq, tq), 0)
    col = lax.broadcasted_iota(jnp.int32, (tq, tq), 1)
    before = col < row

    def chunk(c, afters, accs):
        k0 = pl.multiple_of(c * tq, tq)
        tri = tri_ref[...]
        lanes = [slice(h * HEAD_DIM, (h + 1) * HEAD_DIM) for h in range(heads)]
        zs = [lax.dot_general(q_ref[0, :, lanes[h]], k_ref[0, pl.ds(k0, tq), lanes[h]],
                              (((1,), (1,)), ((), ())), preferred_element_type=_F32) for h in range(heads)]
        csums = []
        for h in range(heads):
            z = zs[h]
            ln = -(jnp.maximum(z, 0.0) + jnp.log(1.0 + jnp.exp(-jnp.abs(z))))
            hi = ln.astype(_BF16)
            lo = (ln - hi.astype(_F32)).astype(_BF16)
            csums.append(jnp.dot(hi, tri, preferred_element_type=_F32)
                         + jnp.dot(lo, tri, preferred_element_type=_F32))
        new_afters, new_accs = [], []
        for h in range(heads):
            w = jnp.exp(zs[h] + csums[h] + afters[h])
            new_accs.append(accs[h] + jnp.dot(w.astype(_BF16), v_ref[0, pl.ds(k0, tq), lanes[h]],
                                              preferred_element_type=_F32))
            new_afters.append(afters[h] + csums[h][:, 0:1])
        return tuple(new_afters), tuple(new_accs)

    def first_two_chunks():
        has_prev = i > 0
        tri = tri_ref[...]
        lanes = [slice(h * HEAD_DIM, (h + 1) * HEAD_DIM) for h in range(heads)]
        starts = (pl.multiple_of(i * tq, tq), pl.multiple_of(jnp.maximum(i - 1, 0) * tq, tq))
        keep = (before, has_prev)
        zs = [[lax.dot_general(q_ref[0, :, lanes[h]], k_ref[0, pl.ds(k0, tq), lanes[h]],
                               (((1,), (1,)), ((), ())), preferred_element_type=_F32) for h in range(heads)]
              for k0 in starts]
        csums = []
        for n in range(2):
            csums.append([])
            for h in range(heads):
                z = zs[n][h]
                ln = jnp.where(keep[n], -(jnp.maximum(z, 0.0) + jnp.log(1.0 + jnp.exp(-jnp.abs(z)))), 0.0)
                hi = ln.astype(_BF16)
                lo = (ln - hi.astype(_F32)).astype(_BF16)
                csums[n].append(jnp.dot(hi, tri, preferred_element_type=_F32)
                                + jnp.dot(lo, tri, preferred_element_type=_F32))
        afters, accs = [], []
        for h in range(heads):
            after = csums[0][h][:, 0:1]
            w_diag = jnp.where(before, jnp.exp(zs[0][h] + csums[0][h]), 0.0)
            w_prev = jnp.where(has_prev, jnp.exp(zs[1][h] + csums[1][h] + after), 0.0)
            accs.append(jnp.dot(w_diag.astype(_BF16), v_ref[0, pl.ds(starts[0], tq), lanes[h]],
                                preferred_element_type=_F32)
                        + jnp.dot(w_prev.astype(_BF16), v_ref[0, pl.ds(starts[1], tq), lanes[h]],
                                  preferred_element_type=_F32))
            afters.append(after + csums[1][h][:, 0:1])
        return tuple(afters), tuple(accs)

    afters, accs = first_two_chunks()

    def keep_going(carry):
        c, afters, _ = carry
        return jnp.logical_and(c >= 0, jnp.max(functools.reduce(jnp.maximum, afters)) > EXP_UNDERFLOW)

    def body(carry):
        c, afters, accs = carry
        afters, accs = chunk(c, afters, accs)
        return c - 1, afters, accs

    _, _, accs = lax.while_loop(keep_going, body, (i - 2, afters, accs))
    for h in range(heads):
        o_ref[0, :, h * HEAD_DIM:(h + 1) * HEAD_DIM] = accs[h].astype(o_ref.dtype)


def _stick_attention(proj3, *, tq, heads):
    b, s, _ = proj3.shape
    width = heads * HEAD_DIM
    assert s % tq == 0 and N_HEADS_B % heads == 0 and (3 * WIDTH_A) % width == 0
    groups = N_HEADS_B // heads
    base = 3 * WIDTH_A // width
    idx = jnp.arange(tq)
    tri = (idx[:, None] >= idx[None, :]).astype(_BF16)
    whole_seq = lambda part: pl.BlockSpec((1, s, width), lambda bi, hg, i: (bi, 0, base + part * groups + hg),
                                          pipeline_mode=pl.Buffered(1))
    return pl.pallas_call(
        functools.partial(_stick_kernel, tq=tq, heads=heads),
        grid=(b, groups, s // tq),
        in_specs=[
            pl.BlockSpec((1, tq, width), lambda bi, hg, i: (bi, i, base + hg)),
            whole_seq(1),
            whole_seq(2),
            pl.BlockSpec((tq, tq), lambda bi, hg, i: (0, 0)),
        ],
        out_specs=pl.BlockSpec((1, tq, width), lambda bi, hg, i: (bi, i, hg)),
        out_shape=jax.ShapeDtypeStruct((b, s, WIDTH_B), _BF16),
        compiler_params=_params(("parallel", "parallel", "arbitrary")),
        name="stick_breaking_attention",
    )(proj3, proj3, proj3, tri)


def _merge_kernel(ya_ref, yb_ref, ga_ref, gb_ref, h_ref, wa_ref, wb_ref, wo_ref, g_ref, o_ref):
    a = jnp.dot(ya_ref[...], wa_ref[...], preferred_element_type=_F32)
    bb = jnp.dot(yb_ref[...], wb_ref[...], preferred_element_type=_F32)
    m = (ga_ref[...].astype(_F32) * a + gb_ref[...].astype(_F32) * bb).astype(_BF16)
    mix = jnp.dot(m, wo_ref[...], preferred_element_type=_F32)
    o_ref[...] = h_ref[...] + _rmsnorm_rows(mix, g_ref[...])


def _merge(ya, yb, proj, h2d, wa, wb, wo, g, layer, *, tm):
    t, d = h2d.shape
    assert t % tm == 0 and QKV_WIDTH % d == 0
    gate_blk = QKV_WIDTH // d
    resident = lambda w: pl.BlockSpec((None,) + w.shape[1:], lambda i: (layer, 0, 0), pipeline_mode=pl.Buffered(1))
    return pl.pallas_call(
        _merge_kernel,
        grid=(t // tm,),
        in_specs=[
            pl.BlockSpec((tm, A_OUT), lambda i: (i, 0)),
            pl.BlockSpec((tm, WIDTH_B), lambda i: (i, 0)),
            pl.BlockSpec((tm, d), lambda i: (i, gate_blk)),
            pl.BlockSpec((tm, d), lambda i: (i, gate_blk + 1)),
            pl.BlockSpec((tm, d), lambda i: (i, 0)),
            resident(wa), resident(wb), resident(wo), _layer_rows(layer, d),
        ],
        out_specs=pl.BlockSpec((tm, d), lambda i: (i, 0)),
        out_shape=jax.ShapeDtypeStruct((t, d), _F32),
        compiler_params=_params(("parallel",)),
        name="gated_merge",
    )(ya, yb, proj, proj, h2d, wa, wb, wo, g)


def _ffn_kernel(h_ref, hp_ref, gpre_ref, wg_ref, wv_ref, cwg_ref, cwv_ref, cbg_ref, cbv_ref, wd_ref,
                gpost_ref, o_ref, u_ref, *, tiles_per_seq, chunk):
    i = pl.program_id(0)
    j = pl.program_id(1)
    tm = h_ref.shape[0]
    n_chunks = tm // chunk

    @pl.when(j == 0)
    def _():
        halo = _rmsnorm_rows(hp_ref[...], gpre_ref[...])
        at_sequence_start = (i % tiles_per_seq) == 0
        u_ref[0:CONV_HALO, :] = jnp.where(at_sequence_start, 0.0, halo).astype(u_ref.dtype)
        o_ref[...] = jnp.zeros(o_ref.shape, o_ref.dtype)

        def body(c, carry):
            r0 = pl.multiple_of(c * chunk, chunk)
            x = h_ref[pl.ds(r0, chunk), :]
            u_ref[pl.ds(CONV_HALO + r0, chunk), :] = _rmsnorm_rows(x, gpre_ref[...]).astype(u_ref.dtype)
            return carry
        lax.fori_loop(0, n_chunks, body, 0)

    u = u_ref[...]

    def conv_branch(w_ref, cw_ref, cb_ref):
        x = jnp.dot(u, w_ref[...], preferred_element_type=_F32)
        cw = cw_ref[...]
        y = cb_ref[...] + cw[0:1, :] * pltpu.roll(x, 2, 0)
        y = y + cw[1:2, :] * pltpu.roll(x, 1, 0)
        y = y + cw[2:3, :] * x
        return y[CONV_HALO:, :]

    gate = conv_branch(wg_ref, cwg_ref, cbg_ref)
    value = conv_branch(wv_ref, cwv_ref, cbv_ref)
    act = (jax.nn.gelu(gate, approximate=True) * value).astype(_BF16)
    o_ref[...] += jnp.dot(act, wd_ref[...], preferred_element_type=_F32)

    @pl.when(j == pl.num_programs(1) - 1)
    def _():
        def body(c, carry):
            r0 = pl.multiple_of(c * chunk, chunk)
            f = o_ref[pl.ds(r0, chunk), :]
            o_ref[pl.ds(r0, chunk), :] = h_ref[pl.ds(r0, chunk), :] + _rmsnorm_rows(f, gpost_ref[...])
            return carry
        lax.fori_loop(0, n_chunks, body, 0)


def _ffn(h2d, gpre, w_up, conv_w, conv_b, w_down, gpost, layer, *, tm, tf, seq):
    t, d = h2d.shape
    d_ff = w_down.shape[1]
    assert t % tm == 0 and seq % tm == 0 and d_ff % tf == 0 and tm % CONV_HALO == 0
    nj = d_ff // tf
    halo_blocks = tm // CONV_HALO
    return pl.pallas_call(
        functools.partial(_ffn_kernel, tiles_per_seq=seq // tm, chunk=min(tm, 128)),
        grid=(t // tm, nj),
        in_specs=[
            pl.BlockSpec((tm, d), lambda i, j: (i, 0)),
            pl.BlockSpec((CONV_HALO, d), lambda i, j: (jnp.maximum(i * halo_blocks - 1, 0), 0)),
            _layer_rows(layer, d),
            pl.BlockSpec((None, d, tf), lambda i, j: (layer, 0, j)),
            pl.BlockSpec((None, d, tf), lambda i, j: (layer, 0, nj + j)),
            pl.BlockSpec((None, CONV_WIDTH, tf), lambda i, j: (layer, 0, j)),
            pl.BlockSpec((None, CONV_WIDTH, tf), lambda i, j: (layer, 0, nj + j)),
            pl.BlockSpec((None, 1, tf), lambda i, j: (layer, 0, j)),
            pl.BlockSpec((None, 1, tf), lambda i, j: (layer, 0, nj + j)),
            pl.BlockSpec((None, tf, d), lambda i, j: (layer, j, 0)),
            _layer_rows(layer, d),
        ],
        out_specs=pl.BlockSpec((tm, d), lambda i, j: (i, 0)),
        out_shape=jax.ShapeDtypeStruct((t, d), _F32),
        scratch_shapes=[pltpu.VMEM((CONV_HALO + tm, d), _BF16)],
        compiler_params=_params(("parallel", "arbitrary")),
        name="conv_ffn",
    )(h2d, h2d, gpre, w_up, w_up, conv_w, conv_w, conv_b, conv_b, w_down, gpost)


def _ple_kernel(h_ref, p_ref, wg_ref, wi_ref, o_ref):
    h = h_ref[...]
    gate = jax.nn.sigmoid(jnp.dot(h.astype(_BF16), wg_ref[...], preferred_element_type=_F32))
    e = jnp.dot(p_ref[...].astype(_BF16), wi_ref[...], preferred_element_type=_F32)
    o_ref[...] = h + gate * e


def _ple(h2d, p3d, w_gate, w_in, layer, *, tm):
    t, d = h2d.shape
    pd = p3d.shape[2]
    assert t % tm == 0
    resident = lambda w: pl.BlockSpec((None,) + w.shape[1:], lambda i: (layer, 0, 0), pipeline_mode=pl.Buffered(1))
    return pl.pallas_call(
        _ple_kernel,
        grid=(t // tm,),
        in_specs=[
            pl.BlockSpec((tm, d), lambda i: (i, 0)),
            pl.BlockSpec((None, tm, pd), lambda i: (layer, i, 0)),
            resident(w_gate),
            resident(w_in),
        ],
        out_specs=pl.BlockSpec((tm, d), lambda i: (i, 0)),
        out_shape=jax.ShapeDtypeStruct((t, d), _F32),
        compiler_params=_params(("parallel",)),
        name="layer_embedding",
    )(h2d, p3d, w_gate, w_in)


def _tiles(t, seq, d, d_in, d_ff):
    def largest(n, cands):
        return next(c for c in cands if n % c == 0)
    return dict(
        proj_tm=largest(t, (1024, 512, 256, 128)),
        proj_tn=largest(math.gcd(d_in, QKV_WIDTH), (2048, 1024, 512, 256, 128)),
        stick_tq=largest(seq, (256, 128)),
        merge_tm=largest(t, (512, 256, 128)),
        ffn_tm=largest(seq, (1024, 512, 256, 128)),
        ffn_tf=largest(d_ff, (512, 256, 128)),
        ple_tm=largest(t, (1024, 512, 256, 128)),
    )


def kernel(x, p, g_mix_pre, w_in, w_branch_a, w_branch_b, w_out, g_mix_post, g_ffn_pre, w_up, conv_w,
           conv_b, w_down, g_ffn_post, w_ple_in, w_ple_gate):
    b, s, d = x.shape
    depth = w_in.shape[0]
    d_in = w_in.shape[2]
    d_ff = w_down.shape[1]
    t = b * s
    assert d_in == QKV_WIDTH + 2 * d
    tiles = _tiles(t, s, d, d_in, d_ff)

    q_scale = 1.0 / math.sqrt(HEAD_DIM)
    col = jnp.arange(d_in)
    is_q = (col < WIDTH_A) | ((col >= 3 * WIDTH_A) & (col < 3 * WIDTH_A + WIDTH_B))
    colscale = jnp.where(is_q, q_scale, 1.0).astype(_F32)[None, :]

    bf16 = lambda w: w.astype(_BF16)
    rows = lambda v: v[:, None, :]
    w_in, w_branch_a, w_branch_b, w_out = bf16(w_in), bf16(w_branch_a), bf16(w_branch_b), bf16(w_out)
    w_up, w_down, w_ple_in, w_ple_gate = bf16(w_up), bf16(w_down), bf16(w_ple_in), bf16(w_ple_gate)
    p3d = p.reshape(depth, t, p.shape[-1])

    h = x.reshape(t, d)
    for i in range(depth):
        proj = _in_proj(h, rows(g_mix_pre), colscale, w_in, i, tm=tiles["proj_tm"], tn=tiles["proj_tn"])
        proj3 = proj.reshape(b, s, d_in)
        ya = _dilated_attention(proj3).reshape(t, A_OUT)
        yb = _stick_attention(proj3, tq=tiles["stick_tq"], heads=N_HEADS_B).reshape(t, WIDTH_B)
        h = _merge(ya, yb, proj, h, w_branch_a, w_branch_b, w_out, rows(g_mix_post), i, tm=tiles["merge_tm"])
        h = _ffn(h, rows(g_ffn_pre), w_up, conv_w, rows(conv_b), w_down, rows(g_ffn_post), i,
                 tm=tiles["ffn_tm"], tf=tiles["ffn_tf"], seq=s)
        h = _ple(h, p3d, w_ple_gate, w_ple_in, i, tm=tiles["ple_tm"])
    return h.reshape(b, s, d)
```

```python
import functools
import math

import jax
import jax.numpy as jnp
from jax import lax
from jax.experimental import pallas as pl
from jax.experimental.pallas import tpu as pltpu

HEAD_DIM = 128
DIL_GROUPS = ((128, 1), (512, 4), (2048, 16))
NEAR_GROUPS = DIL_GROUPS[:-1]
HEADS_PER_GROUP = 4
N_HEADS_A = HEADS_PER_GROUP * len(DIL_GROUPS)
N_HEADS_B = 4
WIDTH_A = N_HEADS_A * HEAD_DIM
WIDTH_B = N_HEADS_B * HEAD_DIM
A_OUT = HEADS_PER_GROUP * HEAD_DIM
QKV_WIDTH = 3 * WIDTH_A + 3 * WIDTH_B
CONV_WIDTH = 3
RMS_EPS = 1e-6
ALIBI_MAX = 8.0
MASKED = -1e30
EXP_UNDERFLOW = -110.0
QBLK = 128
SPAN = 2048
CONV_HALO = 16
VMEM_LIMIT = 56 * 1024 * 1024

_F32 = jnp.float32
_BF16 = jnp.bfloat16


def _rmsnorm_rows(x, g):
    ms = jnp.mean(x * x, axis=-1, keepdims=True)
    return x * lax.rsqrt(ms + RMS_EPS) * g


def _params(sem):
    return pltpu.CompilerParams(dimension_semantics=sem, vmem_limit_bytes=VMEM_LIMIT)


def _layer_rows(layer, width):
    return pl.BlockSpec((None, 1, width), lambda *_: (layer, 0, 0))


def _in_proj_kernel(x_ref, g_ref, cs_ref, w_ref, o_ref, u_ref, *, gate_tile0, chunk):
    j = pl.program_id(1)
    tm = x_ref.shape[0]

    @pl.when(j == 0)
    def _():
        def body(c, carry):
            r0 = pl.multiple_of(c * chunk, chunk)
            x = x_ref[pl.ds(r0, chunk), :]
            u_ref[pl.ds(r0, chunk), :] = _rmsnorm_rows(x, g_ref[...]).astype(u_ref.dtype)
            return carry
        lax.fori_loop(0, tm // chunk, body, 0)

    acc = jnp.dot(u_ref[...], w_ref[...], preferred_element_type=_F32)

    o_ref[...] = jnp.where(j >= gate_tile0, jax.nn.sigmoid(acc), acc * cs_ref[...]).astype(o_ref.dtype)


def _in_proj(h2d, g, colscale, w, layer, *, tm, tn):
    t, d = h2d.shape
    d_in = w.shape[2]
    assert t % tm == 0 and d_in % tn == 0 and QKV_WIDTH % tn == 0
    return pl.pallas_call(
        functools.partial(_in_proj_kernel, gate_tile0=QKV_WIDTH // tn, chunk=min(tm, 128)),
        grid=(t // tm, d_in // tn),
        in_specs=[
            pl.BlockSpec((tm, d), lambda i, j: (i, 0)),
            _layer_rows(layer, d),
            pl.BlockSpec((1, tn), lambda i, j: (0, j)),
            pl.BlockSpec((None, d, tn), lambda i, j: (layer, 0, j)),
        ],
        out_specs=pl.BlockSpec((tm, tn), lambda i, j: (i, j)),
        out_shape=jax.ShapeDtypeStruct((t, d_in), _BF16),
        scratch_shapes=[pltpu.VMEM((tm, d), _BF16)],
        compiler_params=_params(("parallel", "arbitrary")),
        name="in_proj",
    )(h2d, g, colscale, w)


def _dilated_kernel(q0, q1, kp0, kp1, kc0, kc1, vp0, vp1, vc0, vc1, b0, b1,
                    fq, fkp, fkc, fvp, fvc, fbias, o_ref):
    i = pl.program_id(2)
    q_refs, bias_refs = (q0, q1), (b0, b1)
    k_refs, v_refs = ((kp0, kc0), (kp1, kc1)), ((vp0, vc0), (vp1, vc1))

    def far_group():
        residue_major = lambda ref: pltpu.einshape("mrd->rmd", ref[0])
        token_major = lambda rows: pltpu.einshape("rmd->mrd", jnp.stack(rows)).reshape(SPAN, HEAD_DIM)
        qf = residue_major(fq)
        kf = jnp.concatenate([residue_major(fkp), residue_major(fkc)], axis=1)
        vf = jnp.concatenate([residue_major(fvp), residue_major(fvc)], axis=1)
        kj = lax.broadcasted_iota(jnp.int32, (QBLK, 2 * QBLK), 1)
        in_sequence = kj >= jnp.where(i == 0, QBLK, 0)
        residues = range(qf.shape[0])
        scores = [jnp.where(in_sequence,
                            lax.dot_general(qf[r], kf[r], (((1,), (1,)), ((), ())), preferred_element_type=_F32)
                            + fbias[0], MASKED) for r in residues]
        outs, lses = [], []
        for r in residues:
            m = jnp.max(scores[r], axis=-1, keepdims=True)
            p = jnp.exp(scores[r] - m)
            denom = jnp.sum(p, axis=-1, keepdims=True)
            outs.append(jnp.dot(p.astype(_BF16), vf[r], preferred_element_type=_F32) / denom)
            lses.append(jnp.broadcast_to(m + jnp.log(denom), (QBLK, HEAD_DIM)))
        return token_major(outs), token_major(lses)

    far_out, far_lse_lanes = far_group()

    def window_rows(prev_cur, o, window):
        prev_ref, cur_ref = prev_cur
        n_prev = max(window - o, 0)
        cur = cur_ref[0, max(o - window, 0):o + QBLK, :]
        if n_prev == 0:
            return cur, 0
        return jnp.concatenate([prev_ref[0, SPAN - n_prev:SPAN, :], cur], axis=0), n_prev

    def masked_scores(o):
        scores = []
        for g, (window, _) in enumerate(NEAR_GROUPS):
            kk, n_prev = window_rows(k_refs[g], o, window)
            sc = lax.dot_general(q_refs[g][0, o:o + QBLK, :], kk, (((1,), (1,)), ((), ())),
                                 preferred_element_type=_F32)
            sc = sc + bias_refs[g][0]
            if n_prev:
                kj = lax.broadcasted_iota(jnp.int32, sc.shape, 1)
                sc = jnp.where(jnp.logical_or(kj >= n_prev, i > 0), sc, MASKED)
            scores.append(sc)
        return scores

    def joint_softmax_weights(scores, o):
        far_lse = far_lse_lanes[o:o + QBLK, 0:1]
        m = functools.reduce(jnp.maximum, [jnp.max(sc, axis=-1, keepdims=True) for sc in scores] + [far_lse])
        ps = [jnp.exp(sc - m) for sc in scores]
        far_weight = jnp.exp(far_lse - m)
        denom = functools.reduce(jnp.add, [jnp.sum(p, axis=-1, keepdims=True) for p in ps] + [far_weight])
        return [p.astype(_BF16) for p in ps], far_weight, denom

    def weighted_values(ps, far_weight, o):
        acc = far_weight * far_out[o:o + QBLK, :]
        for g, (window, _) in enumerate(NEAR_GROUPS):
            acc = acc + jnp.dot(ps[g], window_rows(v_refs[g], o, window)[0], preferred_element_type=_F32)
        return acc

    offs = range(0, SPAN, QBLK)
    staged = [masked_scores(o) for o in offs]
    weights = [joint_softmax_weights(scores, o) for o, scores in zip(offs, staged)]
    for o, (ps, far_weight, denom) in zip(offs, weights):
        o_ref[0, o:o + QBLK, :] = (weighted_values(ps, far_weight, o) / denom).astype(o_ref.dtype)


def _alibi_slopes(group):
    slopes = jnp.exp2(-ALIBI_MAX * jnp.arange(1, N_HEADS_A + 1, dtype=_F32) / N_HEADS_A)
    return slopes[group * HEADS_PER_GROUP:(group + 1) * HEADS_PER_GROUP]


def _dilated_bias_tables():
    tables = []
    for g, (window, dilation) in enumerate(NEAR_GROUPS):
        qi = jnp.arange(QBLK)[:, None]
        kj = jnp.arange(window + QBLK)[None, :]
        dist = qi + window - kj
        valid = (dist >= 0) & (dist <= window) & (dist % dilation == 0)
        bias = -_alibi_slopes(g)[:, None, None] * dist.astype(_F32)[None]
        tables.append(jnp.where(valid[None], bias, MASKED))
    return tables


def _far_bias_table():
    window, dilation = DIL_GROUPS[-1]
    qi = jnp.arange(QBLK)[:, None]
    kj = jnp.arange(2 * QBLK)[None, :]
    steps = QBLK + qi - kj
    valid = (steps >= 0) & (steps <= window // dilation)
    bias = -_alibi_slopes(len(DIL_GROUPS) - 1)[:, None, None] * (steps * dilation).astype(_F32)[None]
    return jnp.where(valid[None], bias, MASKED)


def _dilated_attention(proj3):
    b, s, d_in = proj3.shape
    far_window, far_dilation = DIL_GROUPS[-1]
    assert s % SPAN == 0 and SPAN == QBLK * far_dilation == far_window
    n_heads = N_HEADS_A
    far_group = len(DIL_GROUPS) - 1

    def col_spec(base, lag):
        return [pl.BlockSpec((1, SPAN, HEAD_DIM), (lambda bi, hh, i, c=base + g * HEADS_PER_GROUP:
                                                   (bi, jnp.maximum(i - lag, 0), c + hh)))
                for g in range(len(NEAR_GROUPS))]

    def far_spec(base, lag):
        return pl.BlockSpec((1, QBLK, far_dilation, HEAD_DIM),
                            lambda bi, hh, i, c=base + far_group * HEADS_PER_GROUP:
                            (bi, jnp.maximum(i - lag, 0), 0, c + hh))

    head_table = lambda table: pl.BlockSpec((1,) + table.shape[1:], lambda bi, hh, i: (hh, 0, 0))
    tables = _dilated_bias_tables() + [_far_bias_table()]
    near_specs = (col_spec(0, 0) + col_spec(n_heads, 1) + col_spec(n_heads, 0)
                  + col_spec(2 * n_heads, 1) + col_spec(2 * n_heads, 0))
    far_specs = [far_spec(0, 0), far_spec(n_heads, 1), far_spec(n_heads, 0),
                 far_spec(2 * n_heads, 1), far_spec(2 * n_heads, 0)]
    far_view = proj3.reshape(b, s // far_dilation, far_dilation, d_in)
    n_near = len(NEAR_GROUPS)
    return pl.pallas_call(
        _dilated_kernel,
        grid=(b, HEADS_PER_GROUP, s // SPAN),
        in_specs=(near_specs + [head_table(tb) for tb in tables[:n_near]] + far_specs + [head_table(tables[-1])]),
        out_specs=pl.BlockSpec((1, SPAN, HEAD_DIM), lambda bi, hh, i: (bi, i, hh)),
        out_shape=jax.ShapeDtypeStruct((b, s, A_OUT), _BF16),
        compiler_params=_params(("parallel", "parallel", "parallel")),
        name="dilated_attention",
    )(*([proj3] * (5 * n_near)), *tables[:n_near], *([far_view] * 5), tables[-1])


def _stick_kernel(q_ref, kd_ref, kp_ref, vd_ref, vp_ref, seq_hbm, tri_ref, o_ref, kbuf, vbuf,
                  *, tq, heads, k_col, v_col):
    bi = pl.program_id(0)
    i = pl.program_id(2)
    row = lax.broadcasted_iota(jnp.int32, (tq, tq), 0)
    col = lax.broadcasted_iota(jnp.int32, (tq, tq), 1)
    before = col < row
    width = heads * HEAD_DIM

    def chunk(c, afters, accs):
        rows = pl.ds(pl.multiple_of(c * tq, tq), tq)
        pltpu.sync_copy(seq_hbm.at[bi, rows, pl.ds(k_col, width)], kbuf)
        pltpu.sync_copy(seq_hbm.at[bi, rows, pl.ds(v_col, width)], vbuf)
        tri = tri_ref[...]
        lanes = [slice(h * HEAD_DIM, (h + 1) * HEAD_DIM) for h in range(heads)]
        zs = [lax.dot_general(q_ref[0, :, lanes[h]], kbuf[:, lanes[h]],
                              (((1,), (1,)), ((), ())), preferred_element_type=_F32) for h in range(heads)]
        csums = []
        for h in range(heads):
            z = zs[h]
            ln = -(jnp.maximum(z, 0.0) + jnp.log(1.0 + jnp.exp(-jnp.abs(z))))
            hi = ln.astype(_BF16)
            lo = (ln - hi.astype(_F32)).astype(_BF16)
            csums.append(jnp.dot(hi, tri, preferred_element_type=_F32)
                         + jnp.dot(lo, tri, preferred_element_type=_F32))
        new_afters, new_accs = [], []
        for h in range(heads):
            w = jnp.exp(zs[h] + csums[h] + afters[h])
            new_accs.append(accs[h] + jnp.dot(w.astype(_BF16), vbuf[:, lanes[h]], preferred_element_type=_F32))
            new_afters.append(afters[h] + csums[h][:, 0:1])
        return tuple(new_afters), tuple(new_accs)

    def first_two_chunks():
        has_prev = i > 0
        tri = tri_ref[...]
        lanes = [slice(h * HEAD_DIM, (h + 1) * HEAD_DIM) for h in range(heads)]
        keep = (before, has_prev)
        zs = [[lax.dot_general(q_ref[0, :, lanes[h]], k_ref[0, :, lanes[h]],
                               (((1,), (1,)), ((), ())), preferred_element_type=_F32) for h in range(heads)]
              for k_ref in (kd_ref, kp_ref)]
        csums = []
        for n in range(2):
            csums.append([])
            for h in range(heads):
                z = zs[n][h]
                ln = jnp.where(keep[n], -(jnp.maximum(z, 0.0) + jnp.log(1.0 + jnp.exp(-jnp.abs(z)))), 0.0)
                hi = ln.astype(_BF16)
                lo = (ln - hi.astype(_F32)).astype(_BF16)
                csums[n].append(jnp.dot(hi, tri, preferred_element_type=_F32)
                                + jnp.dot(lo, tri, preferred_element_type=_F32))
        afters, accs = [], []
        for h in range(heads):
            after = csums[0][h][:, 0:1]
            w_diag = jnp.where(before, jnp.exp(zs[0][h] + csums[0][h]), 0.0)
            w_prev = jnp.where(has_prev, jnp.exp(zs[1][h] + csums[1][h] + after), 0.0)
            accs.append(jnp.dot(w_diag.astype(_BF16), vd_ref[0, :, lanes[h]], preferred_element_type=_F32)
                        + jnp.dot(w_prev.astype(_BF16), vp_ref[0, :, lanes[h]], preferred_element_type=_F32))
            afters.append(after + csums[1][h][:, 0:1])
        return tuple(afters), tuple(accs)

    afters, accs = first_two_chunks()

    def keep_going(carry):
        c, afters, _ = carry
        return jnp.logical_and(c >= 0, jnp.max(functools.reduce(jnp.maximum, afters)) > EXP_UNDERFLOW)

    def body(carry):
        c, afters, accs = carry
        afters, accs = chunk(c, afters, accs)
        return c - 1, afters, accs

    _, _, accs = lax.while_loop(keep_going, body, (i - 2, afters, accs))
    for h in range(heads):
        o_ref[0, :, h * HEAD_DIM:(h + 1) * HEAD_DIM] = accs[h].astype(o_ref.dtype)


def _stick_attention(proj3, *, tq, heads):
    b, s, _ = proj3.shape
    width = heads * HEAD_DIM
    assert s % tq == 0 and N_HEADS_B % heads == 0 and (3 * WIDTH_A) % width == 0
    groups = N_HEADS_B // heads
    base = 3 * WIDTH_A // width
    idx = jnp.arange(tq)
    tri = (idx[:, None] >= idx[None, :]).astype(_BF16)
    assert groups == 1
    chunk = lambda part, lag: pl.BlockSpec(
        (1, tq, width), lambda bi, hg, i: (bi, jnp.maximum(i - lag, 0), base + part * groups + hg))
    return pl.pallas_call(
        functools.partial(_stick_kernel, tq=tq, heads=heads,
                          k_col=(base + groups) * width, v_col=(base + 2 * groups) * width),
        grid=(b, groups, s // tq),
        in_specs=[
            chunk(0, 0), chunk(1, 0), chunk(1, 1), chunk(2, 0), chunk(2, 1),
            pl.BlockSpec(memory_space=pl.ANY),
            pl.BlockSpec((tq, tq), lambda bi, hg, i: (0, 0)),
        ],
        out_specs=pl.BlockSpec((1, tq, width), lambda bi, hg, i: (bi, i, hg)),
        out_shape=jax.ShapeDtypeStruct((b, s, WIDTH_B), _BF16),
        scratch_shapes=[pltpu.VMEM((tq, width), _BF16), pltpu.VMEM((tq, width), _BF16)],
        compiler_params=_params(("parallel", "parallel", "arbitrary")),
        name="stick_breaking_attention",
    )(proj3, proj3, proj3, proj3, proj3, proj3, tri)


def _merge_kernel(ya_ref, yb_ref, ga_ref, gb_ref, h_ref, wa_ref, wb_ref, wo_ref, g_ref, o_ref):
    a = jnp.dot(ya_ref[...], wa_ref[...], preferred_element_type=_F32)
    bb = jnp.dot(yb_ref[...], wb_ref[...], preferred_element_type=_F32)
    m = (ga_ref[...].astype(_F32) * a + gb_ref[...].astype(_F32) * bb).astype(_BF16)
    mix = jnp.dot(m, wo_ref[...], preferred_element_type=_F32)
    o_ref[...] = h_ref[...] + _rmsnorm_rows(mix, g_ref[...])


def _merge(ya, yb, proj, h2d, wa, wb, wo, g, layer, *, tm):
    t, d = h2d.shape
    assert t % tm == 0 and QKV_WIDTH % d == 0
    gate_blk = QKV_WIDTH // d
    resident = lambda w: pl.BlockSpec((None,) + w.shape[1:], lambda i: (layer, 0, 0), pipeline_mode=pl.Buffered(1))
    return pl.pallas_call(
        _merge_kernel,
        grid=(t // tm,),
        in_specs=[
            pl.BlockSpec((tm, A_OUT), lambda i: (i, 0)),
            pl.BlockSpec((tm, WIDTH_B), lambda i: (i, 0)),
            pl.BlockSpec((tm, d), lambda i: (i, gate_blk)),
            pl.BlockSpec((tm, d), lambda i: (i, gate_blk + 1)),
            pl.BlockSpec((tm, d), lambda i: (i, 0)),
            resident(wa), resident(wb), resident(wo), _layer_rows(layer, d),
        ],
        out_specs=pl.BlockSpec((tm, d), lambda i: (i, 0)),
        out_shape=jax.ShapeDtypeStruct((t, d), _F32),
        compiler_params=_params(("parallel",)),
        name="gated_merge",
    )(ya, yb, proj, proj, h2d, wa, wb, wo, g)


def _ffn_kernel(h_ref, hp_ref, gpre_ref, wg_ref, wv_ref, cwg_ref, cwv_ref, cbg_ref, cbv_ref, wd_ref,
                gpost_ref, o_ref, u_ref, *, tiles_per_seq, chunk):
    i = pl.program_id(0)
    j = pl.program_id(1)
    tm = h_ref.shape[0]
    n_chunks = tm // chunk

    @pl.when(j == 0)
    def _():
        halo = _rmsnorm_rows(hp_ref[...], gpre_ref[...])
        at_sequence_start = (i % tiles_per_seq) == 0
        u_ref[0:CONV_HALO, :] = jnp.where(at_sequence_start, 0.0, halo).astype(u_ref.dtype)
        o_ref[...] = jnp.zeros(o_ref.shape, o_ref.dtype)

        def body(c, carry):
            r0 = pl.multiple_of(c * chunk, chunk)
            x = h_ref[pl.ds(r0, chunk), :]
            u_ref[pl.ds(CONV_HALO + r0, chunk), :] = _rmsnorm_rows(x, gpre_ref[...]).astype(u_ref.dtype)
            return carry
        lax.fori_loop(0, n_chunks, body, 0)

    u = u_ref[...]

    def conv_branch(w_ref, cw_ref, cb_ref):
        x = jnp.dot(u, w_ref[...], preferred_element_type=_F32)
        cw = cw_ref[...]
        y = cb_ref[...] + cw[0:1, :] * pltpu.roll(x, 2, 0)
        y = y + cw[1:2, :] * pltpu.roll(x, 1, 0)
        y = y + cw[2:3, :] * x
        return y[CONV_HALO:, :]

    gate = conv_branch(wg_ref, cwg_ref, cbg_ref)
    value = conv_branch(wv_ref, cwv_ref, cbv_ref)
    act = (jax.nn.gelu(gate, approximate=True) * value).astype(_BF16)
    o_ref[...] += jnp.dot(act, wd_ref[...], preferred_element_type=_F32)

    @pl.when(j == pl.num_programs(1) - 1)
    def _():
        def body(c, carry):
            r0 = pl.multiple_of(c * chunk, chunk)
            f = o_ref[pl.ds(r0, chunk), :]
            o_ref[pl.ds(r0, chunk), :] = h_ref[pl.ds(r0, chunk), :] + _rmsnorm_rows(f, gpost_ref[...])
            return carry
        lax.fori_loop(0, n_chunks, body, 0)


def _ffn(h2d, gpre, w_up, conv_w, conv_b, w_down, gpost, layer, *, tm, tf, seq):
    t, d = h2d.shape
    d_ff = w_down.shape[1]
    assert t % tm == 0 and seq % tm == 0 and d_ff % tf == 0 and tm % CONV_HALO == 0
    nj = d_ff // tf
    halo_blocks = tm // CONV_HALO
    return pl.pallas_call(
        functools.partial(_ffn_kernel, tiles_per_seq=seq // tm, chunk=min(tm, 128)),
        grid=(t // tm, nj),
        in_specs=[
            pl.BlockSpec((tm, d), lambda i, j: (i, 0)),
            pl.BlockSpec((CONV_HALO, d), lambda i, j: (jnp.maximum(i * halo_blocks - 1, 0), 0)),
            _layer_rows(layer, d),
            pl.BlockSpec((None, d, tf), lambda i, j: (layer, 0, j)),
            pl.BlockSpec((None, d, tf), lambda i, j: (layer, 0, nj + j)),
            pl.BlockSpec((None, CONV_WIDTH, tf), lambda i, j: (layer, 0, j)),
            pl.BlockSpec((None, CONV_WIDTH, tf), lambda i, j: (layer, 0, nj + j)),
            pl.BlockSpec((None, 1, tf), lambda i, j: (layer, 0, j)),
            pl.BlockSpec((None, 1, tf), lambda i, j: (layer, 0, nj + j)),
            pl.BlockSpec((None, tf, d), lambda i, j: (layer, j, 0)),
            _layer_rows(layer, d),
        ],
        out_specs=pl.BlockSpec((tm, d), lambda i, j: (i, 0)),
        out_shape=jax.ShapeDtypeStruct((t, d), _F32),
        scratch_shapes=[pltpu.VMEM((CONV_HALO + tm, d), _BF16)],
        compiler_params=_params(("parallel", "arbitrary")),
        name="conv_ffn",
    )(h2d, h2d, gpre, w_up, w_up, conv_w, conv_w, conv_b, conv_b, w_down, gpost)


def _ple_kernel(h_ref, p_ref, wg_ref, wi_ref, o_ref):
    h = h_ref[...]
    gate = jax.nn.sigmoid(jnp.dot(h.astype(_BF16), wg_ref[...], preferred_element_type=_F32))
    e = jnp.dot(p_ref[...].astype(_BF16), wi_ref[...], preferred_element_type=_F32)
    o_ref[...] = h + gate * e


def _ple(h2d, p3d, w_gate, w_in, layer, *, tm):
    t, d = h2d.shape
    pd = p3d.shape[2]
    assert t % tm == 0
    resident = lambda w: pl.BlockSpec((None,) + w.shape[1:], lambda i: (layer, 0, 0), pipeline_mode=pl.Buffered(1))
    return pl.pallas_call(
        _ple_kernel,
        grid=(t // tm,),
        in_specs=[
            pl.BlockSpec((tm, d), lambda i: (i, 0)),
            pl.BlockSpec((None, tm, pd), lambda i: (layer, i, 0)),
            resident(w_gate),
            resident(w_in),
        ],
        out_specs=pl.BlockSpec((tm, d), lambda i: (i, 0)),
        out_shape=jax.ShapeDtypeStruct((t, d), _F32),
        compiler_params=_params(("parallel",)),
        name="layer_embedding",
    )(h2d, p3d, w_gate, w_in)


def _tiles(t, seq, d, d_in, d_ff):
    def largest(n, cands):
        return next(c for c in cands if n % c == 0)
    return dict(
        proj_tm=largest(t, (1024, 512, 256, 128)),
        proj_tn=largest(math.gcd(d_in, QKV_WIDTH), (2048, 1024, 512, 256, 128)),
        stick_tq=largest(seq, (256, 128)),
        merge_tm=largest(t, (512, 256, 128)),
        ffn_tm=largest(seq, (1024, 512, 256, 128)),
        ffn_tf=largest(d_ff, (512, 256, 128)),
        ple_tm=largest(t, (1024, 512, 256, 128)),
    )


def kernel(x, p, g_mix_pre, w_in, w_branch_a, w_branch_b, w_out, g_mix_post, g_ffn_pre, w_up, conv_w,
           conv_b, w_down, g_ffn_post, w_ple_in, w_ple_gate):
    b, s, d = x.shape
    depth = w_in.shape[0]
    d_in = w_in.shape[2]
    d_ff = w_down.shape[1]
    t = b * s
    assert d_in == QKV_WIDTH + 2 * d
    tiles = _tiles(t, s, d, d_in, d_ff)

    q_scale = 1.0 / math.sqrt(HEAD_DIM)
    col = jnp.arange(d_in)
    is_q = (col < WIDTH_A) | ((col >= 3 * WIDTH_A) & (col < 3 * WIDTH_A + WIDTH_B))
    colscale = jnp.where(is_q, q_scale, 1.0).astype(_F32)[None, :]

    bf16 = lambda w: w.astype(_BF16)
    rows = lambda v: v[:, None, :]
    w_in, w_branch_a, w_branch_b, w_out = bf16(w_in), bf16(w_branch_a), bf16(w_branch_b), bf16(w_out)
    w_up, w_down, w_ple_in, w_ple_gate = bf16(w_up), bf16(w_down), bf16(w_ple_in), bf16(w_ple_gate)
    p3d = p.reshape(depth, t, p.shape[-1])

    h = x.reshape(t, d)
    for i in range(depth):
        proj = _in_proj(h, rows(g_mix_pre), colscale, w_in, i, tm=tiles["proj_tm"], tn=tiles["proj_tn"])
        proj3 = proj.reshape(b, s, d_in)
        ya = _dilated_attention(proj3).reshape(t, A_OUT)
        yb = _stick_attention(proj3, tq=tiles["stick_tq"], heads=N_HEADS_B).reshape(t, WIDTH_B)
        h = _merge(ya, yb, proj, h, w_branch_a, w_branch_b, w_out, rows(g_mix_post), i, tm=tiles["merge_tm"])
        h = _ffn(h, rows(g_ffn_pre), w_up, conv_w, rows(conv_b), w_down, rows(g_ffn_post), i,
                 tm=tiles["ffn_tm"], tf=tiles["ffn_tf"], seq=s)
        h = _ple(h, p3d, w_ple_gate, w_ple_in, i, tm=tiles["ple_tm"])
    return h.reshape(b, s, d)
```

```python
import functools
import math

import jax
import jax.numpy as jnp
from jax import lax
from jax.experimental import pallas as pl
from jax.experimental.pallas import tpu as pltpu

HEAD_DIM = 128
DIL_GROUPS = ((128, 1), (512, 4), (2048, 16))
NEAR_GROUPS = DIL_GROUPS[:-1]
HEADS_PER_GROUP = 4
N_HEADS_A = HEADS_PER_GROUP * len(DIL_GROUPS)
N_HEADS_B = 4
WIDTH_A = N_HEADS_A * HEAD_DIM
WIDTH_B = N_HEADS_B * HEAD_DIM
A_OUT = HEADS_PER_GROUP * HEAD_DIM
QKV_WIDTH = 3 * WIDTH_A + 3 * WIDTH_B
CONV_WIDTH = 3
RMS_EPS = 1e-6
ALIBI_MAX = 8.0
MASKED = -1e30
EXP_UNDERFLOW = -110.0
QBLK = 128
SPAN = 2048
CONV_HALO = 16
VMEM_LIMIT = 56 * 1024 * 1024

_F32 = jnp.float32
_BF16 = jnp.bfloat16


def _rmsnorm_rows(x, g):
    ms = jnp.mean(x * x, axis=-1, keepdims=True)
    return x * lax.rsqrt(ms + RMS_EPS) * g


def _params(sem):
    return pltpu.CompilerParams(dimension_semantics=sem, vmem_limit_bytes=VMEM_LIMIT)


def _layer_rows(layer, width):
    return pl.BlockSpec((None, 1, width), lambda *_: (layer, 0, 0))


def _in_proj_kernel(x_ref, g_ref, cs_ref, w_ref, o_ref, u_ref, *, gate_tile0, chunk):
    j = pl.program_id(1)
    tm = x_ref.shape[0]

    @pl.when(j == 0)
    def _():
        def body(c, carry):
            r0 = pl.multiple_of(c * chunk, chunk)
            x = x_ref[pl.ds(r0, chunk), :]
            u_ref[pl.ds(r0, chunk), :] = _rmsnorm_rows(x, g_ref[...]).astype(u_ref.dtype)
            return carry
        lax.fori_loop(0, tm // chunk, body, 0)

    acc = jnp.dot(u_ref[...], w_ref[...], preferred_element_type=_F32)

    o_ref[...] = jnp.where(j >= gate_tile0, jax.nn.sigmoid(acc), acc * cs_ref[...]).astype(o_ref.dtype)


def _in_proj(h2d, g, colscale, w, layer, *, tm, tn):
    t, d = h2d.shape
    d_in = w.shape[2]
    assert t % tm == 0 and d_in % tn == 0 and QKV_WIDTH % tn == 0
    return pl.pallas_call(
        functools.partial(_in_proj_kernel, gate_tile0=QKV_WIDTH // tn, chunk=min(tm, 128)),
        grid=(t // tm, d_in // tn),
        in_specs=[
            pl.BlockSpec((tm, d), lambda i, j: (i, 0)),
            _layer_rows(layer, d),
            pl.BlockSpec((1, tn), lambda i, j: (0, j)),
            pl.BlockSpec((None, d, tn), lambda i, j: (layer, 0, j)),
        ],
        out_specs=pl.BlockSpec((tm, tn), lambda i, j: (i, j)),
        out_shape=jax.ShapeDtypeStruct((t, d_in), _BF16),
        scratch_shapes=[pltpu.VMEM((tm, d), _BF16)],
        compiler_params=_params(("parallel", "arbitrary")),
        name="in_proj",
    )(h2d, g, colscale, w)


def _dilated_kernel(q0, q1, kp0, kp1, kc0, kc1, vp0, vp1, vc0, vc1, b0, b1,
                    fq, fkp, fkc, fvp, fvc, fbias, o_ref):
    i = pl.program_id(2)
    q_refs, bias_refs = (q0, q1), (b0, b1)
    k_refs, v_refs = ((kp0, kc0), (kp1, kc1)), ((vp0, vc0), (vp1, vc1))

    def far_group():
        residue_major = lambda ref: pltpu.einshape("mrd->rmd", ref[0])
        token_major = lambda rows: pltpu.einshape("rmd->mrd", jnp.stack(rows)).reshape(SPAN, HEAD_DIM)
        qf = residue_major(fq)
        kf = jnp.concatenate([residue_major(fkp), residue_major(fkc)], axis=1)
        vf = jnp.concatenate([residue_major(fvp), residue_major(fvc)], axis=1)
        kj = lax.broadcasted_iota(jnp.int32, (QBLK, 2 * QBLK), 1)
        in_sequence = kj >= jnp.where(i == 0, QBLK, 0)
        residues = range(qf.shape[0])
        scores = [jnp.where(in_sequence,
                            lax.dot_general(qf[r], kf[r], (((1,), (1,)), ((), ())), preferred_element_type=_F32)
                            + fbias[0], MASKED) for r in residues]
        outs, lses = [], []
        for r in residues:
            m = jnp.max(scores[r], axis=-1, keepdims=True)
            p = jnp.exp(scores[r] - m)
            denom = jnp.sum(p, axis=-1, keepdims=True)
            outs.append(jnp.dot(p.astype(_BF16), vf[r], preferred_element_type=_F32) / denom)
            lses.append(jnp.broadcast_to(m + jnp.log(denom), (QBLK, HEAD_DIM)))
        return token_major(outs), token_major(lses)

    far_out, far_lse_lanes = far_group()

    def window_rows(prev_cur, o, window):
        prev_ref, cur_ref = prev_cur
        n_prev = max(window - o, 0)
        cur = cur_ref[0, max(o - window, 0):o + QBLK, :]
        if n_prev == 0:
            return cur, 0
        return jnp.concatenate([prev_ref[0, SPAN - n_prev:SPAN, :], cur], axis=0), n_prev

    def masked_scores(o):
        scores = []
        for g, (window, _) in enumerate(NEAR_GROUPS):
            kk, n_prev = window_rows(k_refs[g], o, window)
            sc = lax.dot_general(q_refs[g][0, o:o + QBLK, :], kk, (((1,), (1,)), ((), ())),
                                 preferred_element_type=_F32)
            sc = sc + bias_refs[g][0]
            if n_prev:
                kj = lax.broadcasted_iota(jnp.int32, sc.shape, 1)
                sc = jnp.where(jnp.logical_or(kj >= n_prev, i > 0), sc, MASKED)
            scores.append(sc)
        return scores

    def joint_softmax_weights(scores, o):
        far_lse = far_lse_lanes[o:o + QBLK, 0:1]
        m = functools.reduce(jnp.maximum, [jnp.max(sc, axis=-1, keepdims=True) for sc in scores] + [far_lse])
        ps = [jnp.exp(sc - m) for sc in scores]
        far_weight = jnp.exp(far_lse - m)
        denom = functools.reduce(jnp.add, [jnp.sum(p, axis=-1, keepdims=True) for p in ps] + [far_weight])
        return [p.astype(_BF16) for p in ps], far_weight, denom

    def weighted_values(ps, far_weight, o):
        acc = far_weight * far_out[o:o + QBLK, :]
        for g, (window, _) in enumerate(NEAR_GROUPS):
            acc = acc + jnp.dot(ps[g], window_rows(v_refs[g], o, window)[0], preferred_element_type=_F32)
        return acc

    offs = range(0, SPAN, QBLK)
    staged = [masked_scores(o) for o in offs]
    weights = [joint_softmax_weights(scores, o) for o, scores in zip(offs, staged)]
    for o, (ps, far_weight, denom) in zip(offs, weights):
        o_ref[0, o:o + QBLK, :] = (weighted_values(ps, far_weight, o) / denom).astype(o_ref.dtype)


def _alibi_slopes(group):
    slopes = jnp.exp2(-ALIBI_MAX * jnp.arange(1, N_HEADS_A + 1, dtype=_F32) / N_HEADS_A)
    return slopes[group * HEADS_PER_GROUP:(group + 1) * HEADS_PER_GROUP]


def _dilated_bias_tables():
    tables = []
    for g, (window, dilation) in enumerate(NEAR_GROUPS):
        qi = jnp.arange(QBLK)[:, None]
        kj = jnp.arange(window + QBLK)[None, :]
        dist = qi + window - kj
        valid = (dist >= 0) & (dist <= window) & (dist % dilation == 0)
        bias = -_alibi_slopes(g)[:, None, None] * dist.astype(_F32)[None]
        tables.append(jnp.where(valid[None], bias, MASKED))
    return tables


def _far_bias_table():
    window, dilation = DIL_GROUPS[-1]
    qi = jnp.arange(QBLK)[:, None]
    kj = jnp.arange(2 * QBLK)[None, :]
    steps = QBLK + qi - kj
    valid = (steps >= 0) & (steps <= window // dilation)
    bias = -_alibi_slopes(len(DIL_GROUPS) - 1)[:, None, None] * (steps * dilation).astype(_F32)[None]
    return jnp.where(valid[None], bias, MASKED)


def _dilated_attention(proj3):
    b, s, d_in = proj3.shape
    far_window, far_dilation = DIL_GROUPS[-1]
    assert s % SPAN == 0 and SPAN == QBLK * far_dilation == far_window
    n_heads = N_HEADS_A
    far_group = len(DIL_GROUPS) - 1

    def col_spec(base, lag):
        return [pl.BlockSpec((1, SPAN, HEAD_DIM), (lambda bi, hh, i, c=base + g * HEADS_PER_GROUP:
                                                   (bi, jnp.maximum(i - lag, 0), c + hh)))
                for g in range(len(NEAR_GROUPS))]

    def far_spec(base, lag):
        return pl.BlockSpec((1, QBLK, far_dilation, HEAD_DIM),
                            lambda bi, hh, i, c=base + far_group * HEADS_PER_GROUP:
                            (bi, jnp.maximum(i - lag, 0), 0, c + hh))

    head_table = lambda table: pl.BlockSpec((1,) + table.shape[1:], lambda bi, hh, i: (hh, 0, 0))
    tables = _dilated_bias_tables() + [_far_bias_table()]
    near_specs = (col_spec(0, 0) + col_spec(n_heads, 1) + col_spec(n_heads, 0)
                  + col_spec(2 * n_heads, 1) + col_spec(2 * n_heads, 0))
    far_specs = [far_spec(0, 0), far_spec(n_heads, 1), far_spec(n_heads, 0),
                 far_spec(2 * n_heads, 1), far_spec(2 * n_heads, 0)]
    far_view = proj3.reshape(b, s // far_dilation, far_dilation, d_in)
    n_near = len(NEAR_GROUPS)
    return pl.pallas_call(
        _dilated_kernel,
        grid=(b, HEADS_PER_GROUP, s // SPAN),
        in_specs=(near_specs + [head_table(tb) for tb in tables[:n_near]] + far_specs + [head_table(tables[-1])]),
        out_specs=pl.BlockSpec((1, SPAN, HEAD_DIM), lambda bi, hh, i: (bi, i, hh)),
        out_shape=jax.ShapeDtypeStruct((b, s, A_OUT), _BF16),
        compiler_params=_params(("parallel", "parallel", "parallel")),
        name="dilated_attention",
    )(*([proj3] * (5 * n_near)), *tables[:n_near], *([far_view] * 5), tables[-1])


def _stick_kernel(q_ref, k_ref, v_ref, tri_ref, o_ref, *, tq, heads):
    i = pl.program_id(2)
    row = lax.broadcasted_iota(jnp.int32, (tq, tq), 0)
    col = lax.broadcasted_iota(jnp.int32, (tq, tq), 1)
    before = col < row

    def chunk(c, afters, accs):
        k0 = pl.multiple_of(c * tq, tq)
        tri = tri_ref[...]
        lanes = [slice(h * HEAD_DIM, (h + 1) * HEAD_DIM) for h in range(heads)]
        zs = [lax.dot_general(q_ref[0, :, lanes[h]], k_ref[0, pl.ds(k0, tq), lanes[h]],
                              (((1,), (1,)), ((), ())), preferred_element_type=_F32) for h in range(heads)]
        csums = []
        for h in range(heads):
            z = zs[h]
            ln = -(jnp.maximum(z, 0.0) + jnp.log(1.0 + jnp.exp(-jnp.abs(z))))
            hi = ln.astype(_BF16)
            lo = (ln - hi.astype(_F32)).astype(_BF16)
            csums.append(jnp.dot(hi, tri, preferred_element_type=_F32)
                         + jnp.dot(lo, tri, preferred_element_type=_F32))
        new_afters, new_accs = [], []
        for h in range(heads):
            w = jnp.exp(zs[h] + csums[h] + afters[h])
            new_accs.append(accs[h] + jnp.dot(w.astype(_BF16), v_ref[0, pl.ds(k0, tq), lanes[h]],
                                              preferred_element_type=_F32))
            new_afters.append(afters[h] + csums[h][:, 0:1])
        return tuple(new_afters), tuple(new_accs)

    def first_two_chunks():
        has_prev = i > 0
        tri = tri_ref[...]
        lanes = [slice(h * HEAD_DIM, (h + 1) * HEAD_DIM) for h in range(heads)]
        starts = (pl.multiple_of(i * tq, tq), pl.multiple_of(jnp.maximum(i - 1, 0) * tq, tq))
        keep = (before, has_prev)
        zs = [[lax.dot_general(q_ref[0, :, lanes[h]], k_ref[0, pl.ds(k0, tq), lanes[h]],
                               (((1,), (1,)), ((), ())), preferred_element_type=_F32) for h in range(heads)]
              for k0 in starts]
        csums = []
        for n in range(2):
            csums.append([])
            for h in range(heads):
                z = zs[n][h]
                ln = jnp.where(keep[n], -(jnp.maximum(z, 0.0) + jnp.log(1.0 + jnp.exp(-jnp.abs(z)))), 0.0)
                hi = ln.astype(_BF16)
                lo = (ln - hi.astype(_F32)).astype(_BF16)
                csums[n].append(jnp.dot(hi, tri, preferred_element_type=_F32)
                                + jnp.dot(lo, tri, preferred_element_type=_F32))
        afters, accs = [], []
        for h in range(heads):
            after = csums[0][h][:, 0:1]
            w_diag = jnp.where(before, jnp.exp(zs[0][h] + csums[0][h]), 0.0)
            w_prev = jnp.where(has_prev, jnp.exp(zs[1][h] + csums[1][h] + after), 0.0)
            accs.append(jnp.dot(w_diag.astype(_BF16), v_ref[0, pl.ds(starts[0], tq), lanes[h]],
                                preferred_element_type=_F32)
                        + jnp.dot(w_prev.astype(_BF16), v_ref[0, pl.ds(starts[1], tq), lanes[h]],
                                  preferred_element_type=_F32))
            afters.append(after + csums[1][h][:, 0:1])
        return tuple(afters), tuple(accs)

    afters, accs = first_two_chunks()

    def keep_going(carry):
        c, afters, _ = carry
        return jnp.logical_and(c >= 0, jnp.max(functools.reduce(jnp.maximum, afters)) > EXP_UNDERFLOW)

    def body(carry):
        c, afters, accs = carry
        afters, accs = chunk(c, afters, accs)
        return c - 1, afters, accs

    _, _, accs = lax.while_loop(keep_going, body, (i - 2, afters, accs))
    for h in range(heads):
        o_ref[0, :, h * HEAD_DIM:(h + 1) * HEAD_DIM] = accs[h].astype(o_ref.dtype)


def _stick_attention(proj3, *, tq, heads):
    b, s, _ = proj3.shape
    width = heads * HEAD_DIM
    assert s % tq == 0 and N_HEADS_B % heads == 0 and (3 * WIDTH_A) % width == 0
    groups = N_HEADS_B // heads
    base = 3 * WIDTH_A // width
    idx = jnp.arange(tq)
    tri = (idx[:, None] >= idx[None, :]).astype(_BF16)
    whole_seq = lambda part: pl.BlockSpec((1, s, width), lambda bi, hg, i: (bi, 0, base + part * groups + hg),
                                          pipeline_mode=pl.Buffered(1))
    return pl.pallas_call(
        functools.partial(_stick_kernel, tq=tq, heads=heads),
        grid=(b, groups, s // tq),
        in_specs=[
            pl.BlockSpec((1, tq, width), lambda bi, hg, i: (bi, i, base + hg)),
            whole_seq(1),
            whole_seq(2),
            pl.BlockSpec((tq, tq), lambda bi, hg, i: (0, 0)),
        ],
        out_specs=pl.BlockSpec((1, tq, width), lambda bi, hg, i: (bi, i, hg)),
        out_shape=jax.ShapeDtypeStruct((b, s, WIDTH_B), _BF16),
        compiler_params=_params(("parallel", "parallel", "arbitrary")),
        name="stick_breaking_attention",
    )(proj3, proj3, proj3, tri)


def _merge_kernel(ya_ref, yb_ref, ga_ref, gb_ref, h_ref, wa_ref, wb_ref, wo_ref, g_ref, o_ref):
    a = jnp.dot(ya_ref[...], wa_ref[...], preferred_element_type=_F32)
    bb = jnp.dot(yb_ref[...], wb_ref[...], preferred_element_type=_F32)
    m = (ga_ref[...].astype(_F32) * a + gb_ref[...].astype(_F32) * bb).astype(_BF16)
    mix = jnp.dot(m, wo_ref[...], preferred_element_type=_F32)
    o_ref[...] = h_ref[...] + _rmsnorm_rows(mix, g_ref[...])


def _merge(ya, yb, proj, h2d, wa, wb, wo, g, layer, *, tm):
    t, d = h2d.shape
    assert t % tm == 0 and QKV_WIDTH % d == 0
    gate_blk = QKV_WIDTH // d
    resident = lambda w: pl.BlockSpec((None,) + w.shape[1:], lambda i: (layer, 0, 0), pipeline_mode=pl.Buffered(1))
    return pl.pallas_call(
        _merge_kernel,
        grid=(t // tm,),
        in_specs=[
            pl.BlockSpec((tm, A_OUT), lambda i: (i, 0)),
            pl.BlockSpec((tm, WIDTH_B), lambda i: (i, 0)),
            pl.BlockSpec((tm, d), lambda i: (i, gate_blk)),
            pl.BlockSpec((tm, d), lambda i: (i, gate_blk + 1)),
            pl.BlockSpec((tm, d), lambda i: (i, 0)),
            resident(wa), resident(wb), resident(wo), _layer_rows(layer, d),
        ],
        out_specs=pl.BlockSpec((tm, d), lambda i: (i, 0)),
        out_shape=jax.ShapeDtypeStruct((t, d), _F32),
        compiler_params=_params(("parallel",)),
        name="gated_merge",
    )(ya, yb, proj, proj, h2d, wa, wb, wo, g)


def _ffn_kernel(h_ref, hp_ref, gpre_ref, w_ref, cw_ref, cb_ref, wd_ref,
                gpost_ref, o_ref, u_ref, *, tiles_per_seq, chunk):
    i = pl.program_id(0)
    j = pl.program_id(1)
    tm = h_ref.shape[0]
    n_chunks = tm // chunk

    @pl.when(j == 0)
    def _():
        halo = _rmsnorm_rows(hp_ref[...], gpre_ref[...])
        at_sequence_start = (i % tiles_per_seq) == 0
        u_ref[0:CONV_HALO, :] = jnp.where(at_sequence_start, 0.0, halo).astype(u_ref.dtype)
        o_ref[...] = jnp.zeros(o_ref.shape, o_ref.dtype)

        def body(c, carry):
            r0 = pl.multiple_of(c * chunk, chunk)
            x = h_ref[pl.ds(r0, chunk), :]
            u_ref[pl.ds(CONV_HALO + r0, chunk), :] = _rmsnorm_rows(x, gpre_ref[...]).astype(u_ref.dtype)
            return carry
        lax.fori_loop(0, n_chunks, body, 0)

    x = jnp.dot(u_ref[...], w_ref[...], preferred_element_type=_F32)
    cw = cw_ref[...]
    y = cb_ref[...] + cw[0:1, :] * pltpu.roll(x, 2, 0)
    y = y + cw[1:2, :] * pltpu.roll(x, 1, 0)
    y = (y + cw[2:3, :] * x)[CONV_HALO:, :]
    tf = y.shape[1] // 2
    act = (jax.nn.gelu(y[:, :tf], approximate=True) * y[:, tf:]).astype(_BF16)
    o_ref[...] += jnp.dot(act, wd_ref[...], preferred_element_type=_F32)

    @pl.when(j == pl.num_programs(1) - 1)
    def _():
        def body(c, carry):
            r0 = pl.multiple_of(c * chunk, chunk)
            f = o_ref[pl.ds(r0, chunk), :]
            o_ref[pl.ds(r0, chunk), :] = h_ref[pl.ds(r0, chunk), :] + _rmsnorm_rows(f, gpost_ref[...])
            return carry
        lax.fori_loop(0, n_chunks, body, 0)


def _ffn(h2d, gpre, w_up, conv_w, conv_b, w_down, gpost, layer, *, tm, tf, seq):
    t, d = h2d.shape
    d_ff = w_down.shape[1]
    assert t % tm == 0 and seq % tm == 0 and d_ff % tf == 0 and tm % CONV_HALO == 0
    nj = d_ff // tf
    halo_blocks = tm // CONV_HALO
    return pl.pallas_call(
        functools.partial(_ffn_kernel, tiles_per_seq=seq // tm, chunk=min(tm, 128)),
        grid=(t // tm, nj),
        in_specs=[
            pl.BlockSpec((tm, d), lambda i, j: (i, 0)),
            pl.BlockSpec((CONV_HALO, d), lambda i, j: (jnp.maximum(i * halo_blocks - 1, 0), 0)),
            _layer_rows(layer, d),
            pl.BlockSpec((None, None, d, 2 * tf), lambda i, j: (layer, j, 0, 0)),
            pl.BlockSpec((None, None, CONV_WIDTH, 2 * tf), lambda i, j: (layer, j, 0, 0)),
            pl.BlockSpec((None, None, 1, 2 * tf), lambda i, j: (layer, j, 0, 0)),
            pl.BlockSpec((None, tf, d), lambda i, j: (layer, j, 0)),
            _layer_rows(layer, d),
        ],
        out_specs=pl.BlockSpec((tm, d), lambda i, j: (i, 0)),
        out_shape=jax.ShapeDtypeStruct((t, d), _F32),
        scratch_shapes=[pltpu.VMEM((CONV_HALO + tm, d), _BF16)],
        compiler_params=_params(("parallel", "arbitrary")),
        name="conv_ffn",
    )(h2d, h2d, gpre, w_up, conv_w, conv_b, w_down, gpost)


def _ple_kernel(h_ref, p_ref, wg_ref, wi_ref, o_ref):
    h = h_ref[...]
    gate = jax.nn.sigmoid(jnp.dot(h.astype(_BF16), wg_ref[...], preferred_element_type=_F32))
    e = jnp.dot(p_ref[...].astype(_BF16), wi_ref[...], preferred_element_type=_F32)
    o_ref[...] = h + gate * e


def _ple(h2d, p3d, w_gate, w_in, layer, *, tm):
    t, d = h2d.shape
    pd = p3d.shape[2]
    assert t % tm == 0
    resident = lambda w: pl.BlockSpec((None,) + w.shape[1:], lambda i: (layer, 0, 0), pipeline_mode=pl.Buffered(1))
    return pl.pallas_call(
        _ple_kernel,
        grid=(t // tm,),
        in_specs=[
            pl.BlockSpec((tm, d), lambda i: (i, 0)),
            pl.BlockSpec((None, tm, pd), lambda i: (layer, i, 0)),
            resident(w_gate),
            resident(w_in),
        ],
        out_specs=pl.BlockSpec((tm, d), lambda i: (i, 0)),
        out_shape=jax.ShapeDtypeStruct((t, d), _F32),
        compiler_params=_params(("parallel",)),
        name="layer_embedding",
    )(h2d, p3d, w_gate, w_in)


def _tiles(t, seq, d, d_in, d_ff):
    def largest(n, cands):
        return next(c for c in cands if n % c == 0)
    return dict(
        proj_tm=largest(t, (1024, 512, 256, 128)),
        proj_tn=largest(math.gcd(d_in, QKV_WIDTH), (2048, 1024, 512, 256, 128)),
        stick_tq=largest(seq, (256, 128)),
        merge_tm=largest(t, (512, 256, 128)),
        ffn_tm=largest(seq, (1024, 512, 256, 128)),
        ffn_tf=largest(d_ff, (512, 256, 128)),
        ple_tm=largest(t, (1024, 512, 256, 128)),
    )


def kernel(x, p, g_mix_pre, w_in, w_branch_a, w_branch_b, w_out, g_mix_post, g_ffn_pre, w_up, conv_w,
           conv_b, w_down, g_ffn_post, w_ple_in, w_ple_gate):
    b, s, d = x.shape
    depth = w_in.shape[0]
    d_in = w_in.shape[2]
    d_ff = w_down.shape[1]
    t = b * s
    assert d_in == QKV_WIDTH + 2 * d
    tiles = _tiles(t, s, d, d_in, d_ff)

    q_scale = 1.0 / math.sqrt(HEAD_DIM)
    col = jnp.arange(d_in)
    is_q = (col < WIDTH_A) | ((col >= 3 * WIDTH_A) & (col < 3 * WIDTH_A + WIDTH_B))
    colscale = jnp.where(is_q, q_scale, 1.0).astype(_F32)[None, :]

    bf16 = lambda w: w.astype(_BF16)
    rows = lambda v: v[:, None, :]
    w_in, w_branch_a, w_branch_b, w_out = bf16(w_in), bf16(w_branch_a), bf16(w_branch_b), bf16(w_out)
    w_down, w_ple_in, w_ple_gate = bf16(w_down), bf16(w_ple_in), bf16(w_ple_gate)
    p3d = p.reshape(depth, t, p.shape[-1])

    def chunk_major(a):
        rows_, tf = a.shape[1], tiles["ffn_tf"]
        a = a.reshape(depth, rows_, 2, d_ff // tf, tf)
        return jnp.transpose(a, (0, 3, 1, 2, 4)).reshape(depth, d_ff // tf, rows_, 2 * tf)

    w_up, conv_w, conv_b = chunk_major(bf16(w_up)), chunk_major(conv_w), chunk_major(rows(conv_b))

    h = x.reshape(t, d)
    for i in range(depth):
        proj = _in_proj(h, rows(g_mix_pre), colscale, w_in, i, tm=tiles["proj_tm"], tn=tiles["proj_tn"])
        proj3 = proj.reshape(b, s, d_in)
        ya = _dilated_attention(proj3).reshape(t, A_OUT)
        yb = _stick_attention(proj3, tq=tiles["stick_tq"], heads=N_HEADS_B).reshape(t, WIDTH_B)
        h = _merge(ya, yb, proj, h, w_branch_a, w_branch_b, w_out, rows(g_mix_post), i, tm=tiles["merge_tm"])
        h = _ffn(h, rows(g_ffn_pre), w_up, conv_w, conv_b, w_down, rows(g_ffn_post), i,
                 tm=tiles["ffn_tm"], tf=tiles["ffn_tf"], seq=s)
        h = _ple(h, p3d, w_ple_gate, w_ple_in, i, tm=tiles["ple_tm"])
    return h.reshape(b, s, d)
```

```python
import functools
import math

import jax
import jax.numpy as jnp
from jax import lax
from jax.experimental import pallas as pl
from jax.experimental.pallas import tpu as pltpu

HEAD_DIM = 128
DIL_GROUPS = ((128, 1), (512, 4), (2048, 16))
NEAR_GROUPS = DIL_GROUPS[:-1]
HEADS_PER_GROUP = 4
N_HEADS_A = HEADS_PER_GROUP * len(DIL_GROUPS)
N_HEADS_B = 4
WIDTH_A = N_HEADS_A * HEAD_DIM
WIDTH_B = N_HEADS_B * HEAD_DIM
A_OUT = HEADS_PER_GROUP * HEAD_DIM
QKV_WIDTH = 3 * WIDTH_A + 3 * WIDTH_B
CONV_WIDTH = 3
RMS_EPS = 1e-6
ALIBI_MAX = 8.0
MASKED = -1e30
EXP_UNDERFLOW = -110.0
QBLK = 128
SPAN = 2048
CONV_HALO = 16
VMEM_LIMIT = 56 * 1024 * 1024

_F32 = jnp.float32
_BF16 = jnp.bfloat16


def _rmsnorm_rows(x, g):
    ms = jnp.mean(x * x, axis=-1, keepdims=True)
    return x * lax.rsqrt(ms + RMS_EPS) * g


def _params(sem):
    return pltpu.CompilerParams(dimension_semantics=sem, vmem_limit_bytes=VMEM_LIMIT)


def _layer_rows(layer, width):
    return pl.BlockSpec((None, 1, width), lambda *_: (layer, 0, 0))


def _in_proj_kernel(x_ref, g_ref, cs_ref, w_ref, o_ref, u_ref, *, gate_tile0, chunk):
    j = pl.program_id(1)
    tm = x_ref.shape[0]

    @pl.when(j == 0)
    def _():
        def body(c, carry):
            r0 = pl.multiple_of(c * chunk, chunk)
            x = x_ref[pl.ds(r0, chunk), :]
            u_ref[pl.ds(r0, chunk), :] = _rmsnorm_rows(x, g_ref[...]).astype(u_ref.dtype)
            return carry
        lax.fori_loop(0, tm // chunk, body, 0)

    acc = jnp.dot(u_ref[...], w_ref[...], preferred_element_type=_F32)

    o_ref[...] = jnp.where(j >= gate_tile0, jax.nn.sigmoid(acc), acc * cs_ref[...]).astype(o_ref.dtype)


def _in_proj(h2d, g, colscale, w, layer, *, tm, tn):
    t, d = h2d.shape
    d_in = w.shape[2]
    assert t % tm == 0 and d_in % tn == 0 and QKV_WIDTH % tn == 0
    return pl.pallas_call(
        functools.partial(_in_proj_kernel, gate_tile0=QKV_WIDTH // tn, chunk=min(tm, 128)),
        grid=(t // tm, d_in // tn),
        in_specs=[
            pl.BlockSpec((tm, d), lambda i, j: (i, 0)),
            _layer_rows(layer, d),
            pl.BlockSpec((1, tn), lambda i, j: (0, j)),
            pl.BlockSpec((None, d, tn), lambda i, j: (layer, 0, j)),
        ],
        out_specs=pl.BlockSpec((tm, tn), lambda i, j: (i, j)),
        out_shape=jax.ShapeDtypeStruct((t, d_in), _BF16),
        scratch_shapes=[pltpu.VMEM((tm, d), _BF16)],
        compiler_params=_params(("parallel", "arbitrary")),
        name="in_proj",
    )(h2d, g, colscale, w)


def _dilated_kernel(q0, q1, kp0, kp1, kc0, kc1, vp0, vp1, vc0, vc1, b0, b1,
                    fq, fkp, fkc, fvp, fvc, fbias, o_ref):
    i = pl.program_id(2)
    q_refs, bias_refs = (q0, q1), (b0, b1)
    k_refs, v_refs = ((kp0, kc0), (kp1, kc1)), ((vp0, vc0), (vp1, vc1))

    residue_major = lambda ref: pltpu.einshape("mrd->rmd", ref[0])
    token_major = lambda rows: pltpu.einshape("rmd->mrd", jnp.stack(rows)).reshape(SPAN, HEAD_DIM)

    def far_scores():
        qf = residue_major(fq)
        kf = jnp.concatenate([residue_major(fkp), residue_major(fkc)], axis=1)
        kj = lax.broadcasted_iota(jnp.int32, (QBLK, 2 * QBLK), 1)
        in_sequence = kj >= jnp.where(i == 0, QBLK, 0)
        return [jnp.where(in_sequence,
                          lax.dot_general(qf[r], kf[r], (((1,), (1,)), ((), ())), preferred_element_type=_F32)
                          + fbias[0], MASKED) for r in range(qf.shape[0])]

    def far_outputs(scores):
        vf = jnp.concatenate([residue_major(fvp), residue_major(fvc)], axis=1)
        outs, lses = [], []
        for r, sc in enumerate(scores):
            m = jnp.max(sc, axis=-1, keepdims=True)
            p = jnp.exp(sc - m)
            denom = jnp.sum(p, axis=-1, keepdims=True)
            outs.append(jnp.dot(p.astype(_BF16), vf[r], preferred_element_type=_F32) / denom)
            lses.append(jnp.broadcast_to(m + jnp.log(denom), (QBLK, HEAD_DIM)))
        return token_major(outs), token_major(lses)

    def window_rows(prev_cur, o, window):
        prev_ref, cur_ref = prev_cur
        n_prev = max(window - o, 0)
        cur = cur_ref[0, max(o - window, 0):o + QBLK, :]
        if n_prev == 0:
            return cur, 0
        return jnp.concatenate([prev_ref[0, SPAN - n_prev:SPAN, :], cur], axis=0), n_prev

    def masked_scores(o):
        scores = []
        for g, (window, _) in enumerate(NEAR_GROUPS):
            kk, n_prev = window_rows(k_refs[g], o, window)
            sc = lax.dot_general(q_refs[g][0, o:o + QBLK, :], kk, (((1,), (1,)), ((), ())),
                                 preferred_element_type=_F32)
            sc = sc + bias_refs[g][0]
            if n_prev:
                kj = lax.broadcasted_iota(jnp.int32, sc.shape, 1)
                sc = jnp.where(jnp.logical_or(kj >= n_prev, i > 0), sc, MASKED)
            scores.append(sc)
        return scores

    def joint_softmax_weights(scores, far_lse):
        m = functools.reduce(jnp.maximum, [jnp.max(sc, axis=-1, keepdims=True) for sc in scores] + [far_lse])
        ps = [jnp.exp(sc - m) for sc in scores]
        far_weight = jnp.exp(far_lse - m)
        denom = functools.reduce(jnp.add, [jnp.sum(p, axis=-1, keepdims=True) for p in ps] + [far_weight])
        return [p.astype(_BF16) for p in ps], far_weight, denom

    def weighted_values(ps, far_weight, far_rows, o):
        acc = far_weight * far_rows
        for g, (window, _) in enumerate(NEAR_GROUPS):
            acc = acc + jnp.dot(ps[g], window_rows(v_refs[g], o, window)[0], preferred_element_type=_F32)
        return acc

    offs = range(0, SPAN, QBLK)
    far_staged = far_scores()
    staged = [masked_scores(o) for o in offs]
    far_out, far_lse = far_outputs(far_staged)
    weights = [joint_softmax_weights(scores, far_lse[o:o + QBLK, 0:1]) for o, scores in zip(offs, staged)]
    for o, (ps, far_weight, denom) in zip(offs, weights):
        y = weighted_values(ps, far_weight, far_out[o:o + QBLK, :], o) / denom
        o_ref[0, o:o + QBLK, :] = y.astype(o_ref.dtype)


def _alibi_slopes(group):
    slopes = jnp.exp2(-ALIBI_MAX * jnp.arange(1, N_HEADS_A + 1, dtype=_F32) / N_HEADS_A)
    return slopes[group * HEADS_PER_GROUP:(group + 1) * HEADS_PER_GROUP]


def _dilated_bias_tables():
    tables = []
    for g, (window, dilation) in enumerate(NEAR_GROUPS):
        qi = jnp.arange(QBLK)[:, None]
        kj = jnp.arange(window + QBLK)[None, :]
        dist = qi + window - kj
        valid = (dist >= 0) & (dist <= window) & (dist % dilation == 0)
        bias = -_alibi_slopes(g)[:, None, None] * dist.astype(_F32)[None]
        tables.append(jnp.where(valid[None], bias, MASKED))
    return tables


def _far_bias_table():
    window, dilation = DIL_GROUPS[-1]
    qi = jnp.arange(QBLK)[:, None]
    kj = jnp.arange(2 * QBLK)[None, :]
    steps = QBLK + qi - kj
    valid = (steps >= 0) & (steps <= window // dilation)
    bias = -_alibi_slopes(len(DIL_GROUPS) - 1)[:, None, None] * (steps * dilation).astype(_F32)[None]
    return jnp.where(valid[None], bias, MASKED)


def _dilated_attention(proj3):
    b, s, d_in = proj3.shape
    far_window, far_dilation = DIL_GROUPS[-1]
    assert s % SPAN == 0 and SPAN == QBLK * far_dilation == far_window
    n_heads = N_HEADS_A
    far_group = len(DIL_GROUPS) - 1

    def col_spec(base, lag):
        return [pl.BlockSpec((1, SPAN, HEAD_DIM), (lambda bi, hh, i, c=base + g * HEADS_PER_GROUP:
                                                   (bi, jnp.maximum(i - lag, 0), c + hh)))
                for g in range(len(NEAR_GROUPS))]

    def far_spec(base, lag):
        return pl.BlockSpec((1, QBLK, far_dilation, HEAD_DIM),
                            lambda bi, hh, i, c=base + far_group * HEADS_PER_GROUP:
                            (bi, jnp.maximum(i - lag, 0), 0, c + hh))

    head_table = lambda table: pl.BlockSpec((1,) + table.shape[1:], lambda bi, hh, i: (hh, 0, 0))
    tables = _dilated_bias_tables() + [_far_bias_table()]
    near_specs = (col_spec(0, 0) + col_spec(n_heads, 1) + col_spec(n_heads, 0)
                  + col_spec(2 * n_heads, 1) + col_spec(2 * n_heads, 0))
    far_specs = [far_spec(0, 0), far_spec(n_heads, 1), far_spec(n_heads, 0),
                 far_spec(2 * n_heads, 1), far_spec(2 * n_heads, 0)]
    far_view = proj3.reshape(b, s // far_dilation, far_dilation, d_in)
    n_near = len(NEAR_GROUPS)
    return pl.pallas_call(
        _dilated_kernel,
        grid=(b, HEADS_PER_GROUP, s // SPAN),
        in_specs=(near_specs + [head_table(tb) for tb in tables[:n_near]] + far_specs + [head_table(tables[-1])]),
        out_specs=pl.BlockSpec((1, SPAN, HEAD_DIM), lambda bi, hh, i: (bi, i, hh)),
        out_shape=jax.ShapeDtypeStruct((b, s, A_OUT), _BF16),
        compiler_params=_params(("parallel", "parallel", "parallel")),
        name="dilated_attention",
    )(*([proj3] * (5 * n_near)), *tables[:n_near], *([far_view] * 5), tables[-1])


def _stick_kernel(q_ref, k_ref, v_ref, tri_ref, o_ref, *, tq, heads):
    i = pl.program_id(2)
    row = lax.broadcasted_iota(jnp.int32, (tq, tq), 0)
    col = lax.broadcasted_iota(jnp.int32, (tq, tq), 1)
    before = col < row

    def chunk(c, afters, accs):
        k0 = pl.multiple_of(c * tq, tq)
        tri = tri_ref[...]
        lanes = [slice(h * HEAD_DIM, (h + 1) * HEAD_DIM) for h in range(heads)]
        zs = [lax.dot_general(q_ref[0, :, lanes[h]], k_ref[0, pl.ds(k0, tq), lanes[h]],
                              (((1,), (1,)), ((), ())), preferred_element_type=_F32) for h in range(heads)]
        csums = []
        for h in range(heads):
            z = zs[h]
            ln = -(jnp.maximum(z, 0.0) + jnp.log(1.0 + jnp.exp(-jnp.abs(z))))
            hi = ln.astype(_BF16)
            lo = (ln - hi.astype(_F32)).astype(_BF16)
            csums.append(jnp.dot(hi, tri, preferred_element_type=_F32)
                         + jnp.dot(lo, tri, preferred_element_type=_F32))
        new_afters, new_accs = [], []
        for h in range(heads):
            w = jnp.exp(zs[h] + csums[h] + afters[h])
            new_accs.append(accs[h] + jnp.dot(w.astype(_BF16), v_ref[0, pl.ds(k0, tq), lanes[h]],
                                              preferred_element_type=_F32))
            new_afters.append(afters[h] + csums[h][:, 0:1])
        return tuple(new_afters), tuple(new_accs)

    def first_two_chunks():
        has_prev = i > 0
        tri = tri_ref[...]
        lanes = [slice(h * HEAD_DIM, (h + 1) * HEAD_DIM) for h in range(heads)]
        starts = (pl.multiple_of(i * tq, tq), pl.multiple_of(jnp.maximum(i - 1, 0) * tq, tq))
        keep = (before, has_prev)
        zs = [[lax.dot_general(q_ref[0, :, lanes[h]], k_ref[0, pl.ds(k0, tq), lanes[h]],
                               (((1,), (1,)), ((), ())), preferred_element_type=_F32) for h in range(heads)]
              for k0 in starts]
        csums = []
        for n in range(2):
            csums.append([])
            for h in range(heads):
                z = zs[n][h]
                ln = jnp.where(keep[n], -(jnp.maximum(z, 0.0) + jnp.log(1.0 + jnp.exp(-jnp.abs(z)))), 0.0)
                hi = ln.astype(_BF16)
                lo = (ln - hi.astype(_F32)).astype(_BF16)
                csums[n].append(jnp.dot(hi, tri, preferred_element_type=_F32)
                                + jnp.dot(lo, tri, preferred_element_type=_F32))
        afters, accs = [], []
        for h in range(heads):
            after = csums[0][h][:, 0:1]
            w_diag = jnp.where(before, jnp.exp(zs[0][h] + csums[0][h]), 0.0)
            w_prev = jnp.where(has_prev, jnp.exp(zs[1][h] + csums[1][h] + after), 0.0)
            accs.append(jnp.dot(w_diag.astype(_BF16), v_ref[0, pl.ds(starts[0], tq), lanes[h]],
                                preferred_element_type=_F32)
                        + jnp.dot(w_prev.astype(_BF16), v_ref[0, pl.ds(starts[1], tq), lanes[h]],
                                  preferred_element_type=_F32))
            afters.append(after + csums[1][h][:, 0:1])
        return tuple(afters), tuple(accs)

    afters, accs = first_two_chunks()

    def keep_going(carry):
        c, afters, _ = carry
        return jnp.logical_and(c >= 0, jnp.max(functools.reduce(jnp.maximum, afters)) > EXP_UNDERFLOW)

    def body(carry):
        c, afters, accs = carry
        afters, accs = chunk(c, afters, accs)
        return c - 1, afters, accs

    _, _, accs = lax.while_loop(keep_going, body, (i - 2, afters, accs))
    for h in range(heads):
        o_ref[0, :, h * HEAD_DIM:(h + 1) * HEAD_DIM] = accs[h].astype(o_ref.dtype)


def _stick_attention(proj3, *, tq, heads):
    b, s, _ = proj3.shape
    width = heads * HEAD_DIM
    assert s % tq == 0 and N_HEADS_B % heads == 0 and (3 * WIDTH_A) % width == 0
    groups = N_HEADS_B // heads
    base = 3 * WIDTH_A // width
    idx = jnp.arange(tq)
    tri = (idx[:, None] >= idx[None, :]).astype(_BF16)
    whole_seq = lambda part: pl.BlockSpec((1, s, width), lambda bi, hg, i: (bi, 0, base + part * groups + hg),
                                          pipeline_mode=pl.Buffered(1))
    return pl.pallas_call(
        functools.partial(_stick_kernel, tq=tq, heads=heads),
        grid=(b, groups, s // tq),
        in_specs=[
            pl.BlockSpec((1, tq, width), lambda bi, hg, i: (bi, i, base + hg)),
            whole_seq(1),
            whole_seq(2),
            pl.BlockSpec((tq, tq), lambda bi, hg, i: (0, 0)),
        ],
        out_specs=pl.BlockSpec((1, tq, width), lambda bi, hg, i: (bi, i, hg)),
        out_shape=jax.ShapeDtypeStruct((b, s, WIDTH_B), _BF16),
        compiler_params=_params(("parallel", "parallel", "arbitrary")),
        name="stick_breaking_attention",
    )(proj3, proj3, proj3, tri)


def _merge_kernel(ya_ref, yb_ref, ga_ref, gb_ref, h_ref, wa_ref, wb_ref, wo_ref, g_ref, o_ref):
    half = ya_ref.shape[0] // 2
    rows = [slice(0, half), slice(half, 2 * half)]
    ab = [(jnp.dot(ya_ref[r, :], wa_ref[...], preferred_element_type=_F32),
           jnp.dot(yb_ref[r, :], wb_ref[...], preferred_element_type=_F32)) for r in rows]
    ms = [(ga_ref[r, :].astype(_F32) * a + gb_ref[r, :].astype(_F32) * bb).astype(_BF16) for r, (a, bb) in zip(rows, ab)]
    mixes = [jnp.dot(m, wo_ref[...], preferred_element_type=_F32) for m in ms]
    for r, mix in zip(rows, mixes):
        o_ref[r, :] = h_ref[r, :] + _rmsnorm_rows(mix, g_ref[...])


def _merge(ya, yb, proj, h2d, wa, wb, wo, g, layer, *, tm):
    t, d = h2d.shape
    assert t % tm == 0 and QKV_WIDTH % d == 0
    gate_blk = QKV_WIDTH // d
    resident = lambda w: pl.BlockSpec((None,) + w.shape[1:], lambda i: (layer, 0, 0), pipeline_mode=pl.Buffered(1))
    return pl.pallas_call(
        _merge_kernel,
        grid=(t // tm,),
        in_specs=[
            pl.BlockSpec((tm, A_OUT), lambda i: (i, 0)),
            pl.BlockSpec((tm, WIDTH_B), lambda i: (i, 0)),
            pl.BlockSpec((tm, d), lambda i: (i, gate_blk)),
            pl.BlockSpec((tm, d), lambda i: (i, gate_blk + 1)),
            pl.BlockSpec((tm, d), lambda i: (i, 0)),
            resident(wa), resident(wb), resident(wo), _layer_rows(layer, d),
        ],
        out_specs=pl.BlockSpec((tm, d), lambda i: (i, 0)),
        out_shape=jax.ShapeDtypeStruct((t, d), _F32),
        compiler_params=_params(("parallel",)),
        name="gated_merge",
    )(ya, yb, proj, proj, h2d, wa, wb, wo, g)


def _ffn_kernel(h_ref, hp_ref, gpre_ref, wg_ref, wv_ref, cwg_ref, cwv_ref, cbg_ref, cbv_ref, wd_ref,
                gpost_ref, o_ref, u_ref, *, tiles_per_seq, chunk):
    i = pl.program_id(0)
    j = pl.program_id(1)
    tm = h_ref.shape[0]
    n_chunks = tm // chunk

    @pl.when(j == 0)
    def _():
        halo = _rmsnorm_rows(hp_ref[...], gpre_ref[...])
        at_sequence_start = (i % tiles_per_seq) == 0
        u_ref[0:CONV_HALO, :] = jnp.where(at_sequence_start, 0.0, halo).astype(u_ref.dtype)
        o_ref[...] = jnp.zeros(o_ref.shape, o_ref.dtype)

        def body(c, carry):
            r0 = pl.multiple_of(c * chunk, chunk)
            x = h_ref[pl.ds(r0, chunk), :]
            u_ref[pl.ds(CONV_HALO + r0, chunk), :] = _rmsnorm_rows(x, gpre_ref[...]).astype(u_ref.dtype)
            return carry
        lax.fori_loop(0, n_chunks, body, 0)

    u = u_ref[...]

    def conv_branch(w_ref, cw_ref, cb_ref):
        x = jnp.dot(u, w_ref[...], preferred_element_type=_F32)
        cw = cw_ref[...]
        y = cb_ref[...] + cw[0:1, :] * pltpu.roll(x, 2, 0)
        y = y + cw[1:2, :] * pltpu.roll(x, 1, 0)
        y = y + cw[2:3, :] * x
        return y[CONV_HALO:, :]

    gate = conv_branch(wg_ref, cwg_ref, cbg_ref)
    value = conv_branch(wv_ref, cwv_ref, cbv_ref)
    act = (jax.nn.gelu(gate, approximate=True) * value).astype(_BF16)
    o_ref[...] += jnp.dot(act, wd_ref[...], preferred_element_type=_F32)

    @pl.when(j == pl.num_programs(1) - 1)
    def _():
        def body(c, carry):
            r0 = pl.multiple_of(c * chunk, chunk)
            f = o_ref[pl.ds(r0, chunk), :]
            o_ref[pl.ds(r0, chunk), :] = h_ref[pl.ds(r0, chunk), :] + _rmsnorm_rows(f, gpost_ref[...])
            return carry
        lax.fori_loop(0, n_chunks, body, 0)


def _ffn(h2d, gpre, w_up, conv_w, conv_b, w_down, gpost, layer, *, tm, tf, seq):
    t, d = h2d.shape
    d_ff = w_down.shape[1]
    assert t % tm == 0 and seq % tm == 0 and d_ff % tf == 0 and tm % CONV_HALO == 0
    nj = d_ff // tf
    halo_blocks = tm // CONV_HALO
    return pl.pallas_call(
        functools.partial(_ffn_kernel, tiles_per_seq=seq // tm, chunk=min(tm, 128)),
        grid=(t // tm, nj),
        in_specs=[
            pl.BlockSpec((tm, d), lambda i, j: (i, 0)),
            pl.BlockSpec((CONV_HALO, d), lambda i, j: (jnp.maximum(i * halo_blocks - 1, 0), 0)),
            _layer_rows(layer, d),
            pl.BlockSpec((None, d, tf), lambda i, j: (layer, 0, j)),
            pl.BlockSpec((None, d, tf), lambda i, j: (layer, 0, nj + j)),
            pl.BlockSpec((None, CONV_WIDTH, tf), lambda i, j: (layer, 0, j)),
            pl.BlockSpec((None, CONV_WIDTH, tf), lambda i, j: (layer, 0, nj + j)),
            pl.BlockSpec((None, 1, tf), lambda i, j: (layer, 0, j)),
            pl.BlockSpec((None, 1, tf), lambda i, j: (layer, 0, nj + j)),
            pl.BlockSpec((None, tf, d), lambda i, j: (layer, j, 0)),
            _layer_rows(layer, d),
        ],
        out_specs=pl.BlockSpec((tm, d), lambda i, j: (i, 0)),
        out_shape=jax.ShapeDtypeStruct((t, d), _F32),
        scratch_shapes=[pltpu.VMEM((CONV_HALO + tm, d), _BF16)],
        compiler_params=_params(("parallel", "arbitrary")),
        name="conv_ffn",
    )(h2d, h2d, gpre, w_up, w_up, conv_w, conv_w, conv_b, conv_b, w_down, gpost)


def _ple_kernel(h_ref, p_ref, wg_ref, wi_ref, o_ref):
    h = h_ref[...]
    gate = jax.nn.sigmoid(jnp.dot(h.astype(_BF16), wg_ref[...], preferred_element_type=_F32))
    e = jnp.dot(p_ref[...].astype(_BF16), wi_ref[...], preferred_element_type=_F32)
    o_ref[...] = h + gate * e


def _ple(h2d, p3d, w_gate, w_in, layer, *, tm):
    t, d = h2d.shape
    pd = p3d.shape[2]
    assert t % tm == 0
    resident = lambda w: pl.BlockSpec((None,) + w.shape[1:], lambda i: (layer, 0, 0), pipeline_mode=pl.Buffered(1))
    return pl.pallas_call(
        _ple_kernel,
        grid=(t // tm,),
        in_specs=[
            pl.BlockSpec((tm, d), lambda i: (i, 0)),
            pl.BlockSpec((None, tm, pd), lambda i: (layer, i, 0)),
            resident(w_gate),
            resident(w_in),
        ],
        out_specs=pl.BlockSpec((tm, d), lambda i: (i, 0)),
        out_shape=jax.ShapeDtypeStruct((t, d), _F32),
        compiler_params=_params(("parallel",)),
        name="layer_embedding",
    )(h2d, p3d, w_gate, w_in)


def _tiles(t, seq, d, d_in, d_ff):
    def largest(n, cands):
        return next(c for c in cands if n % c == 0)
    return dict(
        proj_tm=largest(t, (1024, 512, 256, 128)),
        proj_tn=largest(math.gcd(d_in, QKV_WIDTH), (2048, 1024, 512, 256, 128)),
        stick_tq=largest(seq, (256, 128)),
        merge_tm=largest(t, (512, 256, 128)),
        ffn_tm=largest(seq, (1024, 512, 256, 128)),
        ffn_tf=largest(d_ff, (512, 256, 128)),
        ple_tm=largest(t, (1024, 512, 256, 128)),
    )


def kernel(x, p, g_mix_pre, w_in, w_branch_a, w_branch_b, w_out, g_mix_post, g_ffn_pre, w_up, conv_w,
           conv_b, w_down, g_ffn_post, w_ple_in, w_ple_gate):
    b, s, d = x.shape
    depth = w_in.shape[0]
    d_in = w_in.shape[2]
    d_ff = w_down.shape[1]
    t = b * s
    assert d_in == QKV_WIDTH + 2 * d
    tiles = _tiles(t, s, d, d_in, d_ff)

    q_scale = 1.0 / math.sqrt(HEAD_DIM)
    col = jnp.arange(d_in)
    is_q = (col < WIDTH_A) | ((col >= 3 * WIDTH_A) & (col < 3 * WIDTH_A + WIDTH_B))
    colscale = jnp.where(is_q, q_scale, 1.0).astype(_F32)[None, :]

    bf16 = lambda w: w.astype(_BF16)
    rows = lambda v: v[:, None, :]
    w_in, w_branch_a, w_branch_b, w_out = bf16(w_in), bf16(w_branch_a), bf16(w_branch_b), bf16(w_out)
    w_up, w_down, w_ple_in, w_ple_gate = bf16(w_up), bf16(w_down), bf16(w_ple_in), bf16(w_ple_gate)
    p3d = p.reshape(depth, t, p.shape[-1])

    h = x.reshape(t, d)
    for i in range(depth):
        proj = _in_proj(h, rows(g_mix_pre), colscale, w_in, i, tm=tiles["proj_tm"], tn=tiles["proj_tn"])
        proj3 = proj.reshape(b, s, d_in)
        ya = _dilated_attention(proj3).reshape(t, A_OUT)
        yb = _stick_attention(proj3, tq=tiles["stick_tq"], heads=N_HEADS_B).reshape(t, WIDTH_B)
        h = _merge(ya, yb, proj, h, w_branch_a, w_branch_b, w_out, rows(g_mix_post), i, tm=tiles["merge_tm"])
        h = _ffn(h, rows(g_ffn_pre), w_up, conv_w, rows(conv_b), w_down, rows(g_ffn_post), i,
                 tm=tiles["ffn_tm"], tf=tiles["ffn_tf"], seq=s)
        h = _ple(h, p3d, w_ple_gate, w_ple_in, i, tm=tiles["ple_tm"])
    return h.reshape(b, s, d)
```

```python
import functools
import math

import jax
import jax.numpy as jnp
from jax import lax
from jax.experimental import pallas as pl
from jax.experimental.pallas import tpu as pltpu

HEAD_DIM = 128
DIL_GROUPS = ((128, 1), (512, 4), (2048, 16))
NEAR_GROUPS = DIL_GROUPS[:-1]
HEADS_PER_GROUP = 4
N_HEADS_A = HEADS_PER_GROUP * len(DIL_GROUPS)
N_HEADS_B = 4
WIDTH_A = N_HEADS_A * HEAD_DIM
WIDTH_B = N_HEADS_B * HEAD_DIM
A_OUT = HEADS_PER_GROUP * HEAD_DIM
QKV_WIDTH = 3 * WIDTH_A + 3 * WIDTH_B
CONV_WIDTH = 3
RMS_EPS = 1e-6
ALIBI_MAX = 8.0
MASKED = -1e30
EXP_UNDERFLOW = -110.0
QBLK = 128
SPAN = 2048
CONV_HALO = 16
VMEM_LIMIT = 56 * 1024 * 1024

_F32 = jnp.float32
_BF16 = jnp.bfloat16


def _rmsnorm_rows(x, g):
    ms = jnp.mean(x * x, axis=-1, keepdims=True)
    return x * lax.rsqrt(ms + RMS_EPS) * g


def _params(sem):
    return pltpu.CompilerParams(dimension_semantics=sem, vmem_limit_bytes=VMEM_LIMIT)


def _layer_rows(layer, width):
    return pl.BlockSpec((None, 1, width), lambda *_: (layer, 0, 0))


def _in_proj_kernel(x_ref, g_ref, cs_ref, w_ref, o_ref, u_ref, *, gate_tile0, chunk):
    j = pl.program_id(1)
    tm = x_ref.shape[0]

    @pl.when(j == 0)
    def _():
        def body(c, carry):
            r0 = pl.multiple_of(c * chunk, chunk)
            x = x_ref[pl.ds(r0, chunk), :]
            u_ref[pl.ds(r0, chunk), :] = _rmsnorm_rows(x, g_ref[...]).astype(u_ref.dtype)
            return carry
        lax.fori_loop(0, tm // chunk, body, 0)

    acc = jnp.dot(u_ref[...], w_ref[...], preferred_element_type=_F32)

    o_ref[...] = jnp.where(j >= gate_tile0, jax.nn.sigmoid(acc), acc * cs_ref[...]).astype(o_ref.dtype)


def _in_proj(h2d, g, colscale, w, layer, *, tm, tn):
    t, d = h2d.shape
    d_in = w.shape[2]
    assert t % tm == 0 and d_in % tn == 0 and QKV_WIDTH % tn == 0
    return pl.pallas_call(
        functools.partial(_in_proj_kernel, gate_tile0=QKV_WIDTH // tn, chunk=min(tm, 128)),
        grid=(t // tm, d_in // tn),
        in_specs=[
            pl.BlockSpec((tm, d), lambda i, j: (i, 0)),
            _layer_rows(layer, d),
            pl.BlockSpec((1, tn), lambda i, j: (0, j)),
            pl.BlockSpec((None, d, tn), lambda i, j: (layer, 0, j)),
        ],
        out_specs=pl.BlockSpec((tm, tn), lambda i, j: (i, j)),
        out_shape=jax.ShapeDtypeStruct((t, d_in), _BF16),
        scratch_shapes=[pltpu.VMEM((tm, d), _BF16)],
        compiler_params=_params(("parallel", "arbitrary")),
        name="in_proj",
    )(h2d, g, colscale, w)


def _dilated_kernel(q0, q1, kp0, kp1, kc0, kc1, vp0, vp1, vc0, vc1, b0, b1,
                    fq, fkp, fkc, fvp, fvc, fbias, o_ref):
    i = pl.program_id(2)
    q_refs, bias_refs = (q0, q1), (b0, b1)
    k_refs, v_refs = ((kp0, kc0), (kp1, kc1)), ((vp0, vc0), (vp1, vc1))

    residue_major = lambda ref: pltpu.einshape("mrd->rmd", ref[0])
    token_major = lambda rows: pltpu.einshape("rmd->mrd", jnp.stack(rows)).reshape(SPAN, HEAD_DIM)

    def far_scores():
        qf = residue_major(fq)
        kf = jnp.concatenate([residue_major(fkp), residue_major(fkc)], axis=1)
        kj = lax.broadcasted_iota(jnp.int32, (QBLK, 2 * QBLK), 1)
        in_sequence = kj >= jnp.where(i == 0, QBLK, 0)
        return [jnp.where(in_sequence,
                          lax.dot_general(qf[r], kf[r], (((1,), (1,)), ((), ())), preferred_element_type=_F32)
                          + fbias[0], MASKED) for r in range(qf.shape[0])]

    def far_outputs(scores):
        vf = jnp.concatenate([residue_major(fvp), residue_major(fvc)], axis=1)
        outs, lses = [], []
        for r, sc in enumerate(scores):
            m = jnp.max(sc, axis=-1, keepdims=True)
            p = jnp.exp(sc - m)
            denom = jnp.sum(p, axis=-1, keepdims=True)
            outs.append(jnp.dot(p.astype(_BF16), vf[r], preferred_element_type=_F32) / denom)
            lses.append(jnp.broadcast_to(m + jnp.log(denom), (QBLK, HEAD_DIM)))
        return token_major(outs), token_major(lses)

    def window_rows(prev_cur, o, window):
        prev_ref, cur_ref = prev_cur
        n_prev = max(window - o, 0)
        cur = cur_ref[0, max(o - window, 0):o + QBLK, :]
        if n_prev == 0:
            return cur, 0
        return jnp.concatenate([prev_ref[0, SPAN - n_prev:SPAN, :], cur], axis=0), n_prev

    def masked_scores(o):
        scores = []
        for g, (window, _) in enumerate(NEAR_GROUPS):
            kk, n_prev = window_rows(k_refs[g], o, window)
            sc = lax.dot_general(q_refs[g][0, o:o + QBLK, :], kk, (((1,), (1,)), ((), ())),
                                 preferred_element_type=_F32)
            sc = sc + bias_refs[g][0]
            if n_prev:
                kj = lax.broadcasted_iota(jnp.int32, sc.shape, 1)
                sc = jnp.where(jnp.logical_or(kj >= n_prev, i > 0), sc, MASKED)
            scores.append(sc)
        return scores

    def joint_softmax_weights(scores, far_lse):
        m = functools.reduce(jnp.maximum, [jnp.max(sc, axis=-1, keepdims=True) for sc in scores] + [far_lse])
        ps = [jnp.exp(sc - m) for sc in scores]
        far_weight = jnp.exp(far_lse - m)
        denom = functools.reduce(jnp.add, [jnp.sum(p, axis=-1, keepdims=True) for p in ps] + [far_weight])
        return [p.astype(_BF16) for p in ps], far_weight, denom

    def weighted_values(ps, far_weight, far_rows, o):
        acc = far_weight * far_rows
        for g, (window, _) in enumerate(NEAR_GROUPS):
            acc = acc + jnp.dot(ps[g], window_rows(v_refs[g], o, window)[0], preferred_element_type=_F32)
        return acc

    offs = range(0, SPAN, QBLK)
    far_staged = far_scores()
    staged = [masked_scores(o) for o in offs]
    far_out, far_lse = far_outputs(far_staged)
    weights = [joint_softmax_weights(scores, far_lse[o:o + QBLK, 0:1]) for o, scores in zip(offs, staged)]
    for o, (ps, far_weight, denom) in zip(offs, weights):
        y = weighted_values(ps, far_weight, far_out[o:o + QBLK, :], o) / denom
        o_ref[0, o:o + QBLK, :] = y.astype(o_ref.dtype)


def _alibi_slopes(group):
    slopes = jnp.exp2(-ALIBI_MAX * jnp.arange(1, N_HEADS_A + 1, dtype=_F32) / N_HEADS_A)
    return slopes[group * HEADS_PER_GROUP:(group + 1) * HEADS_PER_GROUP]


def _dilated_bias_tables():
    tables = []
    for g, (window, dilation) in enumerate(NEAR_GROUPS):
        qi = jnp.arange(QBLK)[:, None]
        kj = jnp.arange(window + QBLK)[None, :]
        dist = qi + window - kj
        valid = (dist >= 0) & (dist <= window) & (dist % dilation == 0)
        bias = -_alibi_slopes(g)[:, None, None] * dist.astype(_F32)[None]
        tables.append(jnp.where(valid[None], bias, MASKED))
    return tables


def _far_bias_table():
    window, dilation = DIL_GROUPS[-1]
    qi = jnp.arange(QBLK)[:, None]
    kj = jnp.arange(2 * QBLK)[None, :]
    steps = QBLK + qi - kj
    valid = (steps >= 0) & (steps <= window // dilation)
    bias = -_alibi_slopes(len(DIL_GROUPS) - 1)[:, None, None] * (steps * dilation).astype(_F32)[None]
    return jnp.where(valid[None], bias, MASKED)


def _dilated_attention(proj3):
    b, s, d_in = proj3.shape
    far_window, far_dilation = DIL_GROUPS[-1]
    assert s % SPAN == 0 and SPAN == QBLK * far_dilation == far_window
    n_heads = N_HEADS_A
    far_group = len(DIL_GROUPS) - 1

    def col_spec(base, lag):
        return [pl.BlockSpec((1, SPAN, HEAD_DIM), (lambda bi, hh, i, c=base + g * HEADS_PER_GROUP:
                                                   (bi, jnp.maximum(i - lag, 0), c + hh)))
                for g in range(len(NEAR_GROUPS))]

    def far_spec(base, lag):
        return pl.BlockSpec((1, QBLK, far_dilation, HEAD_DIM),
                            lambda bi, hh, i, c=base + far_group * HEADS_PER_GROUP:
                            (bi, jnp.maximum(i - lag, 0), 0, c + hh))

    head_table = lambda table: pl.BlockSpec((1,) + table.shape[1:], lambda bi, hh, i: (hh, 0, 0))
    tables = _dilated_bias_tables() + [_far_bias_table()]
    near_specs = (col_spec(0, 0) + col_spec(n_heads, 1) + col_spec(n_heads, 0)
                  + col_spec(2 * n_heads, 1) + col_spec(2 * n_heads, 0))
    far_specs = [far_spec(0, 0), far_spec(n_heads, 1), far_spec(n_heads, 0),
                 far_spec(2 * n_heads, 1), far_spec(2 * n_heads, 0)]
    far_view = proj3.reshape(b, s // far_dilation, far_dilation, d_in)
    n_near = len(NEAR_GROUPS)
    return pl.pallas_call(
        _dilated_kernel,
        grid=(b, HEADS_PER_GROUP, s // SPAN),
        in_specs=(near_specs + [head_table(tb) for tb in tables[:n_near]] + far_specs + [head_table(tables[-1])]),
        out_specs=pl.BlockSpec((1, SPAN, HEAD_DIM), lambda bi, hh, i: (bi, i, hh)),
        out_shape=jax.ShapeDtypeStruct((b, s, A_OUT), _BF16),
        compiler_params=_params(("parallel", "parallel", "parallel")),
        name="dilated_attention",
    )(*([proj3] * (5 * n_near)), *tables[:n_near], *([far_view] * 5), tables[-1])


def _stick_kernel(q_ref, k_ref, v_ref, tri_ref, o_ref, *, tq, heads):
    i = pl.program_id(2)
    row = lax.broadcasted_iota(jnp.int32, (tq, tq), 0)
    col = lax.broadcasted_iota(jnp.int32, (tq, tq), 1)
    before = col < row

    def chunk(c, afters, accs):
        k0 = pl.multiple_of(c * tq, tq)
        tri = tri_ref[...]
        lanes = [slice(h * HEAD_DIM, (h + 1) * HEAD_DIM) for h in range(heads)]
        zs = [lax.dot_general(q_ref[0, :, lanes[h]], k_ref[0, pl.ds(k0, tq), lanes[h]],
                              (((1,), (1,)), ((), ())), preferred_element_type=_F32) for h in range(heads)]
        csums = []
        for h in range(heads):
            z = zs[h]
            ln = -(jnp.maximum(z, 0.0) + jnp.log(1.0 + jnp.exp(-jnp.abs(z))))
            hi = ln.astype(_BF16)
            lo = (ln - hi.astype(_F32)).astype(_BF16)
            csums.append(jnp.dot(hi, tri, preferred_element_type=_F32)
                         + jnp.dot(lo, tri, preferred_element_type=_F32))
        new_afters, new_accs = [], []
        for h in range(heads):
            w = jnp.exp(zs[h] + csums[h] + afters[h])
            new_accs.append(accs[h] + jnp.dot(w.astype(_BF16), v_ref[0, pl.ds(k0, tq), lanes[h]],
                                              preferred_element_type=_F32))
            new_afters.append(afters[h] + csums[h][:, 0:1])
        return tuple(new_afters), tuple(new_accs)

    def first_two_chunks():
        has_prev = i > 0
        tri = tri_ref[...]
        lanes = [slice(h * HEAD_DIM, (h + 1) * HEAD_DIM) for h in range(heads)]
        starts = (pl.multiple_of(i * tq, tq), pl.multiple_of(jnp.maximum(i - 1, 0) * tq, tq))
        keep = (before, has_prev)
        zs = [[lax.dot_general(q_ref[0, :, lanes[h]], k_ref[0, pl.ds(k0, tq), lanes[h]],
                               (((1,), (1,)), ((), ())), preferred_element_type=_F32) for h in range(heads)]
              for k0 in starts]
        csums = []
        for n in range(2):
            csums.append([])
            for h in range(heads):
                z = zs[n][h]
                ln = jnp.where(keep[n], -(jnp.maximum(z, 0.0) + jnp.log(1.0 + jnp.exp(-jnp.abs(z)))), 0.0)
                hi = ln.astype(_BF16)
                lo = (ln - hi.astype(_F32)).astype(_BF16)
                csums[n].append(jnp.dot(hi, tri, preferred_element_type=_F32)
                                + jnp.dot(lo, tri, preferred_element_type=_F32))
        afters, accs = [], []
        for h in range(heads):
            after = csums[0][h][:, 0:1]
            w_diag = jnp.where(before, jnp.exp(zs[0][h] + csums[0][h]), 0.0)
            w_prev = jnp.where(has_prev, jnp.exp(zs[1][h] + csums[1][h] + after), 0.0)
            accs.append(jnp.dot(w_diag.astype(_BF16), v_ref[0, pl.ds(starts[0], tq), lanes[h]],
                                preferred_element_type=_F32)
                        + jnp.dot(w_prev.astype(_BF16), v_ref[0, pl.ds(starts[1], tq), lanes[h]],
                                  preferred_element_type=_F32))
            afters.append(after + csums[1][h][:, 0:1])
        return tuple(afters), tuple(accs)

    afters, accs = first_two_chunks()

    def keep_going(carry):
        c, afters, _ = carry
        return jnp.logical_and(c >= 0, jnp.max(functools.reduce(jnp.maximum, afters)) > EXP_UNDERFLOW)

    def body(carry):
        c, afters, accs = carry
        afters, accs = chunk(c, afters, accs)
        return c - 1, afters, accs

    _, _, accs = lax.while_loop(keep_going, body, (i - 2, afters, accs))
    for h in range(heads):
        o_ref[0, :, h * HEAD_DIM:(h + 1) * HEAD_DIM] = accs[h].astype(o_ref.dtype)


def _stick_attention(proj3, *, tq, heads):
    b, s, _ = proj3.shape
    width = heads * HEAD_DIM
    assert s % tq == 0 and N_HEADS_B % heads == 0 and (3 * WIDTH_A) % width == 0
    groups = N_HEADS_B // heads
    base = 3 * WIDTH_A // width
    idx = jnp.arange(tq)
    tri = (idx[:, None] >= idx[None, :]).astype(_BF16)
    whole_seq = lambda part: pl.BlockSpec((1, s, width), lambda bi, hg, i: (bi, 0, base + part * groups + hg),
                                          pipeline_mode=pl.Buffered(1))
    return pl.pallas_call(
        functools.partial(_stick_kernel, tq=tq, heads=heads),
        grid=(b, groups, s // tq),
        in_specs=[
            pl.BlockSpec((1, tq, width), lambda bi, hg, i: (bi, i, base + hg)),
            whole_seq(1),
            whole_seq(2),
            pl.BlockSpec((tq, tq), lambda bi, hg, i: (0, 0)),
        ],
        out_specs=pl.BlockSpec((1, tq, width), lambda bi, hg, i: (bi, i, hg)),
        out_shape=jax.ShapeDtypeStruct((b, s, WIDTH_B), _BF16),
        compiler_params=_params(("parallel", "parallel", "arbitrary")),
        name="stick_breaking_attention",
    )(proj3, proj3, proj3, tri)


def _merge_kernel(ya_ref, yb_ref, ga_ref, gb_ref, h_ref, wa_ref, wb_ref, wo_ref, g_ref, o_ref):
    a = jnp.dot(ya_ref[...], wa_ref[...], preferred_element_type=_F32)
    bb = jnp.dot(yb_ref[...], wb_ref[...], preferred_element_type=_F32)
    m = (ga_ref[...].astype(_F32) * a + gb_ref[...].astype(_F32) * bb).astype(_BF16)
    mix = jnp.dot(m, wo_ref[...], preferred_element_type=_F32)
    o_ref[...] = h_ref[...] + _rmsnorm_rows(mix, g_ref[...])


def _merge(ya, yb, proj, h2d, wa, wb, wo, g, layer, *, tm):
    t, d = h2d.shape
    assert t % tm == 0 and QKV_WIDTH % d == 0
    gate_blk = QKV_WIDTH // d
    resident = lambda w: pl.BlockSpec((None,) + w.shape[1:], lambda i: (layer, 0, 0), pipeline_mode=pl.Buffered(1))
    return pl.pallas_call(
        _merge_kernel,
        grid=(t // tm,),
        in_specs=[
            pl.BlockSpec((tm, A_OUT), lambda i: (i, 0)),
            pl.BlockSpec((tm, WIDTH_B), lambda i: (i, 0)),
            pl.BlockSpec((tm, d), lambda i: (i, gate_blk)),
            pl.BlockSpec((tm, d), lambda i: (i, gate_blk + 1)),
            pl.BlockSpec((tm, d), lambda i: (i, 0)),
            resident(wa), resident(wb), resident(wo), _layer_rows(layer, d),
        ],
        out_specs=pl.BlockSpec((tm, d), lambda i: (i, 0)),
        out_shape=jax.ShapeDtypeStruct((t, d), _F32),
        compiler_params=_params(("parallel",)),
        name="gated_merge",
    )(ya, yb, proj, proj, h2d, wa, wb, wo, g)


def _ffn_kernel(h_ref, hp_ref, gpre_ref, wg_ref, wv_ref, cwg_ref, cwv_ref, cbg_ref, cbv_ref, wd_ref,
                gpost_ref, o_ref, u_ref, *, tiles_per_seq, chunk):
    i = pl.program_id(0)
    j = pl.program_id(1)
    tm = h_ref.shape[0]
    n_chunks = tm // chunk

    @pl.when(j == 0)
    def _():
        halo = _rmsnorm_rows(hp_ref[...], gpre_ref[...])
        at_sequence_start = (i % tiles_per_seq) == 0
        u_ref[0:CONV_HALO, :] = jnp.where(at_sequence_start, 0.0, halo).astype(u_ref.dtype)
        o_ref[...] = jnp.zeros(o_ref.shape, o_ref.dtype)

        def body(c, carry):
            r0 = pl.multiple_of(c * chunk, chunk)
            x = h_ref[pl.ds(r0, chunk), :]
            u_ref[pl.ds(CONV_HALO + r0, chunk), :] = _rmsnorm_rows(x, gpre_ref[...]).astype(u_ref.dtype)
            return carry
        lax.fori_loop(0, n_chunks, body, 0)

    u = u_ref[...]

    def conv_branch(w_ref, cw_ref, cb_ref):
        x = jnp.dot(u, w_ref[...], preferred_element_type=_F32)
        cw = cw_ref[...]
        y = cb_ref[...] + cw[0:1, :] * pltpu.roll(x, 2, 0)
        y = y + cw[1:2, :] * pltpu.roll(x, 1, 0)
        y = y + cw[2:3, :] * x
        return y[CONV_HALO:, :]

    gate = conv_branch(wg_ref, cwg_ref, cbg_ref)
    value = conv_branch(wv_ref, cwv_ref, cbv_ref)
    act = (jax.nn.gelu(gate, approximate=True) * value).astype(_BF16)
    o_ref[...] += jnp.dot(act, wd_ref[...], preferred_element_type=_F32)

    @pl.when(j == pl.num_programs(1) - 1)
    def _():
        def body(c, carry):
            r0 = pl.multiple_of(c * chunk, chunk)
            f = o_ref[pl.ds(r0, chunk), :]
            o_ref[pl.ds(r0, chunk), :] = h_ref[pl.ds(r0, chunk), :] + _rmsnorm_rows(f, gpost_ref[...])
            return carry
        lax.fori_loop(0, n_chunks, body, 0)


def _ffn(h2d, gpre, w_up, conv_w, conv_b, w_down, gpost, layer, *, tm, tf, seq):
    t, d = h2d.shape
    d_ff = w_down.shape[1]
    assert t % tm == 0 and seq % tm == 0 and d_ff % tf == 0 and tm % CONV_HALO == 0
    nj = d_ff // tf
    halo_blocks = tm // CONV_HALO
    return pl.pallas_call(
        functools.partial(_ffn_kernel, tiles_per_seq=seq // tm, chunk=min(tm, 128)),
        grid=(t // tm, nj),
        in_specs=[
            pl.BlockSpec((tm, d), lambda i, j: (i, 0)),
            pl.BlockSpec((CONV_HALO, d), lambda i, j: (jnp.maximum(i * halo_blocks - 1, 0), 0)),
            _layer_rows(layer, d),
            pl.BlockSpec((None, d, tf), lambda i, j: (layer, 0, j)),
            pl.BlockSpec((None, d, tf), lambda i, j: (layer, 0, nj + j)),
            pl.BlockSpec((None, CONV_WIDTH, tf), lambda i, j: (layer, 0, j)),
            pl.BlockSpec((None, CONV_WIDTH, tf), lambda i, j: (layer, 0, nj + j)),
            pl.BlockSpec((None, 1, tf), lambda i, j: (layer, 0, j)),
            pl.BlockSpec((None, 1, tf), lambda i, j: (layer, 0, nj + j)),
            pl.BlockSpec((None, tf, d), lambda i, j: (layer, j, 0)),
            _layer_rows(layer, d),
        ],
        out_specs=pl.BlockSpec((tm, d), lambda i, j: (i, 0)),
        out_shape=jax.ShapeDtypeStruct((t, d), _F32),
        scratch_shapes=[pltpu.VMEM((CONV_HALO + tm, d), _BF16)],
        compiler_params=_params(("parallel", "arbitrary")),
        name="conv_ffn",
    )(h2d, h2d, gpre, w_up, w_up, conv_w, conv_w, conv_b, conv_b, w_down, gpost)


def _ple_kernel(h_ref, p_ref, wg_ref, wi_ref, o_ref):
    h = h_ref[...]
    gate = jax.nn.sigmoid(jnp.dot(h.astype(_BF16), wg_ref[...], preferred_element_type=_F32))
    e = jnp.dot(p_ref[...].astype(_BF16), wi_ref[...], preferred_element_type=_F32)
    o_ref[...] = h + gate * e


def _ple(h2d, p3d, w_gate, w_in, layer, *, tm):
    t, d = h2d.shape
    pd = p3d.shape[2]
    assert t % tm == 0
    resident = lambda w: pl.BlockSpec((None,) + w.shape[1:], lambda i: (layer, 0, 0), pipeline_mode=pl.Buffered(1))
    return pl.pallas_call(
        _ple_kernel,
        grid=(t // tm,),
        in_specs=[
            pl.BlockSpec((tm, d), lambda i: (i, 0)),
            pl.BlockSpec((None, tm, pd), lambda i: (layer, i, 0)),
            resident(w_gate),
            resident(w_in),
        ],
        out_specs=pl.BlockSpec((tm, d), lambda i: (i, 0)),
        out_shape=jax.ShapeDtypeStruct((t, d), _F32),
        compiler_params=_params(("parallel",)),
        name="layer_embedding",
    )(h2d, p3d, w_gate, w_in)


def _tiles(t, seq, d, d_in, d_ff):
    def largest(n, cands):
        return next(c for c in cands if n % c == 0)
    return dict(
        proj_tm=largest(t, (1024, 512, 256, 128)),
        proj_tn=largest(math.gcd(d_in, QKV_WIDTH), (2048, 1024, 512, 256, 128)),
        stick_tq=largest(seq, (256, 128)),
        merge_tm=largest(t, (512, 256, 128)),
        ffn_tm=largest(seq, (1024, 512, 256, 128)),
        ffn_tf=largest(d_ff, (512, 256, 128)),
        ple_tm=largest(t, (1024, 512, 256, 128)),
    )


def kernel(x, p, g_mix_pre, w_in, w_branch_a, w_branch_b, w_out, g_mix_post, g_ffn_pre, w_up, conv_w,
           conv_b, w_down, g_ffn_post, w_ple_in, w_ple_gate):
    b, s, d = x.shape
    depth = w_in.shape[0]
    d_in = w_in.shape[2]
    d_ff = w_down.shape[1]
    t = b * s
    assert d_in == QKV_WIDTH + 2 * d
    tiles = _tiles(t, s, d, d_in, d_ff)

    q_scale = 1.0 / math.sqrt(HEAD_DIM)
    col = jnp.arange(d_in)
    is_q = (col < WIDTH_A) | ((col >= 3 * WIDTH_A) & (col < 3 * WIDTH_A + WIDTH_B))
    colscale = jnp.where(is_q, q_scale, 1.0).astype(_F32)[None, :]

    bf16 = lambda w: w.astype(_BF16)
    rows = lambda v: v[:, None, :]
    w_in, w_branch_a, w_branch_b, w_out = bf16(w_in), bf16(w_branch_a), bf16(w_branch_b), bf16(w_out)
    w_up, w_down, w_ple_in, w_ple_gate = bf16(w_up), bf16(w_down), bf16(w_ple_in), bf16(w_ple_gate)
    p3d = p.reshape(depth, t, p.shape[-1])

    h = x.reshape(t, d)
    for i in range(depth):
        proj = _in_proj(h, rows(g_mix_pre), colscale, w_in, i, tm=tiles["proj_tm"], tn=tiles["proj_tn"])
        proj3 = proj.reshape(b, s, d_in)
        ya = _dilated_attention(proj3).reshape(t, A_OUT)
        yb = _stick_attention(proj3, tq=tiles["stick_tq"], heads=N_HEADS_B).reshape(t, WIDTH_B)
        h = _merge(ya, yb, proj, h, w_branch_a, w_branch_b, w_out, rows(g_mix_post), i, tm=tiles["merge_tm"])
        h = _ffn(h, rows(g_ffn_pre), w_up, conv_w, rows(conv_b), w_down, rows(g_ffn_post), i,
                 tm=tiles["ffn_tm"], tf=tiles["ffn_tf"], seq=s)
        h = _ple(h, p3d, w_ple_gate, w_ple_in, i, tm=tiles["ple_tm"])
    return h.reshape(b, s, d)
```
